```python
import jax, jax.numpy as jnp
from jax import lax
import numpy as np

D_MODEL = 1024
BATCH = 8
SEQ = 2048
DEPTH = 2

PLE_DIM = 256
HEAD_DIM = 64
SGU_HEADS = 4
SGU_WIDTH = SGU_HEADS * HEAD_DIM
CHUNK = 128
CONV_GROUPS = 4
CONV_WIDTH = CONV_GROUPS * HEAD_DIM
CONV_KERNEL = 31
MLA_HEADS = 8
MLA_NOPE = 64
MLA_ROPE = 32
MLA_QK = MLA_NOPE + MLA_ROPE
MLA_V = 64
MLA_WIDTH = MLA_HEADS * MLA_V
Q_RANK = 256
KV_RANK = 128
ROPE_THETA = 10000.0
Q_BLOCK = 128
MIX_WIDTH = SGU_WIDTH + CONV_WIDTH + MLA_WIDTH
IN_WIDTH = 2 * SGU_WIDTH + 2 * CONV_WIDTH + Q_RANK + KV_RANK + MLA_ROPE
IN_SPLITS = [2 * SGU_WIDTH, 2 * SGU_WIDTH + 2 * CONV_WIDTH,
             2 * SGU_WIDTH + 2 * CONV_WIDTH + Q_RANK,
             2 * SGU_WIDTH + 2 * CONV_WIDTH + Q_RANK + KV_RANK]
N_GROUPS = 4
EXPERTS_PER_GROUP = 8
N_EXPERTS = N_GROUPS * EXPERTS_PER_GROUP
D_EXPERT = 256
TOP_K = 2
EXPERT_BLOCK = 128
MAX_START = 4096
EPS = 1e-6

kernel_name = "hymba_sgu_conformer_mla_hmoe"


def rmsnorm(x, g):
    xf = x.astype(jnp.float32)
    y = xf * lax.rsqrt(jnp.mean(xf * xf, axis=-1, keepdims=True) + EPS)
    return (y * g.astype(jnp.float32)).astype(x.dtype)


def layernorm(x, g, b):
    xf = x.astype(jnp.float32)
    mu = jnp.mean(xf, axis=-1, keepdims=True)
    var = jnp.mean(jnp.square(xf - mu), axis=-1, keepdims=True)
    y = (xf - mu) * lax.rsqrt(var + EPS)
    return (y * g.astype(jnp.float32) + b.astype(jnp.float32)).astype(x.dtype)


def rope(x, cos, sin):
    xf = x.astype(jnp.float32)
    x1, x2 = jnp.split(xf, 2, axis=-1)
    return jnp.concatenate([x1 * cos - x2 * sin, x2 * cos + x1 * sin], axis=-1).astype(x.dtype)


def sgu_mixer(z, ln_g, ln_b, w_s, b_s):
    B, S, _ = z.shape
    z = jax.nn.gelu(z)
    u, v = jnp.split(z, 2, axis=-1)
    v = layernorm(v.reshape(B, S, SGU_HEADS, HEAD_DIM),
                  ln_g.reshape(SGU_HEADS, HEAD_DIM), ln_b.reshape(SGU_HEADS, HEAD_DIM))
    v = v.reshape(B, S // CHUNK, CHUNK, SGU_HEADS, HEAD_DIM)
    causal = jnp.tril(jnp.ones((CHUNK, CHUNK), w_s.dtype))
    s = jnp.einsum('hij,bnjhd->bnihd', w_s * causal, v)
    s = s + jnp.transpose(b_s)[None, None, :, :, None]
    return u * s.reshape(B, S, SGU_WIDTH)


def conv_mixer(z, conv_w, conv_b, ln_g, ln_b, pw_w, pw_b):
    B, S, _ = z.shape
    a, g = jnp.split(z, 2, axis=-1)
    y = a * jax.nn.sigmoid(g)
    y = lax.conv_general_dilated(
        y, conv_w[:, None, :].astype(y.dtype), window_strides=(1,),
        padding=[(CONV_KERNEL - 1, 0)], dimension_numbers=('NWC', 'WIO', 'NWC'),
        feature_group_count=CONV_WIDTH) + conv_b
    y = layernorm(y.reshape(B, S, CONV_GROUPS, HEAD_DIM),
                  ln_g.reshape(CONV_GROUPS, HEAD_DIM), ln_b.reshape(CONV_GROUPS, HEAD_DIM))
    y = jax.nn.silu(y.reshape(B, S, CONV_WIDTH))
    return y @ pw_w + pw_b


def causal_block_attention(q, k, v):
    B, S, H, _ = q.shape
    nb = S // Q_BLOCK
    qb = jnp.transpose(q.reshape(B, nb, Q_BLOCK, H, MLA_QK), (1, 0, 2, 3, 4))
    kpos = jnp.arange(S)
    scale = MLA_QK ** -0.5

    def block(args):
        qi, bi = args
        s = jnp.einsum('bqhd,bkhd->bhqk', qi, k, preferred_element_type=jnp.float32) * scale
        qpos = bi * Q_BLOCK + jnp.arange(Q_BLOCK)
        s = jnp.where(kpos[None, :] <= qpos[:, None], s, -1e30)
        pr = jax.nn.softmax(s, axis=-1).astype(v.dtype)
        return jnp.einsum('bhqk,bkhd->bqhd', pr, v)

    o = lax.map(block, (qb, jnp.arange(nb)))
    return jnp.transpose(o, (1, 0, 2, 3, 4)).reshape(B, S, H * MLA_V)


def mla_mixer(c_q, c_kv, k_pe, cos, sin, qa_g, w_uq, kva_g, w_ukv, qn_g, kn_g):
    B, S, _ = c_q.shape
    q = (rmsnorm(c_q, qa_g) @ w_uq).reshape(B, S, MLA_HEADS, MLA_QK)
    kv = (rmsnorm(c_kv, kva_g) @ w_ukv).reshape(B, S, MLA_HEADS, MLA_NOPE + MLA_V)
    q_nope = rmsnorm(q[..., :MLA_NOPE], qn_g[:MLA_NOPE])
    q_pe = rope(rmsnorm(q[..., MLA_NOPE:], qn_g[MLA_NOPE:]), cos, sin)
    k_nope = rmsnorm(kv[..., :MLA_NOPE], kn_g[:MLA_NOPE])
    v = kv[..., MLA_NOPE:]
    k_pe = rope(rmsnorm(k_pe, kn_g[MLA_NOPE:])[:, :, None, :], cos, sin)
    q = jnp.concatenate([q_nope, q_pe], axis=-1)
    k = jnp.concatenate([k_nope, jnp.broadcast_to(k_pe, (B, S, MLA_HEADS, MLA_ROPE))], axis=-1)
    return causal_block_attention(q, k, v)


def hier_moe(m, wg, bg, we, be, w_gate, w_up, w_down):
    B, S, D = m.shape
    t = m.reshape(-1, D)
    N = t.shape[0]
    glog = (t @ wg + bg).astype(jnp.float32)
    gprob = jax.nn.softmax(glog, axis=-1)
    g_idx = jnp.argmax(glog, axis=-1)
    g_w = jnp.take_along_axis(gprob, g_idx[:, None], axis=1)[:, 0]
    elog = (t @ we + be).astype(jnp.float32).reshape(N, N_GROUPS, EXPERTS_PER_GROUP)
    elog = jnp.take_along_axis(elog, g_idx[:, None, None], axis=1)[:, 0, :]
    eprob = jax.nn.softmax(elog, axis=-1)
    topv, topi = lax.top_k(eprob, TOP_K)
    topv = topv / jnp.sum(topv, axis=-1, keepdims=True)

    nk = N * TOP_K
    flat_e = (g_idx[:, None] * EXPERTS_PER_GROUP + topi).reshape(-1)
    flat_tok = jnp.repeat(jnp.arange(N, dtype=jnp.int32), TOP_K)
    flat_w = (g_w[:, None] * topv).reshape(-1)
    order = jnp.argsort(flat_e)
    e_s, tok_s, w_s = flat_e[order], flat_tok[order], flat_w[order]
    counts = jnp.bincount(flat_e, length=N_EXPERTS)
    start = jnp.cumsum(counts) - counts
    padded = (counts + EXPERT_BLOCK - 1) // EXPERT_BLOCK * EXPERT_BLOCK
    pend = jnp.cumsum(padded)
    pstart = pend - padded
    dest = pstart[e_s] + (jnp.arange(nk) - start[e_s])
    P = nk + N_EXPERTS * EXPERT_BLOCK
    nb = P // EXPERT_BLOCK
    buf_tok = jnp.zeros((P,), jnp.int32).at[dest].set(tok_s)
    buf_w = jnp.zeros((P,), t.dtype).at[dest].set(w_s.astype(t.dtype))
    blk_e = jnp.minimum(jnp.searchsorted(pend, jnp.arange(nb) * EXPERT_BLOCK, side='right'),
                        N_EXPERTS - 1)

    def expert_block(args):
        tok_b, w_b, e = args
        xb = t[tok_b]
        hb = jax.nn.silu(xb @ w_gate[e]) * (xb @ w_up[e])
        return (hb @ w_down[e]) * w_b[:, None]

    y = lax.map(expert_block, (buf_tok.reshape(nb, EXPERT_BLOCK),
                               buf_w.reshape(nb, EXPERT_BLOCK), blk_e))
    out = jnp.zeros((N, D), t.dtype).at[buf_tok].add(y.reshape(P, D))
    return out.reshape(B, S, D)


def setup_inputs(seed: int = 0) -> dict:
    key = jax.random.key(seed)
    ks = iter(jax.random.split(key, 48))
    L = DEPTH

    def nrm(shape, scale):
        return jax.random.normal(next(ks), shape, jnp.float32) * scale

    def gain(shape):
        return 1.0 + nrm(shape, 0.01)

    x = nrm((BATCH, SEQ, D_MODEL), 1.0)
    p = nrm((DEPTH, BATCH, SEQ, PLE_DIM), 1.0)
    positions = (jax.random.randint(next(ks), (BATCH, 1), 0, MAX_START, dtype=jnp.int32)
                 + jnp.arange(SEQ, dtype=jnp.int32)[None, :])
    return {
        "x": x,
        "p": p,
        "positions": positions,
        "mix_norm_g": gain((L, D_MODEL)),
        "w_in": nrm((L, D_MODEL, IN_WIDTH), D_MODEL ** -0.5),
        "sgu_ln_g": gain((L, SGU_WIDTH)),
        "sgu_ln_b": nrm((L, SGU_WIDTH), 0.01),
        "sgu_w": nrm((L, SGU_HEADS, CHUNK, CHUNK), CHUNK ** -0.5),
        "sgu_b": gain((L, SGU_HEADS, CHUNK)),
        "conv_w": nrm((L, CONV_KERNEL, CONV_WIDTH), CONV_KERNEL ** -0.5),
        "conv_b": nrm((L, CONV_WIDTH), 0.01),
        "conv_ln_g": gain((L, CONV_WIDTH)),
        "conv_ln_b": nrm((L, CONV_WIDTH), 0.01),
        "conv_pw_w": nrm((L, CONV_WIDTH, CONV_WIDTH), CONV_WIDTH ** -0.5),
        "conv_pw_b": nrm((L, CONV_WIDTH), 0.01),
        "q_a_norm_g": gain((L, Q_RANK)),
        "w_uq": nrm((L, Q_RANK, MLA_HEADS * MLA_QK), Q_RANK ** -0.5),
        "kv_a_norm_g": gain((L, KV_RANK)),
        "w_ukv": nrm((L, KV_RANK, MLA_HEADS * (MLA_NOPE + MLA_V)), KV_RANK ** -0.5),
        "q_norm_g": gain((L, MLA_QK)),
        "k_norm_g": gain((L, MLA_QK)),
        "branch_norm_g": gain((L, MIX_WIDTH)),
        "w_o": nrm((L, MIX_WIDTH, D_MODEL), MIX_WIDTH ** -0.5),
        "ffn_norm_g": gain((L, D_MODEL)),
        "router_group_w": nrm((L, D_MODEL, N_GROUPS), D_MODEL ** -0.5),
        "router_group_b": nrm((L, N_GROUPS), 0.01),
        "router_expert_w": nrm((L, D_MODEL, N_EXPERTS), D_MODEL ** -0.5),
        "router_expert_b": nrm((L, N_EXPERTS), 0.01),
        "moe_w_gate": nrm((L, N_EXPERTS, D_MODEL, D_EXPERT), D_MODEL ** -0.5),
        "moe_w_up": nrm((L, N_EXPERTS, D_MODEL, D_EXPERT), D_MODEL ** -0.5),
        "moe_w_down": nrm((L, N_EXPERTS, D_EXPERT, D_MODEL), D_EXPERT ** -0.5),
        "ple_norm_g": gain((L, D_MODEL)),
        "ple_gate_w": nrm((L, D_MODEL, D_MODEL), D_MODEL ** -0.5),
        "ple_proj_w": nrm((L, PLE_DIM, D_MODEL), PLE_DIM ** -0.5),
        "ple_post_norm_g": gain((L, D_MODEL)),
    }


def reference(x, p, positions, mix_norm_g, w_in, sgu_ln_g, sgu_ln_b, sgu_w, sgu_b,
              conv_w, conv_b, conv_ln_g, conv_ln_b, conv_pw_w, conv_pw_b,
              q_a_norm_g, w_uq, kv_a_norm_g, w_ukv, q_norm_g, k_norm_g,
              branch_norm_g, w_o, ffn_norm_g, router_group_w, router_group_b,
              router_expert_w, router_expert_b, moe_w_gate, moe_w_up, moe_w_down,
              ple_norm_g, ple_gate_w, ple_proj_w, ple_post_norm_g):
    inv_freq = ROPE_THETA ** (-jnp.arange(0, MLA_ROPE, 2, dtype=jnp.float32) / MLA_ROPE)
    ang = positions.astype(jnp.float32)[..., None] * inv_freq
    cos = jnp.cos(ang)[:, :, None, :]
    sin = jnp.sin(ang)[:, :, None, :]

    h = x
    for i in range(DEPTH):
        a = rmsnorm(h, mix_norm_g[i])
        z = a @ w_in[i]
        z_sgu, z_conv, c_q, c_kv, k_pe = jnp.split(z, IN_SPLITS, axis=-1)
        y_a = sgu_mixer(z_sgu, sgu_ln_g[i], sgu_ln_b[i], sgu_w[i], sgu_b[i])
        y_b = conv_mixer(z_conv, conv_w[i], conv_b[i], conv_ln_g[i], conv_ln_b[i],
                         conv_pw_w[i], conv_pw_b[i])
        y_c = mla_mixer(c_q, c_kv, k_pe, cos, sin, q_a_norm_g[i], w_uq[i],
                        kv_a_norm_g[i], w_ukv[i], q_norm_g[i], k_norm_g[i])
        bg = branch_norm_g[i]
        y = jnp.concatenate([
            rmsnorm(y_a, bg[:SGU_WIDTH]),
            rmsnorm(y_b, bg[SGU_WIDTH:SGU_WIDTH + CONV_WIDTH]),
            rmsnorm(y_c, bg[SGU_WIDTH + CONV_WIDTH:]),
        ], axis=-1)
        h = h + y @ w_o[i]

        m = rmsnorm(h, ffn_norm_g[i])
        h = h + hier_moe(m, router_group_w[i], router_group_b[i], router_expert_w[i],
                         router_expert_b[i], moe_w_gate[i], moe_w_up[i], moe_w_down[i])

        gate = jax.nn.sigmoid(rmsnorm(h, ple_norm_g[i]) @ ple_gate_w[i])
        e = rmsnorm(p[i] @ ple_proj_w[i], ple_post_norm_g[i])
        h = h + gate * e
    return h
```

```python
import functools

import jax
import jax.numpy as jnp
from jax import lax
from jax.experimental import pallas as pl
from jax.experimental.pallas import tpu as pltpu

F32 = jnp.float32
BF16 = jnp.bfloat16

D_MODEL = 1024
HEAD_DIM = 64
SGU_HEADS = 4
SGU_WIDTH = 256
CHUNK = 128
CONV_WIDTH = 256
CONV_KERNEL = 31
MLA_HEADS = 8
MLA_NOPE = 64
MLA_ROPE = 32
MLA_QK = MLA_NOPE + MLA_ROPE
MLA_V = 64
Q_RANK = 256
KV_RANK = 128
ROPE_THETA = 10000.0
N_GROUPS = 4
EXPERTS_PER_GROUP = 8
N_EXPERTS = 32
D_EXPERT = 256
TOP_K = 2
EPS = 1e-6

LANES = 128
HEAD_PAD = 128
MIX_TILE = 512
CONV_HALO = 32
ATT_TQ = 256
RANK_CHUNK = 512
EXPERT_ROWS = 256
DISPATCH_TILE = 256
COMBINE_TILE = 256
VMEM_LIMIT = 48 * 1024 * 1024
NEG_BIG = -1e30


def _dot(a, b):
    return jnp.dot(a, b, preferred_element_type=F32)


def _rms(x, g):
    ms = jnp.mean(x * x, axis=-1, keepdims=True)
    return x * lax.rsqrt(ms + EPS) * g


def _seg_sum(x, g_ref):
    g = g_ref[...]
    outs = []
    for c in range(x.shape[1] // 256):
        xb = x[:, c * 256:(c + 1) * 256]
        hi = xb.astype(BF16)
        lo = (xb - hi.astype(F32)).astype(BF16)
        outs.append(_dot(hi, g) + _dot(lo, g))
    return outs[0] if len(outs) == 1 else jnp.concatenate(outs, axis=-1)


def _group_layernorm(x, g_ref, gain, bias):
    inv = 1.0 / HEAD_DIM
    mean = _seg_sum(x, g_ref) * inv
    d = x - mean
    var = _seg_sum(d * d, g_ref) * inv
    return d * lax.rsqrt(var + EPS) * gain + bias


def _rope_table_kernel(pos_ref, invf_ref, cos_ref, sin_ref):
    ang = pos_ref[...].astype(F32) * invf_ref[...]
    cos_ref[...] = jnp.cos(ang)
    sin_ref[...] = jnp.sin(ang)


def _rope_tables(positions):
    b, s = positions.shape
    n = b * s
    half = MLA_ROPE // 2
    inv_freq = ROPE_THETA ** (-jnp.arange(0, MLA_ROPE, 2, dtype=F32) / MLA_ROPE)
    pos_rep = jnp.repeat(positions.reshape(n), half).reshape(n * half // LANES, LANES)
    invf_rep = jnp.tile(inv_freq, LANES // half).reshape(1, LANES)
    shape = jax.ShapeDtypeStruct(pos_rep.shape, F32)
    cos, sin = pl.pallas_call(_rope_table_kernel, out_shape=(shape, shape), name="rope_tables")(pos_rep, invf_rep)
    return cos.reshape(n, half), sin.reshape(n, half)


def _mixer_pre_kernel(
        h_ref, rc_ref, rs1_ref, rs2_ref, cost_ref, sint_ref,
        mixg_ref, wsgu_ref, wconv_ref, wcq_ref, wckv_ref,
        slng_ref, slnb_ref, wcat_ref, sbias_ref, g64_ref, bga_ref,
        cw_ref, cb_ref, clng_ref, clnb_ref, pww_ref, pwb_ref, bgb_ref,
        qag_ref, wuq_ref, gqk_ref, qinvc_ref, qgain_ref,
        kvag_ref, wukt_ref, wuv_ref, kng_ref, kpeg_ref,
        ya_ref, yb_ref, q_ref, kt_ref, v_ref,
        ybuf_ref, *, tiles_per_seq):
    t = MIX_TILE
    a = _rms(h_ref[...], mixg_ref[...]).astype(BF16)

    zg = jax.nn.gelu(_dot(a, wsgu_ref[...]))
    u = zg[:, :SGU_WIDTH]
    vn = _group_layernorm(zg[:, SGU_WIDTH:], g64_ref, slng_ref[...], slnb_ref[...])
    lane_head = lax.broadcasted_iota(jnp.int32, (CHUNK, SGU_WIDTH), 1) // HEAD_DIM
    wcat = wcat_ref[...]
    sbias = sbias_ref[...]
    parts = []
    for c in range(t // CHUNK):
        vc = vn[c * CHUNK:(c + 1) * CHUNK]
        stacked = jnp.concatenate(
            [jnp.where(lane_head == hh, vc, 0.0) for hh in range(SGU_HEADS)], axis=0).astype(BF16)
        s = _dot(wcat, stacked) + sbias
        parts.append(u[c * CHUNK:(c + 1) * CHUNK] * s)
    ya_ref[...] = _rms(jnp.concatenate(parts, axis=0), bga_ref[...])

    zc = _dot(a, wconv_ref[...])
    yg = zc[:, :CONV_WIDTH] * jax.nn.sigmoid(zc[:, CONV_WIDTH:])
    first = (pl.program_id(0) % tiles_per_seq) == 0

    @pl.when(first)
    def _():
        ybuf_ref[0:CONV_HALO, :] = jnp.zeros((CONV_HALO, CONV_WIDTH), F32)

    @pl.when(jnp.logical_not(first))
    def _():
        ybuf_ref[0:CONV_HALO, :] = ybuf_ref[t:t + CONV_HALO, :]

    ybuf_ref[CONV_HALO:CONV_HALO + t, :] = yg
    rows = 64
    shift = CONV_HALO - (CONV_KERNEL - 1)
    conv_parts = []
    for r in range(t // rows):
        acc = jnp.broadcast_to(cb_ref[...], (rows, CONV_WIDTH))
        for tap in range(CONV_KERNEL):
            acc = acc + cw_ref[tap:tap + 1, :] * ybuf_ref[pl.ds(r * rows + shift + tap, rows), :]
        conv_parts.append(acc)
    cv = jnp.concatenate(conv_parts, axis=0)
    cn = _group_layernorm(cv, g64_ref, clng_ref[...], clnb_ref[...])
    yb = _dot(jax.nn.silu(cn).astype(BF16), pww_ref[...]) + pwb_ref[...]
    yb_ref[...] = _rms(yb, bgb_ref[...])

    cqn = _rms(_dot(a, wcq_ref[...]), qag_ref[...]).astype(BF16)
    qf = _dot(cqn, wuq_ref[...])
    ss = _seg_sum(qf * qf, gqk_ref)
    qn = qf * lax.rsqrt(ss * qinvc_ref[...] + EPS) * qgain_ref[...]
    rc, rs1, rs2 = rc_ref[...], rs1_ref[...], rs2_ref[...]
    for hh in range(MLA_HEADS):
        blk = qn[:, hh * HEAD_PAD:(hh + 1) * HEAD_PAD]
        half = MLA_ROPE // 2
        rot = blk * rc + pltpu.roll(blk, HEAD_PAD - half, 1) * rs1 + pltpu.roll(blk, half, 1) * rs2
        q_ref[:, hh * HEAD_PAD:(hh + 1) * HEAD_PAD] = rot.astype(BF16)

    zkv = _dot(a, wckv_ref[...])
    ckvn = _rms(zkv[:, :KV_RANK], kvag_ref[...]).astype(BF16)
    v_ref[...] = _dot(ckvn, wuv_ref[...]).astype(BF16)
    knt = lax.dot_general(wukt_ref[...], ckvn, (((1,), (1,)), ((), ())), preferred_element_type=F32)
    x = zkv[:, KV_RANK:].T[0:MLA_ROPE]
    xn = x * lax.rsqrt(jnp.mean(x * x, axis=0, keepdims=True) + EPS) * kpeg_ref[...]
    x1, x2 = xn[:MLA_ROPE // 2], xn[MLA_ROPE // 2:]
    cos, sin = cost_ref[0], sint_ref[0]
    kpe = jnp.concatenate([x1 * cos - x2 * sin, x2 * cos + x1 * sin], axis=0)
    pad = jnp.zeros((HEAD_PAD - MLA_QK, t), F32)
    kng = kng_ref[...]
    for hh in range(MLA_HEADS):
        blk = knt[hh * MLA_NOPE:(hh + 1) * MLA_NOPE]
        kn = blk * lax.rsqrt(jnp.mean(blk * blk, axis=0, keepdims=True) + EPS) * kng
        kt_ref[0, hh * HEAD_PAD:(hh + 1) * HEAD_PAD, :] = jnp.concatenate([kn, kpe, pad], axis=0).astype(BF16)


def _full(shape):
    nd = len(shape)
    return pl.BlockSpec(shape, lambda *_: (0,) * nd)


def _mixer_pre(h, rope, lw, batch, seq):
    n = h.shape[0]
    t = MIX_TILE
    tps = seq // t
    row = lambda w: pl.BlockSpec((t, w), lambda i: (i, 0))
    tspec = pl.BlockSpec((1, MLA_ROPE // 2, t), lambda i: (i // tps, 0, i % tps))
    weights = [lw[k] for k in (
        "mix_g", "w_sgu", "w_conv", "w_cq", "w_ckv",
        "sgu_ln_g", "sgu_ln_b", "sgu_wcat", "sgu_bias", "g64", "bg_a",
        "conv_w", "conv_b", "conv_ln_g", "conv_ln_b", "pw_w", "pw_b", "bg_b",
        "qa_g", "w_uq", "gqk", "q_invc", "q_gain",
        "kva_g", "w_ukt", "w_uv", "kn_g", "kpe_g")]
    in_specs = [row(D_MODEL), row(LANES), row(LANES), row(LANES), tspec, tspec] + [_full(w.shape) for w in weights]
    out_shape = (
        jax.ShapeDtypeStruct((n, SGU_WIDTH), F32),
        jax.ShapeDtypeStruct((n, CONV_WIDTH), F32),
        jax.ShapeDtypeStruct((n, MLA_HEADS * HEAD_PAD), BF16),
        jax.ShapeDtypeStruct((batch, MLA_HEADS * HEAD_PAD, seq), BF16),
        jax.ShapeDtypeStruct((n, MLA_HEADS * MLA_V), BF16),
    )
    out_specs = (
        row(SGU_WIDTH), row(CONV_WIDTH), row(MLA_HEADS * HEAD_PAD),
        pl.BlockSpec((1, MLA_HEADS * HEAD_PAD, t), lambda i: (i // tps, 0, i % tps)),
        row(MLA_HEADS * MLA_V),
    )
    return pl.pallas_call(
        functools.partial(_mixer_pre_kernel, tiles_per_seq=tps),
        grid=(n // t,),
        in_specs=in_specs,
        out_specs=out_specs,
        out_shape=out_shape,
        scratch_shapes=[pltpu.VMEM((t + CONV_HALO, CONV_WIDTH), F32)],
        compiler_params=pltpu.CompilerParams(dimension_semantics=("arbitrary",), vmem_limit_bytes=VMEM_LIMIT),
        name="mixer_pre",
    )(h, rope["rc"], rope["rs1"], rope["rs2"], rope["cos_t"], rope["sin_t"], *weights)


def _attention_kernel(q_ref, kt_ref, v_ref, o_ref, *, seq):
    tq = ATT_TQ
    scale = MLA_QK ** -0.5
    row = lax.broadcasted_iota(jnp.int32, (tq, tq), 0)
    col = lax.broadcasted_iota(jnp.int32, (tq, tq), 1)
    lane = lax.broadcasted_iota(jnp.int32, (tq, 2 * MLA_V), 1)
    for qi in range(seq // tq):
        nk = (qi + 1) * tq
        outs = []
        for hh in range(2):
            q = q_ref[qi * tq:(qi + 1) * tq, hh * HEAD_PAD:(hh + 1) * HEAD_PAD]
            s = _dot(q, kt_ref[0, hh * HEAD_PAD:(hh + 1) * HEAD_PAD, 0:nk]) * scale
            diag = jnp.where(col <= row, s[:, nk - tq:], NEG_BIG)
            s = diag if qi == 0 else jnp.concatenate([s[:, :nk - tq], diag], axis=1)
            p = jnp.exp(s - jnp.max(s, axis=-1, keepdims=True))
            l = jnp.sum(p, axis=-1, keepdims=True)
            outs.append(_dot(p.astype(BF16), v_ref[0:nk, :]) / l)
        o_ref[qi * tq:(qi + 1) * tq, :] = jnp.where(lane < MLA_V, outs[0], outs[1])


def _attention(q, kt, v, batch, seq):
    n = q.shape[0]
    pairs = MLA_HEADS // 2
    return pl.pallas_call(
        functools.partial(_attention_kernel, seq=seq),
        grid=(batch, pairs),
        in_specs=[
            pl.BlockSpec((seq, 2 * HEAD_PAD), lambda b, p: (b, p)),
            pl.BlockSpec((1, 2 * HEAD_PAD, seq), lambda b, p: (b, p, 0)),
            pl.BlockSpec((seq, 2 * MLA_V), lambda b, p: (b, p)),
        ],
        out_specs=pl.BlockSpec((seq, 2 * MLA_V), lambda b, p: (b, p)),
        out_shape=jax.ShapeDtypeStruct((n, MLA_HEADS * MLA_V), F32),
        compiler_params=pltpu.CompilerParams(
            dimension_semantics=("arbitrary", "arbitrary"), vmem_limit_bytes=VMEM_LIMIT),
        name="attention",
    )(q, kt, v)


def _outproj_router_kernel(h_ref, ya_ref, yb_ref, yc_ref, bgc_ref, woa_ref, wob_ref, woc_ref,
                           ffng_ref, wrh_ref, wrl_ref, br_ref,
                           h1_ref, m_ref, eidx_ref, wts_ref):
    t = MIX_TILE
    ycn = _rms(yc_ref[...], bgc_ref[...])
    proj = (_dot(ya_ref[...].astype(BF16), woa_ref[...]) + _dot(yb_ref[...].astype(BF16), wob_ref[...])
            + _dot(ycn.astype(BF16), woc_ref[...]))
    h1 = h_ref[...] + proj
    h1_ref[...] = h1
    m = _rms(h1, ffng_ref[...])
    m_ref[...] = m

    mh = m.astype(BF16)
    ml = (m - mh.astype(F32)).astype(BF16)
    wrh = wrh_ref[...]
    logits = _dot(mh, wrh) + _dot(ml, wrh) + _dot(mh, wrl_ref[...]) + br_ref[...]
    lt = logits.T
    rowi = lax.broadcasted_iota(jnp.int32, (EXPERTS_PER_GROUP, t), 0)
    g8 = lt[0:8]
    gmax = jnp.max(g8, axis=0, keepdims=True)
    gsum = jnp.sum(jnp.exp(g8 - gmax), axis=0, keepdims=True)
    gidx = jnp.min(jnp.where(g8 == gmax, rowi, 8), axis=0, keepdims=True)
    g_w = 1.0 / gsum
    esel = jnp.zeros((EXPERTS_PER_GROUP, t), F32)
    for g in range(N_GROUPS):
        esel = jnp.where(gidx == g, lt[8 + g * EXPERTS_PER_GROUP:8 + (g + 1) * EXPERTS_PER_GROUP], esel)
    ep = jnp.exp(esel - jnp.max(esel, axis=0, keepdims=True))
    eprob = ep / jnp.sum(ep, axis=0, keepdims=True)
    v1 = jnp.max(eprob, axis=0, keepdims=True)
    i1 = jnp.min(jnp.where(eprob == v1, rowi, 8), axis=0, keepdims=True)
    rest = jnp.where(rowi == i1, -1.0, eprob)
    v2 = jnp.max(rest, axis=0, keepdims=True)
    i2 = jnp.min(jnp.where(rest == v2, rowi, 8), axis=0, keepdims=True)
    den = v1 + v2
    e1 = gidx * EXPERTS_PER_GROUP + i1
    e2 = gidx * EXPERTS_PER_GROUP + i2
    eidx_ref[...] = jnp.where(rowi == 0, e1, jnp.where(rowi == 1, e2, 0))
    wts_ref[...] = jnp.where(rowi == 0, g_w * (v1 / den), jnp.where(rowi == 1, g_w * (v2 / den), 0.0))


def _outproj_router(h, ya, yb, yc, lw):
    n = h.shape[0]
    t = MIX_TILE
    row = lambda w: pl.BlockSpec((t, w), lambda i: (i, 0))
    weights = [lw[k] for k in ("bg_c", "w_o_a", "w_o_b", "w_o_c", "ffn_g", "wr_hi", "wr_lo", "br")]
    colspec = pl.BlockSpec((8, t), lambda i: (0, i))
    return pl.pallas_call(
        _outproj_router_kernel,
        grid=(n // t,),
        in_specs=[row(D_MODEL), row(SGU_WIDTH), row(CONV_WIDTH), row(MLA_HEADS * MLA_V)]
        + [_full(w.shape) for w in weights],
        out_specs=(row(D_MODEL), row(D_MODEL), colspec, colspec),
        out_shape=(
            jax.ShapeDtypeStruct((n, D_MODEL), F32),
            jax.ShapeDtypeStruct((n, D_MODEL), F32),
            jax.ShapeDtypeStruct((8, n), jnp.int32),
            jax.ShapeDtypeStruct((8, n), F32),
        ),
        compiler_params=pltpu.CompilerParams(dimension_semantics=("arbitrary",), vmem_limit_bytes=VMEM_LIMIT),
        name="outproj_router",
    )(h, ya, yb, yc, *weights)


def _moe_rank_kernel(e_ref, upper_ref, ones_ref, ltri_ref, dest_ref, cnt_ref, run_ref, base_ref):
    phase = pl.program_id(0)
    step = pl.program_id(1)
    c = RANK_CHUNK
    expert = lax.broadcasted_iota(jnp.int32, (N_EXPERTS, c), 0)
    onehot = jnp.where(expert == e_ref[...], 1.0, 0.0)
    oh16 = onehot.astype(BF16)

    @pl.when(jnp.logical_and(phase == 0, step == 0))
    def _():
        run_ref[...] = jnp.zeros_like(run_ref)
        base_ref[...] = jnp.zeros_like(base_ref)

    @pl.when(jnp.logical_and(phase == 1, step == 0))
    def _():
        blocks = jnp.floor((run_ref[...] + (EXPERT_ROWS - 1)) * (1.0 / EXPERT_ROWS))
        base_ref[...] = _dot(ltri_ref[...], blocks.astype(BF16)) * EXPERT_ROWS
        run_ref[...] = jnp.zeros_like(run_ref)

    before = _dot(oh16, upper_ref[...])
    pos = before + run_ref[:, 0:1] + base_ref[:, 0:1]
    dest_ref[...] = jnp.sum(onehot * pos, axis=0, keepdims=True).astype(jnp.int32)
    run_ref[...] = run_ref[...] + _dot(oh16, ones_ref[...])
    cnt_ref[...] = run_ref[...]


def _moe_rank(e_flat, consts):
    total = e_flat.shape[1]
    c = RANK_CHUNK
    return pl.pallas_call(
        _moe_rank_kernel,
        grid=(2, total // c),
        in_specs=[pl.BlockSpec((1, c), lambda p, s: (0, s)),
                  _full((c, c)), _full((c, LANES)), _full((N_EXPERTS, N_EXPERTS))],
        out_specs=(pl.BlockSpec((1, c), lambda p, s: (0, s * p)),
                   pl.BlockSpec((N_EXPERTS, LANES), lambda p, s: (0, 0))),
        out_shape=(jax.ShapeDtypeStruct((1, total), jnp.int32),
                   jax.ShapeDtypeStruct((N_EXPERTS, LANES), F32)),
        scratch_shapes=[pltpu.VMEM((N_EXPERTS, LANES), F32), pltpu.VMEM((N_EXPERTS, LANES), F32)],
        compiler_params=pltpu.CompilerParams(dimension_semantics=("arbitrary", "arbitrary")),
        name="moe_rank",
    )(e_flat, consts["upper"], consts["ones"], consts["ltri"])


def _row_copy(src, src_row, dst, dst_row, sem):
    return pltpu.make_async_copy(src.at[pl.ds(src_row, 1)], dst.at[pl.ds(dst_row, 1)], sem)


def _moe_dispatch_kernel(dest_ref, m_ref, xs_in_ref, xs_ref, sem, *, n_tok):
    del xs_in_ref
    t = DISPATCH_TILE
    base = pl.program_id(0) * t

    def issue(r, carry):
        for k in range(TOP_K):
            _row_copy(m_ref, r, xs_ref, dest_ref[k * n_tok + base + r], sem).start()
        return carry

    def drain(r, carry):
        for k in range(TOP_K):
            _row_copy(m_ref, 0, xs_ref, 0, sem).wait()
        return carry

    lax.fori_loop(0, t, issue, 0)
    lax.fori_loop(0, t, drain, 0)


def _moe_dispatch(dest, m, rows):
    n = m.shape[0]
    t = DISPATCH_TILE
    zeros = jnp.zeros((rows, D_MODEL), F32)
    return pl.pallas_call(
        functools.partial(_moe_dispatch_kernel, n_tok=n),
        grid_spec=pltpu.PrefetchScalarGridSpec(
            num_scalar_prefetch=1,
            grid=(n // t,),
            in_specs=[pl.BlockSpec((t, D_MODEL), lambda i, d: (i, 0)),
                      pl.BlockSpec(memory_space=pl.ANY)],
            out_specs=pl.BlockSpec(memory_space=pl.ANY),
            scratch_shapes=[pltpu.SemaphoreType.DMA(())],
        ),
        out_shape=jax.ShapeDtypeStruct((rows, D_MODEL), F32),
        input_output_aliases={2: 0},
        compiler_params=pltpu.CompilerParams(dimension_semantics=("arbitrary",)),
        name="moe_dispatch",
    )(dest, m, zeros)


def _moe_experts_kernel(blk_e_ref, nact_ref, x_ref, wgu_ref, wdn_ref, y_ref):
    del blk_e_ref
    live = pl.program_id(0) < nact_ref[0]

    @pl.when(live)
    def _():
        gu = _dot(x_ref[...].astype(BF16), wgu_ref[0])
        hb = jax.nn.silu(gu[:, :D_EXPERT]) * gu[:, D_EXPERT:]
        y_ref[...] = _dot(hb.astype(BF16), wdn_ref[0])

    @pl.when(jnp.logical_not(live))
    def _():
        y_ref[...] = jnp.zeros_like(y_ref)


def _moe_experts(blk_e, nact, xs, w_gu, w_dn):
    rows = xs.shape[0]
    r = EXPERT_ROWS
    live = lambda b, be, na: (jnp.minimum(b, na[0] - 1), 0)
    return pl.pallas_call(
        _moe_experts_kernel,
        grid_spec=pltpu.PrefetchScalarGridSpec(
            num_scalar_prefetch=2,
            grid=(rows // r,),
            in_specs=[pl.BlockSpec((r, D_MODEL), live),
                      pl.BlockSpec((1, D_MODEL, 2 * D_EXPERT), lambda b, be, na: (be[b], 0, 0)),
                      pl.BlockSpec((1, D_EXPERT, D_MODEL), lambda b, be, na: (be[b], 0, 0))],
            out_specs=pl.BlockSpec((r, D_MODEL), lambda b, be, na: (b, 0)),
        ),
        out_shape=jax.ShapeDtypeStruct((rows, D_MODEL), F32),
        compiler_params=pltpu.CompilerParams(dimension_semantics=("arbitrary",)),
        name="moe_experts",
    )(blk_e, nact, xs, w_gu, w_dn)


def _combine_ple_kernel(dest_ref, h1_ref, wt_ref, p_ref, y_ref, pleg_ref, gatew_ref, projw_ref, postg_ref,
                        out_ref, ybuf_ref, sem, *, n_tok):
    t = COMBINE_TILE
    base = pl.program_id(0) * t

    def issue(r, carry):
        for k in range(TOP_K):
            _row_copy(y_ref, dest_ref[k * n_tok + base + r], ybuf_ref.at[k], r, sem).start()
        return carry

    def drain(r, carry):
        for k in range(TOP_K):
            _row_copy(y_ref, 0, ybuf_ref.at[k], 0, sem).wait()
        return carry

    lax.fori_loop(0, t, issue, 0)
    e = _rms(_dot(p_ref[...].astype(BF16), projw_ref[...]), postg_ref[...])
    lax.fori_loop(0, t, drain, 0)
    wt = wt_ref[...]
    h2 = h1_ref[...] + (wt[:, 0:1] * ybuf_ref[0] + wt[:, 1:2] * ybuf_ref[1])
    gate = jax.nn.sigmoid(_dot(_rms(h2, pleg_ref[...]).astype(BF16), gatew_ref[...]))
    out_ref[...] = h2 + gate * e


def _combine_ple(dest, h1, wt, p, y, lw):
    n = h1.shape[0]
    t = COMBINE_TILE
    row = lambda w: pl.BlockSpec((t, w), lambda i, d: (i, 0))
    weights = [lw[k] for k in ("ple_g", "gate_w", "proj_w", "post_g")]
    return pl.pallas_call(
        functools.partial(_combine_ple_kernel, n_tok=n),
        grid_spec=pltpu.PrefetchScalarGridSpec(
            num_scalar_prefetch=1,
            grid=(n // t,),
            in_specs=[row(D_MODEL), row(TOP_K), row(p.shape[1]), pl.BlockSpec(memory_space=pl.ANY)]
            + [pl.BlockSpec(w.shape, lambda i, d, nd=w.ndim: (0,) * nd) for w in weights],
            out_specs=row(D_MODEL),
            scratch_shapes=[pltpu.VMEM((TOP_K, t, D_MODEL), F32), pltpu.SemaphoreType.DMA(())],
        ),
        out_shape=jax.ShapeDtypeStruct((n, D_MODEL), F32),
        compiler_params=pltpu.CompilerParams(dimension_semantics=("arbitrary",), vmem_limit_bytes=VMEM_LIMIT),
        name="combine_ple",
    )(dest, h1, wt, p, y, *weights)


def _segment_matrix(seg_ids):
    seg_ids = jnp.asarray(seg_ids)
    return (seg_ids[:, None] == seg_ids[None, :]).astype(BF16)


def _constants():
    lane = jnp.arange(256)
    qk_seg = (lane // HEAD_PAD) * 3 + jnp.where(lane % HEAD_PAD < MLA_NOPE, 0, jnp.where(lane % HEAD_PAD < MLA_QK, 1, 2))
    i = jnp.arange(RANK_CHUNK)
    e = jnp.arange(N_EXPERTS)
    return {
        "g64": _segment_matrix(lane // HEAD_DIM),
        "gqk": _segment_matrix(qk_seg),
        "upper": (i[:, None] < i[None, :]).astype(BF16),
        "ones": jnp.ones((RANK_CHUNK, LANES), BF16),
        "ltri": (e[None, :] < e[:, None]).astype(BF16),
    }


def _rope_layouts(cos, sin, batch, seq):
    n, half = cos.shape
    ones = jnp.ones((n, MLA_NOPE), F32)
    zeros_n = jnp.zeros((n, MLA_NOPE), F32)
    zpad = jnp.zeros((n, HEAD_PAD - MLA_QK), F32)
    zhalf = jnp.zeros((n, half), F32)
    return {
        "rc": jnp.concatenate([ones, cos, cos, zpad], axis=1),
        "rs1": jnp.concatenate([zeros_n, -sin, zhalf, zpad], axis=1),
        "rs2": jnp.concatenate([zeros_n, zhalf, sin, zpad], axis=1),
        "cos_t": jnp.transpose(cos.reshape(batch, seq, half), (0, 2, 1)),
        "sin_t": jnp.transpose(sin.reshape(batch, seq, half), (0, 2, 1)),
    }


def _layer_weights(i, consts, p):
    row = lambda v: v.reshape(1, -1).astype(F32)
    w_in = p["w_in"][i]
    o_sgu, o_conv, o_q, o_kv = 2 * SGU_WIDTH, 2 * SGU_WIDTH + 2 * CONV_WIDTH, 0, 0
    o_q = o_conv + Q_RANK
    o_kv = o_q + KV_RANK
    w_ckv = jnp.concatenate([w_in[:, o_q:], jnp.zeros((D_MODEL, 256 - KV_RANK - MLA_ROPE), F32)], axis=1)

    causal = jnp.tril(jnp.ones((CHUNK, CHUNK), F32))
    wcat = jnp.transpose(p["sgu_w"][i] * causal, (1, 0, 2)).reshape(CHUNK, SGU_HEADS * CHUNK)
    sbias = jnp.repeat(jnp.transpose(p["sgu_b"][i]), HEAD_DIM, axis=1)

    w_uq = p["w_uq"][i].reshape(Q_RANK, MLA_HEADS, MLA_QK)
    w_uq = jnp.pad(w_uq, ((0, 0), (0, 0), (0, HEAD_PAD - MLA_QK))).reshape(Q_RANK, MLA_HEADS * HEAD_PAD)
    qn_g = p["q_norm_g"][i]
    q_gain = jnp.tile(jnp.concatenate([qn_g, jnp.zeros((HEAD_PAD - MLA_QK,), F32)]), MLA_HEADS)
    head_invc = jnp.concatenate([jnp.full((MLA_NOPE,), 1.0 / MLA_NOPE, F32),
                                 jnp.full((HEAD_PAD - MLA_NOPE,), 1.0 / MLA_ROPE, F32)])
    w_ukv = p["w_ukv"][i].reshape(KV_RANK, MLA_HEADS, MLA_NOPE + MLA_V)
    w_ukt = jnp.transpose(w_ukv[:, :, :MLA_NOPE], (1, 2, 0)).reshape(MLA_HEADS * MLA_NOPE, KV_RANK)
    w_uv = w_ukv[:, :, MLA_NOPE:].reshape(KV_RANK, MLA_HEADS * MLA_V)
    kn_g = p["k_norm_g"][i]

    bg = p["branch_norm_g"][i]
    w_o = p["w_o"][i]
    wr = jnp.concatenate([p["router_group_w"][i], jnp.zeros((D_MODEL, 8 - N_GROUPS), F32),
                          p["router_expert_w"][i], jnp.zeros((D_MODEL, LANES - 8 - N_EXPERTS), F32)], axis=1)
    wr_hi = wr.astype(BF16)
    br = jnp.concatenate([p["router_group_b"][i], jnp.full((8 - N_GROUPS,), NEG_BIG, F32),
                          p["router_expert_b"][i], jnp.zeros((LANES - 8 - N_EXPERTS,), F32)])
    return {
        "mix_g": row(p["mix_norm_g"][i]),
        "w_sgu": w_in[:, :o_sgu].astype(BF16),
        "w_conv": w_in[:, o_sgu:o_conv].astype(BF16),
        "w_cq": w_in[:, o_conv:o_q].astype(BF16),
        "w_ckv": w_ckv.astype(BF16),
        "sgu_ln_g": row(p["sgu_ln_g"][i]), "sgu_ln_b": row(p["sgu_ln_b"][i]),
        "sgu_wcat": wcat.astype(BF16), "sgu_bias": sbias, "g64": consts["g64"], "bg_a": row(bg[:SGU_WIDTH]),
        "conv_w": jnp.pad(p["conv_w"][i], ((0, 1), (0, 0))), "conv_b": row(p["conv_b"][i]),
        "conv_ln_g": row(p["conv_ln_g"][i]), "conv_ln_b": row(p["conv_ln_b"][i]),
        "pw_w": p["conv_pw_w"][i].astype(BF16), "pw_b": row(p["conv_pw_b"][i]),
        "bg_b": row(bg[SGU_WIDTH:SGU_WIDTH + CONV_WIDTH]),
        "qa_g": row(p["q_a_norm_g"][i]), "w_uq": w_uq.astype(BF16), "gqk": consts["gqk"],
        "q_invc": row(jnp.tile(head_invc, MLA_HEADS)), "q_gain": row(q_gain),
        "kva_g": row(p["kv_a_norm_g"][i]), "w_ukt": w_ukt.astype(BF16), "w_uv": w_uv.astype(BF16),
        "kn_g": jnp.broadcast_to(kn_g[:MLA_NOPE, None], (MLA_NOPE, MIX_TILE)),
        "kpe_g": jnp.broadcast_to(kn_g[MLA_NOPE:, None], (MLA_ROPE, MIX_TILE)),
        "bg_c": row(bg[SGU_WIDTH + CONV_WIDTH:]),
        "w_o_a": w_o[:SGU_WIDTH].astype(BF16),
        "w_o_b": w_o[SGU_WIDTH:SGU_WIDTH + CONV_WIDTH].astype(BF16),
        "w_o_c": w_o[SGU_WIDTH + CONV_WIDTH:].astype(BF16),
        "ffn_g": row(p["ffn_norm_g"][i]),
        "wr_hi": wr_hi, "wr_lo": (wr - wr_hi.astype(F32)).astype(BF16), "br": row(br),
        "w_gu": jnp.concatenate([p["moe_w_gate"][i], p["moe_w_up"][i]], axis=-1).astype(BF16),
        "w_dn": p["moe_w_down"][i].astype(BF16),
        "ple_g": row(p["ple_norm_g"][i]), "gate_w": p["ple_gate_w"][i].astype(BF16),
        "proj_w": p["ple_proj_w"][i].astype(BF16), "post_g": row(p["ple_post_norm_g"][i]),
    }


def _moe(h1, m, eidx, wts, pl_i, lw, consts):
    n = h1.shape[0]
    total = TOP_K * n
    rows = total + N_EXPERTS * EXPERT_ROWS
    dest2d, counts = _moe_rank(eidx[:TOP_K].reshape(1, total), consts)
    dest = dest2d.reshape(total)
    cnt = counts[:, 0].astype(jnp.int32)
    padded = (cnt + EXPERT_ROWS - 1) // EXPERT_ROWS * EXPERT_ROWS
    pend = jnp.cumsum(padded)
    nblk = rows // EXPERT_ROWS
    blk_e = jnp.minimum(jnp.searchsorted(pend, jnp.arange(nblk, dtype=jnp.int32) * EXPERT_ROWS, side="right"),
                        N_EXPERTS - 1).astype(jnp.int32)
    nact = (pend[-1:] // EXPERT_ROWS).astype(jnp.int32)
    xs = _moe_dispatch(dest, m, rows)
    y = _moe_experts(blk_e, nact, xs, lw["w_gu"], lw["w_dn"])
    return _combine_ple(dest, h1, jnp.transpose(wts[:TOP_K]), pl_i, y, lw)


def kernel(x, p, positions, mix_norm_g, w_in, sgu_ln_g, sgu_ln_b, sgu_w, sgu_b, conv_w, conv_b, conv_ln_g, conv_ln_b, conv_pw_w, conv_pw_b, q_a_norm_g, w_uq, kv_a_norm_g, w_ukv, q_norm_g, k_norm_g, branch_norm_g, w_o, ffn_norm_g, router_group_w, router_group_b, router_expert_w, router_expert_b, moe_w_gate, moe_w_up, moe_w_down, ple_norm_g, ple_gate_w, ple_proj_w, ple_post_norm_g):
    params = dict(
        mix_norm_g=mix_norm_g, w_in=w_in, sgu_ln_g=sgu_ln_g, sgu_ln_b=sgu_ln_b, sgu_w=sgu_w, sgu_b=sgu_b,
        conv_w=conv_w, conv_b=conv_b, conv_ln_g=conv_ln_g, conv_ln_b=conv_ln_b, conv_pw_w=conv_pw_w,
        conv_pw_b=conv_pw_b, q_a_norm_g=q_a_norm_g, w_uq=w_uq, kv_a_norm_g=kv_a_norm_g, w_ukv=w_ukv,
        q_norm_g=q_norm_g, k_norm_g=k_norm_g, branch_norm_g=branch_norm_g, w_o=w_o, ffn_norm_g=ffn_norm_g,
        router_group_w=router_group_w, router_group_b=router_group_b, router_expert_w=router_expert_w,
        router_expert_b=router_expert_b, moe_w_gate=moe_w_gate, moe_w_up=moe_w_up, moe_w_down=moe_w_down,
        ple_norm_g=ple_norm_g, ple_gate_w=ple_gate_w, ple_proj_w=ple_proj_w, ple_post_norm_g=ple_post_norm_g)
    batch, seq, d = x.shape
    n = batch * seq
    depth = w_in.shape[0]
    consts = _constants()
    cos, sin = _rope_tables(positions)
    rope = _rope_layouts(cos, sin, batch, seq)
    h = x.reshape(n, d)
    for i in range(depth):
        lw = _layer_weights(i, consts, params)
        ya, yb, q, kt, v = _mixer_pre(h, rope, lw, batch, seq)
        yc = _attention(q, kt, v, batch, seq)
        h1, m, eidx, wts = _outproj_router(h, ya, yb, yc, lw)
        h = _moe(h1, m, eidx, wts, p[i].reshape(n, -1), lw, consts)
    return h.reshape(batch, seq, d)
```

```python
import functools

import jax
import jax.numpy as jnp
from jax import lax
from jax.experimental import pallas as pl
from jax.experimental.pallas import tpu as pltpu

F32 = jnp.float32
BF16 = jnp.bfloat16

D_MODEL = 1024
HEAD_DIM = 64
SGU_HEADS = 4
SGU_WIDTH = 256
CHUNK = 128
CONV_WIDTH = 256
CONV_KERNEL = 31
MLA_HEADS = 8
MLA_NOPE = 64
MLA_ROPE = 32
MLA_QK = MLA_NOPE + MLA_ROPE
MLA_V = 64
Q_RANK = 256
KV_RANK = 128
ROPE_THETA = 10000.0
N_GROUPS = 4
EXPERTS_PER_GROUP = 8
N_EXPERTS = 32
D_EXPERT = 256
TOP_K = 2
EPS = 1e-6

LANES = 128
HEAD_PAD = 128
MIX_TILE = 512
CONV_HALO = 32
ATT_TQ = 256
RANK_CHUNK = 512
EXPERT_ROWS = 256
COMBINE_TILE = 256
ROW_TILES = D_MODEL // LANES
DMA_UNROLL = 8
VMEM_LIMIT = 48 * 1024 * 1024
NEG_BIG = -1e30


def _dot(a, b):
    return jnp.dot(a, b, preferred_element_type=F32)


def _rms(x, g):
    ms = jnp.mean(x * x, axis=-1, keepdims=True)
    return x * lax.rsqrt(ms + EPS) * g


def _seg_sum(x, g_ref):
    g = g_ref[...]
    outs = []
    for c in range(x.shape[1] // 256):
        xb = x[:, c * 256:(c + 1) * 256]
        hi = xb.astype(BF16)
        lo = (xb - hi.astype(F32)).astype(BF16)
        outs.append(_dot(hi, g) + _dot(lo, g))
    return outs[0] if len(outs) == 1 else jnp.concatenate(outs, axis=-1)


def _group_layernorm(x, g_ref, gain, bias):
    inv = 1.0 / HEAD_DIM
    mean = _seg_sum(x, g_ref) * inv
    d = x - mean
    var = _seg_sum(d * d, g_ref) * inv
    return d * lax.rsqrt(var + EPS) * gain + bias


def _rope_table_kernel(pos_ref, invf_ref, cos_ref, sin_ref):
    ang = pos_ref[...].astype(F32) * invf_ref[...]
    cos_ref[...] = jnp.cos(ang)
    sin_ref[...] = jnp.sin(ang)


def _rope_tables(positions):
    b, s = positions.shape
    n = b * s
    half = MLA_ROPE // 2
    inv_freq = ROPE_THETA ** (-jnp.arange(0, MLA_ROPE, 2, dtype=F32) / MLA_ROPE)
    pos_rep = jnp.repeat(positions.reshape(n), half).reshape(n * half // LANES, LANES)
    invf_rep = jnp.tile(inv_freq, LANES // half).reshape(1, LANES)
    shape = jax.ShapeDtypeStruct(pos_rep.shape, F32)
    cos, sin = pl.pallas_call(_rope_table_kernel, out_shape=(shape, shape), name="rope_tables")(pos_rep, invf_rep)
    return cos.reshape(n, half), sin.reshape(n, half)


def _mixer_pre_kernel(
        h_ref, rc_ref, rs1_ref, rs2_ref, cost_ref, sint_ref,
        mixg_ref, wsgu_ref, wconv_ref, wcq_ref, wckv_ref,
        slng_ref, slnb_ref, wcat_ref, sbias_ref, g64_ref, bga_ref,
        cw_ref, cb_ref, clng_ref, clnb_ref, pww_ref, pwb_ref, bgb_ref,
        qag_ref, wuq_ref, gqk_ref, qinvc_ref, qgain_ref,
        kvag_ref, wukt_ref, wuv_ref, kng_ref, kpeg_ref,
        ya_ref, yb_ref, q_ref, kt_ref, v_ref,
        ybuf_ref, *, tiles_per_seq):
    t = MIX_TILE
    a = _rms(h_ref[...], mixg_ref[...]).astype(BF16)

    zg = jax.nn.gelu(_dot(a, wsgu_ref[...]))
    u = zg[:, :SGU_WIDTH]
    vn = _group_layernorm(zg[:, SGU_WIDTH:], g64_ref, slng_ref[...], slnb_ref[...])
    lane_head = lax.broadcasted_iota(jnp.int32, (CHUNK, SGU_WIDTH), 1) // HEAD_DIM
    wcat = wcat_ref[...]
    sbias = sbias_ref[...]
    parts = []
    for c in range(t // CHUNK):
        vc = vn[c * CHUNK:(c + 1) * CHUNK]
        stacked = jnp.concatenate(
            [jnp.where(lane_head == hh, vc, 0.0) for hh in range(SGU_HEADS)], axis=0).astype(BF16)
        s = _dot(wcat, stacked) + sbias
        parts.append(u[c * CHUNK:(c + 1) * CHUNK] * s)
    ya_ref[...] = _rms(jnp.concatenate(parts, axis=0), bga_ref[...])

    zc = _dot(a, wconv_ref[...])
    yg = zc[:, :CONV_WIDTH] * jax.nn.sigmoid(zc[:, CONV_WIDTH:])
    first = (pl.program_id(0) % tiles_per_seq) == 0

    @pl.when(first)
    def _():
        ybuf_ref[0:CONV_HALO, :] = jnp.zeros((CONV_HALO, CONV_WIDTH), F32)

    @pl.when(jnp.logical_not(first))
    def _():
        ybuf_ref[0:CONV_HALO, :] = ybuf_ref[t:t + CONV_HALO, :]

    ybuf_ref[CONV_HALO:CONV_HALO + t, :] = yg
    rows = 64
    shift = CONV_HALO - (CONV_KERNEL - 1)
    conv_parts = []
    for r in range(t // rows):
        acc = jnp.broadcast_to(cb_ref[...], (rows, CONV_WIDTH))
        for tap in range(CONV_KERNEL):
            acc = acc + cw_ref[tap:tap + 1, :] * ybuf_ref[pl.ds(r * rows + shift + tap, rows), :]
        conv_parts.append(acc)
    cv = jnp.concatenate(conv_parts, axis=0)
    cn = _group_layernorm(cv, g64_ref, clng_ref[...], clnb_ref[...])
    yb = _dot(jax.nn.silu(cn).astype(BF16), pww_ref[...]) + pwb_ref[...]
    yb_ref[...] = _rms(yb, bgb_ref[...])

    cqn = _rms(_dot(a, wcq_ref[...]), qag_ref[...]).astype(BF16)
    qf = _dot(cqn, wuq_ref[...])
    ss = _seg_sum(qf * qf, gqk_ref)
    qn = qf * lax.rsqrt(ss * qinvc_ref[...] + EPS) * qgain_ref[...]
    rc, rs1, rs2 = rc_ref[...], rs1_ref[...], rs2_ref[...]
    for hh in range(MLA_HEADS):
        blk = qn[:, hh * HEAD_PAD:(hh + 1) * HEAD_PAD]
        half = MLA_ROPE // 2
        rot = blk * rc + pltpu.roll(blk, HEAD_PAD - half, 1) * rs1 + pltpu.roll(blk, half, 1) * rs2
        q_ref[:, hh * HEAD_PAD:(hh + 1) * HEAD_PAD] = rot.astype(BF16)

    zkv = _dot(a, wckv_ref[...])
    ckvn = _rms(zkv[:, :KV_RANK], kvag_ref[...]).astype(BF16)
    v_ref[...] = _dot(ckvn, wuv_ref[...]).astype(BF16)
    knt = lax.dot_general(wukt_ref[...], ckvn, (((1,), (1,)), ((), ())), preferred_element_type=F32)
    x = zkv[:, KV_RANK:].T[0:MLA_ROPE]
    xn = x * lax.rsqrt(jnp.mean(x * x, axis=0, keepdims=True) + EPS) * kpeg_ref[...]
    x1, x2 = xn[:MLA_ROPE // 2], xn[MLA_ROPE // 2:]
    cos, sin = cost_ref[0], sint_ref[0]
    kpe = jnp.concatenate([x1 * cos - x2 * sin, x2 * cos + x1 * sin], axis=0)
    pad = jnp.zeros((HEAD_PAD - MLA_QK, t), F32)
    kng = kng_ref[...]
    for hh in range(MLA_HEADS):
        blk = knt[hh * MLA_NOPE:(hh + 1) * MLA_NOPE]
        kn = blk * lax.rsqrt(jnp.mean(blk * blk, axis=0, keepdims=True) + EPS) * kng
        kt_ref[0, hh * HEAD_PAD:(hh + 1) * HEAD_PAD, :] = jnp.concatenate([kn, kpe, pad], axis=0).astype(BF16)


def _full(shape):
    nd = len(shape)
    return pl.BlockSpec(shape, lambda *_: (0,) * nd)


def _mixer_pre(h, rope, lw, batch, seq):
    n = h.shape[0]
    t = MIX_TILE
    tps = seq // t
    row = lambda w: pl.BlockSpec((t, w), lambda i: (i, 0))
    tspec = pl.BlockSpec((1, MLA_ROPE // 2, t), lambda i: (i // tps, 0, i % tps))
    weights = [lw[k] for k in (
        "mix_g", "w_sgu", "w_conv", "w_cq", "w_ckv",
        "sgu_ln_g", "sgu_ln_b", "sgu_wcat", "sgu_bias", "g64", "bg_a",
        "conv_w", "conv_b", "conv_ln_g", "conv_ln_b", "pw_w", "pw_b", "bg_b",
        "qa_g", "w_uq", "gqk", "q_invc", "q_gain",
        "kva_g", "w_ukt", "w_uv", "kn_g", "kpe_g")]
    in_specs = [row(D_MODEL), row(LANES), row(LANES), row(LANES), tspec, tspec] + [_full(w.shape) for w in weights]
    out_shape = (
        jax.ShapeDtypeStruct((n, SGU_WIDTH), F32),
        jax.ShapeDtypeStruct((n, CONV_WIDTH), F32),
        jax.ShapeDtypeStruct((n, MLA_HEADS * HEAD_PAD), BF16),
        jax.ShapeDtypeStruct((batch, MLA_HEADS * HEAD_PAD, seq), BF16),
        jax.ShapeDtypeStruct((n, MLA_HEADS * MLA_V), BF16),
    )
    out_specs = (
        row(SGU_WIDTH), row(CONV_WIDTH), row(MLA_HEADS * HEAD_PAD),
        pl.BlockSpec((1, MLA_HEADS * HEAD_PAD, t), lambda i: (i // tps, 0, i % tps)),
        row(MLA_HEADS * MLA_V),
    )
    return pl.pallas_call(
        functools.partial(_mixer_pre_kernel, tiles_per_seq=tps),
        grid=(n // t,),
        in_specs=in_specs,
        out_specs=out_specs,
        out_shape=out_shape,
        scratch_shapes=[pltpu.VMEM((t + CONV_HALO, CONV_WIDTH), F32)],
        compiler_params=pltpu.CompilerParams(dimension_semantics=("arbitrary",), vmem_limit_bytes=VMEM_LIMIT),
        name="mixer_pre",
    )(h, rope["rc"], rope["rs1"], rope["rs2"], rope["cos_t"], rope["sin_t"], *weights)


def _attention_kernel(q_ref, kt_ref, v_ref, o_ref, *, seq):
    tq = ATT_TQ
    scale = MLA_QK ** -0.5
    row = lax.broadcasted_iota(jnp.int32, (tq, tq), 0)
    col = lax.broadcasted_iota(jnp.int32, (tq, tq), 1)
    lane = lax.broadcasted_iota(jnp.int32, (tq, 2 * MLA_V), 1)
    for qi in range(seq // tq):
        nk = (qi + 1) * tq
        outs = []
        for hh in range(2):
            q = q_ref[qi * tq:(qi + 1) * tq, hh * HEAD_PAD:(hh + 1) * HEAD_PAD]
            s = _dot(q, kt_ref[0, hh * HEAD_PAD:(hh + 1) * HEAD_PAD, 0:nk]) * scale
            diag = jnp.where(col <= row, s[:, nk - tq:], NEG_BIG)
            s = diag if qi == 0 else jnp.concatenate([s[:, :nk - tq], diag], axis=1)
            p = jnp.exp(s - jnp.max(s, axis=-1, keepdims=True))
            l = jnp.sum(p, axis=-1, keepdims=True)
            outs.append(_dot(p.astype(BF16), v_ref[0:nk, :]) / l)
        o_ref[qi * tq:(qi + 1) * tq, :] = jnp.where(lane < MLA_V, outs[0], outs[1])


def _attention(q, kt, v, batch, seq):
    n = q.shape[0]
    pairs = MLA_HEADS // 2
    return pl.pallas_call(
        functools.partial(_attention_kernel, seq=seq),
        grid=(batch, pairs),
        in_specs=[
            pl.BlockSpec((seq, 2 * HEAD_PAD), lambda b, p: (b, p)),
            pl.BlockSpec((1, 2 * HEAD_PAD, seq), lambda b, p: (b, p, 0)),
            pl.BlockSpec((seq, 2 * MLA_V), lambda b, p: (b, p)),
        ],
        out_specs=pl.BlockSpec((seq, 2 * MLA_V), lambda b, p: (b, p)),
        out_shape=jax.ShapeDtypeStruct((n, MLA_HEADS * MLA_V), F32),
        compiler_params=pltpu.CompilerParams(
            dimension_semantics=("arbitrary", "arbitrary"), vmem_limit_bytes=VMEM_LIMIT),
        name="attention",
    )(q, kt, v)


def _outproj_router_kernel(h_ref, ya_ref, yb_ref, yc_ref, bgc_ref, woa_ref, wob_ref, woc_ref,
                           ffng_ref, wrh_ref, wrl_ref, br_ref,
                           h1_ref, m_ref, eidx_ref, wts_ref):
    t = MIX_TILE
    ycn = _rms(yc_ref[...], bgc_ref[...])
    proj = (_dot(ya_ref[...].astype(BF16), woa_ref[...]) + _dot(yb_ref[...].astype(BF16), wob_ref[...])
            + _dot(ycn.astype(BF16), woc_ref[...]))
    h1 = h_ref[...] + proj
    h1_ref[...] = h1
    m = _rms(h1, ffng_ref[...])
    _store_row_tiled(m_ref, m)

    mh = m.astype(BF16)
    ml = (m - mh.astype(F32)).astype(BF16)
    wrh = wrh_ref[...]
    logits = _dot(mh, wrh) + _dot(ml, wrh) + _dot(mh, wrl_ref[...]) + br_ref[...]
    lt = logits.T
    rowi = lax.broadcasted_iota(jnp.int32, (EXPERTS_PER_GROUP, t), 0)
    g8 = lt[0:8]
    gmax = jnp.max(g8, axis=0, keepdims=True)
    gsum = jnp.sum(jnp.exp(g8 - gmax), axis=0, keepdims=True)
    gidx = jnp.min(jnp.where(g8 == gmax, rowi, 8), axis=0, keepdims=True)
    g_w = 1.0 / gsum
    esel = jnp.zeros((EXPERTS_PER_GROUP, t), F32)
    for g in range(N_GROUPS):
        esel = jnp.where(gidx == g, lt[8 + g * EXPERTS_PER_GROUP:8 + (g + 1) * EXPERTS_PER_GROUP], esel)
    ep = jnp.exp(esel - jnp.max(esel, axis=0, keepdims=True))
    eprob = ep / jnp.sum(ep, axis=0, keepdims=True)
    v1 = jnp.max(eprob, axis=0, keepdims=True)
    i1 = jnp.min(jnp.where(eprob == v1, rowi, 8), axis=0, keepdims=True)
    rest = jnp.where(rowi == i1, -1.0, eprob)
    v2 = jnp.max(rest, axis=0, keepdims=True)
    i2 = jnp.min(jnp.where(rest == v2, rowi, 8), axis=0, keepdims=True)
    den = v1 + v2
    e1 = gidx * EXPERTS_PER_GROUP + i1
    e2 = gidx * EXPERTS_PER_GROUP + i2
    eidx_ref[...] = jnp.where(rowi == 0, e1, jnp.where(rowi == 1, e2, 0))
    wts_ref[...] = jnp.where(rowi == 0, g_w * (v1 / den), jnp.where(rowi == 1, g_w * (v2 / den), 0.0))


def _outproj_router(h, ya, yb, yc, lw):
    n = h.shape[0]
    t = MIX_TILE
    row = lambda w: pl.BlockSpec((t, w), lambda i: (i, 0))
    weights = [lw[k] for k in ("bg_c", "w_o_a", "w_o_b", "w_o_c", "ffn_g", "wr_hi", "wr_lo", "br")]
    colspec = pl.BlockSpec((8, t), lambda i: (0, i))
    return pl.pallas_call(
        _outproj_router_kernel,
        grid=(n // t,),
        in_specs=[row(D_MODEL), row(SGU_WIDTH), row(CONV_WIDTH), row(MLA_HEADS * MLA_V)]
        + [_full(w.shape) for w in weights],
        out_specs=(row(D_MODEL), pl.BlockSpec((t * ROW_TILES, LANES), lambda i: (i, 0)), colspec, colspec),
        out_shape=(
            jax.ShapeDtypeStruct((n, D_MODEL), F32),
            jax.ShapeDtypeStruct((n * ROW_TILES, LANES), F32),
            jax.ShapeDtypeStruct((8, n), jnp.int32),
            jax.ShapeDtypeStruct((8, n), F32),
        ),
        compiler_params=pltpu.CompilerParams(dimension_semantics=("arbitrary",), vmem_limit_bytes=VMEM_LIMIT),
        name="outproj_router",
    )(h, ya, yb, yc, *weights)


def _moe_rank_kernel(e_ref, upper_ref, ones_ref, ltri_ref, dest_ref, cnt_ref, run_ref, base_ref):
    phase = pl.program_id(0)
    step = pl.program_id(1)
    c = RANK_CHUNK
    expert = lax.broadcasted_iota(jnp.int32, (N_EXPERTS, c), 0)
    onehot = jnp.where(expert == e_ref[...], 1.0, 0.0)
    oh16 = onehot.astype(BF16)

    @pl.when(jnp.logical_and(phase == 0, step == 0))
    def _():
        run_ref[...] = jnp.zeros_like(run_ref)
        base_ref[...] = jnp.zeros_like(base_ref)

    @pl.when(jnp.logical_and(phase == 1, step == 0))
    def _():
        blocks = jnp.floor((run_ref[...] + (EXPERT_ROWS - 1)) * (1.0 / EXPERT_ROWS))
        base_ref[...] = _dot(ltri_ref[...], blocks.astype(BF16)) * EXPERT_ROWS
        run_ref[...] = jnp.zeros_like(run_ref)

    @pl.when(phase == 0)
    def _():
        dest_ref[...] = jnp.zeros_like(dest_ref)

    @pl.when(phase == 1)
    def _():
        before = _dot(oh16, upper_ref[...])
        pos = before + run_ref[:, 0:1] + base_ref[:, 0:1]
        dest_ref[...] = jnp.sum(onehot * pos, axis=0, keepdims=True).astype(jnp.int32)

    run_ref[...] = run_ref[...] + _dot(oh16, ones_ref[...])
    cnt_ref[...] = run_ref[...]


def _moe_rank(e_flat, consts):
    total = e_flat.shape[1]
    c = RANK_CHUNK
    return pl.pallas_call(
        _moe_rank_kernel,
        grid=(2, total // c),
        in_specs=[pl.BlockSpec((1, c), lambda p, s: (0, s)),
                  _full((c, c)), _full((c, LANES)), _full((N_EXPERTS, N_EXPERTS))],
        out_specs=(pl.BlockSpec((1, c), lambda p, s: (0, s * p)),
                   pl.BlockSpec((N_EXPERTS, LANES), lambda p, s: (0, 0))),
        out_shape=(jax.ShapeDtypeStruct((1, total), jnp.int32),
                   jax.ShapeDtypeStruct((N_EXPERTS, LANES), F32)),
        scratch_shapes=[pltpu.VMEM((N_EXPERTS, LANES), F32), pltpu.VMEM((N_EXPERTS, LANES), F32)],
        compiler_params=pltpu.CompilerParams(dimension_semantics=("arbitrary", "arbitrary")),
        name="moe_rank",
    )(e_flat, consts["upper"], consts["ones"], consts["ltri"])


def _store_row_tiled(ref, x, offset=0):
    rows = x.shape[0]
    for s in range(ROW_TILES):
        ref[pl.ds(offset + s, rows, stride=ROW_TILES), :] = x[:, s * LANES:(s + 1) * LANES]


def _load_row_tiled(ref, rows, offset=0):
    return jnp.concatenate(
        [ref[pl.ds(offset + s, rows, stride=ROW_TILES), :] for s in range(ROW_TILES)], axis=1)


def _row_copy(src, src_row, dst, dst_row, sem):
    return pltpu.make_async_copy(
        src.at[pl.ds(pl.multiple_of(src_row * ROW_TILES, ROW_TILES), ROW_TILES)],
        dst.at[pl.ds(pl.multiple_of(dst_row * ROW_TILES, ROW_TILES), ROW_TILES)], sem)


def _rows_wait(src, dst, dst_row, rows, sem):
    pltpu.make_async_copy(
        src.at[pl.ds(0, rows * ROW_TILES)],
        dst.at[pl.ds(pl.multiple_of(dst_row * ROW_TILES, ROW_TILES), rows * ROW_TILES)], sem).wait()


def _moe_invert_kernel(dest_ref, inv_ref, *, n_tok):
    def clear(s, carry):
        inv_ref[s] = 0
        return carry

    def place(j, carry):
        for k in range(TOP_K):
            inv_ref[dest_ref[k * n_tok + j]] = j
        return carry

    lax.fori_loop(0, inv_ref.shape[0], clear, 0, unroll=DMA_UNROLL)
    lax.fori_loop(0, n_tok, place, 0, unroll=DMA_UNROLL)


def _moe_invert(dest, rows, n_tok):
    return pl.pallas_call(
        functools.partial(_moe_invert_kernel, n_tok=n_tok),
        in_specs=[pl.BlockSpec(memory_space=pltpu.SMEM)],
        out_specs=pl.BlockSpec(memory_space=pltpu.SMEM),
        out_shape=jax.ShapeDtypeStruct((rows,), jnp.int32),
        name="moe_invert",
    )(dest)


def _moe_experts_kernel(inv_ref, blk_e_ref, nact_ref, m_ref, wg_ref, wu_ref, wd_ref, y_ref,
                        xbuf_ref, wgu_ref, wdn_ref, sem):
    r = EXPERT_ROWS
    b = pl.program_id(0)
    nact = nact_ref[0]

    def gather(blk, slot):
        def issue(i, carry):
            _row_copy(m_ref, inv_ref[blk * r + i], xbuf_ref, slot * r + i, sem.at[slot]).start()
            return carry
        lax.fori_loop(0, r, issue, 0, unroll=DMA_UNROLL)

    @pl.when(b == 0)
    def _():
        gather(0, 0)

    @pl.when(b + 1 < nact)
    def _():
        gather(b + 1, (b + 1) % 2)

    @pl.when(b < nact)
    def _():
        @pl.when(jnp.logical_or(b == 0, blk_e_ref[b] != blk_e_ref[jnp.maximum(b - 1, 0)]))
        def _():
            wgu_ref[:, :D_EXPERT] = wg_ref[0, 0].astype(BF16)
            wgu_ref[:, D_EXPERT:] = wu_ref[0, 0].astype(BF16)
            wdn_ref[...] = wd_ref[0, 0].astype(BF16)

        slot = b % 2
        _rows_wait(m_ref, xbuf_ref, slot * r, r, sem.at[slot])
        x = _load_row_tiled(xbuf_ref, r, slot * (r * ROW_TILES)).astype(BF16)
        gu = _dot(x, wgu_ref[...])
        hb = jax.nn.silu(gu[:, :D_EXPERT]) * gu[:, D_EXPERT:]
        _store_row_tiled(y_ref, _dot(hb.astype(BF16), wdn_ref[...]))

    @pl.when(b >= nact)
    def _():
        y_ref[...] = jnp.zeros_like(y_ref)


def _moe_experts(inv, blk_e, nact, m_rt, layer, w_gate, w_up, w_down):
    rows = inv.shape[0]
    r = EXPERT_ROWS
    wspec = lambda k, n: pl.BlockSpec((1, 1, k, n), lambda b, iv, be, na: (layer, be[b], 0, 0))
    return pl.pallas_call(
        _moe_experts_kernel,
        grid_spec=pltpu.PrefetchScalarGridSpec(
            num_scalar_prefetch=3,
            grid=(rows // r,),
            in_specs=[pl.BlockSpec(memory_space=pl.ANY),
                      wspec(D_MODEL, D_EXPERT), wspec(D_MODEL, D_EXPERT), wspec(D_EXPERT, D_MODEL)],
            out_specs=pl.BlockSpec((r * ROW_TILES, LANES), lambda b, iv, be, na: (b, 0)),
            scratch_shapes=[pltpu.VMEM((2 * r * ROW_TILES, LANES), F32),
                            pltpu.VMEM((D_MODEL, 2 * D_EXPERT), BF16),
                            pltpu.VMEM((D_EXPERT, D_MODEL), BF16),
                            pltpu.SemaphoreType.DMA((2,))],
        ),
        out_shape=jax.ShapeDtypeStruct((rows * ROW_TILES, LANES), F32),
        compiler_params=pltpu.CompilerParams(dimension_semantics=("arbitrary",), vmem_limit_bytes=VMEM_LIMIT),
        name="moe_experts",
    )(inv, blk_e, nact, m_rt, w_gate, w_up, w_down)


def _combine_ple_kernel(dest_ref, h1_ref, wt_ref, p_ref, y_ref, pleg_ref, gatew_ref, projw_ref, postg_ref,
                        out_ref, ybuf_ref, sem, *, n_tok):
    t = COMBINE_TILE
    i = pl.program_id(0)

    def gather(step, slot):
        def issue(r, carry):
            for k in range(TOP_K):
                _row_copy(y_ref, dest_ref[k * n_tok + step * t + r], ybuf_ref, (slot * TOP_K + k) * t + r,
                          sem.at[slot]).start()
            return carry
        lax.fori_loop(0, t, issue, 0, unroll=DMA_UNROLL // TOP_K)

    @pl.when(i == 0)
    def _():
        gather(0, 0)

    @pl.when(i + 1 < pl.num_programs(0))
    def _():
        gather(i + 1, (i + 1) % 2)

    e = _rms(_dot(p_ref[...].astype(BF16), projw_ref[...]), postg_ref[...])
    slot = i % 2
    _rows_wait(y_ref, ybuf_ref, slot * TOP_K * t, TOP_K * t, sem.at[slot])
    wt = wt_ref[...]
    y0 = _load_row_tiled(ybuf_ref, t, (slot * TOP_K) * (t * ROW_TILES))
    y1 = _load_row_tiled(ybuf_ref, t, (slot * TOP_K + 1) * (t * ROW_TILES))
    h2 = h1_ref[...] + (wt[:, 0:1] * y0 + wt[:, 1:2] * y1)
    gate = jax.nn.sigmoid(_dot(_rms(h2, pleg_ref[...]).astype(BF16), gatew_ref[...]))
    out_ref[...] = h2 + gate * e


def _combine_ple(dest, h1, wt, p, y_rt, lw):
    n = h1.shape[0]
    t = COMBINE_TILE
    row = lambda w: pl.BlockSpec((t, w), lambda i, d: (i, 0))
    weights = [lw[k] for k in ("ple_g", "gate_w", "proj_w", "post_g")]
    return pl.pallas_call(
        functools.partial(_combine_ple_kernel, n_tok=n),
        grid_spec=pltpu.PrefetchScalarGridSpec(
            num_scalar_prefetch=1,
            grid=(n // t,),
            in_specs=[row(D_MODEL), row(TOP_K), row(p.shape[1]), pl.BlockSpec(memory_space=pl.ANY)]
            + [pl.BlockSpec(w.shape, lambda i, d, nd=w.ndim: (0,) * nd) for w in weights],
            out_specs=row(D_MODEL),
            scratch_shapes=[pltpu.VMEM((2 * TOP_K * t * ROW_TILES, LANES), F32), pltpu.SemaphoreType.DMA((2,))],
        ),
        out_shape=jax.ShapeDtypeStruct((n, D_MODEL), F32),
        compiler_params=pltpu.CompilerParams(dimension_semantics=("arbitrary",), vmem_limit_bytes=VMEM_LIMIT),
        name="combine_ple",
    )(dest, h1, wt, p, y_rt, *weights)


def _segment_matrix(seg_ids):
    seg_ids = jnp.asarray(seg_ids)
    return (seg_ids[:, None] == seg_ids[None, :]).astype(BF16)


def _constants():
    lane = jnp.arange(256)
    qk_seg = (lane // HEAD_PAD) * 3 + jnp.where(lane % HEAD_PAD < MLA_NOPE, 0, jnp.where(lane % HEAD_PAD < MLA_QK, 1, 2))
    i = jnp.arange(RANK_CHUNK)
    e = jnp.arange(N_EXPERTS)
    return {
        "g64": _segment_matrix(lane // HEAD_DIM),
        "gqk": _segment_matrix(qk_seg),
        "upper": (i[:, None] < i[None, :]).astype(BF16),
        "ones": jnp.ones((RANK_CHUNK, LANES), BF16),
        "ltri": (e[None, :] < e[:, None]).astype(BF16),
    }


def _rope_layouts(cos, sin, batch, seq):
    n, half = cos.shape
    ones = jnp.ones((n, MLA_NOPE), F32)
    zeros_n = jnp.zeros((n, MLA_NOPE), F32)
    zpad = jnp.zeros((n, HEAD_PAD - MLA_QK), F32)
    zhalf = jnp.zeros((n, half), F32)
    return {
        "rc": jnp.concatenate([ones, cos, cos, zpad], axis=1),
        "rs1": jnp.concatenate([zeros_n, -sin, zhalf, zpad], axis=1),
        "rs2": jnp.concatenate([zeros_n, zhalf, sin, zpad], axis=1),
        "cos_t": jnp.transpose(cos.reshape(batch, seq, half), (0, 2, 1)),
        "sin_t": jnp.transpose(sin.reshape(batch, seq, half), (0, 2, 1)),
    }


def _layer_weights(i, consts, p):
    row = lambda v: v.reshape(1, -1).astype(F32)
    w_in = p["w_in"][i]
    o_sgu, o_conv, o_q, o_kv = 2 * SGU_WIDTH, 2 * SGU_WIDTH + 2 * CONV_WIDTH, 0, 0
    o_q = o_conv + Q_RANK
    o_kv = o_q + KV_RANK
    w_ckv = jnp.concatenate([w_in[:, o_q:], jnp.zeros((D_MODEL, 256 - KV_RANK - MLA_ROPE), F32)], axis=1)

    causal = jnp.tril(jnp.ones((CHUNK, CHUNK), F32))
    wcat = jnp.transpose(p["sgu_w"][i] * causal, (1, 0, 2)).reshape(CHUNK, SGU_HEADS * CHUNK)
    sbias = jnp.repeat(jnp.transpose(p["sgu_b"][i]), HEAD_DIM, axis=1)

    w_uq = p["w_uq"][i].reshape(Q_RANK, MLA_HEADS, MLA_QK)
    w_uq = jnp.pad(w_uq, ((0, 0), (0, 0), (0, HEAD_PAD - MLA_QK))).reshape(Q_RANK, MLA_HEADS * HEAD_PAD)
    qn_g = p["q_norm_g"][i]
    q_gain = jnp.tile(jnp.concatenate([qn_g, jnp.zeros((HEAD_PAD - MLA_QK,), F32)]), MLA_HEADS)
    head_invc = jnp.concatenate([jnp.full((MLA_NOPE,), 1.0 / MLA_NOPE, F32),
                                 jnp.full((HEAD_PAD - MLA_NOPE,), 1.0 / MLA_ROPE, F32)])
    w_ukv = p["w_ukv"][i].reshape(KV_RANK, MLA_HEADS, MLA_NOPE + MLA_V)
    w_ukt = jnp.transpose(w_ukv[:, :, :MLA_NOPE], (1, 2, 0)).reshape(MLA_HEADS * MLA_NOPE, KV_RANK)
    w_uv = w_ukv[:, :, MLA_NOPE:].reshape(KV_RANK, MLA_HEADS * MLA_V)
    kn_g = p["k_norm_g"][i]

    bg = p["branch_norm_g"][i]
    w_o = p["w_o"][i]
    wr = jnp.concatenate([p["router_group_w"][i], jnp.zeros((D_MODEL, 8 - N_GROUPS), F32),
                          p["router_expert_w"][i], jnp.zeros((D_MODEL, LANES - 8 - N_EXPERTS), F32)], axis=1)
    wr_hi = wr.astype(BF16)
    br = jnp.concatenate([p["router_group_b"][i], jnp.full((8 - N_GROUPS,), NEG_BIG, F32),
                          p["router_expert_b"][i], jnp.zeros((LANES - 8 - N_EXPERTS,), F32)])
    return {
        "mix_g": row(p["mix_norm_g"][i]),
        "w_sgu": w_in[:, :o_sgu].astype(BF16),
        "w_conv": w_in[:, o_sgu:o_conv].astype(BF16),
        "w_cq": w_in[:, o_conv:o_q].astype(BF16),
        "w_ckv": w_ckv.astype(BF16),
        "sgu_ln_g": row(p["sgu_ln_g"][i]), "sgu_ln_b": row(p["sgu_ln_b"][i]),
        "sgu_wcat": wcat.astype(BF16), "sgu_bias": sbias, "g64": consts["g64"], "bg_a": row(bg[:SGU_WIDTH]),
        "conv_w": jnp.pad(p["conv_w"][i], ((0, 1), (0, 0))), "conv_b": row(p["conv_b"][i]),
        "conv_ln_g": row(p["conv_ln_g"][i]), "conv_ln_b": row(p["conv_ln_b"][i]),
        "pw_w": p["conv_pw_w"][i].astype(BF16), "pw_b": row(p["conv_pw_b"][i]),
        "bg_b": row(bg[SGU_WIDTH:SGU_WIDTH + CONV_WIDTH]),
        "qa_g": row(p["q_a_norm_g"][i]), "w_uq": w_uq.astype(BF16), "gqk": consts["gqk"],
        "q_invc": row(jnp.tile(head_invc, MLA_HEADS)), "q_gain": row(q_gain),
        "kva_g": row(p["kv_a_norm_g"][i]), "w_ukt": w_ukt.astype(BF16), "w_uv": w_uv.astype(BF16),
        "kn_g": jnp.broadcast_to(kn_g[:MLA_NOPE, None], (MLA_NOPE, MIX_TILE)),
        "kpe_g": jnp.broadcast_to(kn_g[MLA_NOPE:, None], (MLA_ROPE, MIX_TILE)),
        "bg_c": row(bg[SGU_WIDTH + CONV_WIDTH:]),
        "w_o_a": w_o[:SGU_WIDTH].astype(BF16),
        "w_o_b": w_o[SGU_WIDTH:SGU_WIDTH + CONV_WIDTH].astype(BF16),
        "w_o_c": w_o[SGU_WIDTH + CONV_WIDTH:].astype(BF16),
        "ffn_g": row(p["ffn_norm_g"][i]),
        "wr_hi": wr_hi, "wr_lo": (wr - wr_hi.astype(F32)).astype(BF16), "br": row(br),
        "layer": i, "w_gate": p["moe_w_gate"], "w_up": p["moe_w_up"], "w_down": p["moe_w_down"],
        "ple_g": row(p["ple_norm_g"][i]), "gate_w": p["ple_gate_w"][i].astype(BF16),
        "proj_w": p["ple_proj_w"][i].astype(BF16), "post_g": row(p["ple_post_norm_g"][i]),
    }


def _moe(h1, m, eidx, wts, pl_i, lw, consts):
    n = h1.shape[0]
    total = TOP_K * n
    rows = total + N_EXPERTS * EXPERT_ROWS
    dest2d, counts = _moe_rank(eidx[:TOP_K].reshape(1, total), consts)
    dest = dest2d.reshape(total)
    cnt = counts[:, 0].astype(jnp.int32)
    padded = (cnt + EXPERT_ROWS - 1) // EXPERT_ROWS * EXPERT_ROWS
    pend = jnp.cumsum(padded)
    nblk = rows // EXPERT_ROWS
    blk_start = jnp.arange(nblk, dtype=jnp.int32) * EXPERT_ROWS
    blk_e = jnp.minimum(jnp.sum((pend[None, :] <= blk_start[:, None]).astype(jnp.int32), axis=1), N_EXPERTS - 1)
    nact = (pend[-1:] // EXPERT_ROWS).astype(jnp.int32)
    inv = _moe_invert(dest, rows, n)
    y = _moe_experts(inv, blk_e, nact, m, lw["layer"], lw["w_gate"], lw["w_up"], lw["w_down"])
    return _combine_ple(dest, h1, jnp.transpose(wts[:TOP_K]), pl_i, y, lw)


def kernel(x, p, positions, mix_norm_g, w_in, sgu_ln_g, sgu_ln_b, sgu_w, sgu_b, conv_w, conv_b, conv_ln_g, conv_ln_b, conv_pw_w, conv_pw_b, q_a_norm_g, w_uq, kv_a_norm_g, w_ukv, q_norm_g, k_norm_g, branch_norm_g, w_o, ffn_norm_g, router_group_w, router_group_b, router_expert_w, router_expert_b, moe_w_gate, moe_w_up, moe_w_down, ple_norm_g, ple_gate_w, ple_proj_w, ple_post_norm_g):
    params = dict(
        mix_norm_g=mix_norm_g, w_in=w_in, sgu_ln_g=sgu_ln_g, sgu_ln_b=sgu_ln_b, sgu_w=sgu_w, sgu_b=sgu_b,
        conv_w=conv_w, conv_b=conv_b, conv_ln_g=conv_ln_g, conv_ln_b=conv_ln_b, conv_pw_w=conv_pw_w,
        conv_pw_b=conv_pw_b, q_a_norm_g=q_a_norm_g, w_uq=w_uq, kv_a_norm_g=kv_a_norm_g, w_ukv=w_ukv,
        q_norm_g=q_norm_g, k_norm_g=k_norm_g, branch_norm_g=branch_norm_g, w_o=w_o, ffn_norm_g=ffn_norm_g,
        router_group_w=router_group_w, router_group_b=router_group_b, router_expert_w=router_expert_w,
        router_expert_b=router_expert_b, moe_w_gate=moe_w_gate, moe_w_up=moe_w_up, moe_w_down=moe_w_down,
        ple_norm_g=ple_norm_g, ple_gate_w=ple_gate_w, ple_proj_w=ple_proj_w, ple_post_norm_g=ple_post_norm_g)
    batch, seq, d = x.shape
    n = batch * seq
    depth = w_in.shape[0]
    consts = _constants()
    cos, sin = _rope_tables(positions)
    rope = _rope_layouts(cos, sin, batch, seq)
    h = x.reshape(n, d)
    for i in range(depth):
        lw = _layer_weights(i, consts, params)
        ya, yb, q, kt, v = _mixer_pre(h, rope, lw, batch, seq)
        yc = _attention(q, kt, v, batch, seq)
        h1, m, eidx, wts = _outproj_router(h, ya, yb, yc, lw)
        h = _moe(h1, m, eidx, wts, p[i].reshape(n, -1), lw, consts)
    return h.reshape(batch, seq, d)
```

```python
import functools
import math

import jax
import jax.numpy as jnp
from jax import lax
from jax.experimental import pallas as pl
from jax.experimental.pallas import tpu as pltpu

F32 = jnp.float32
BF16 = jnp.bfloat16

D_MODEL = 1024
HEAD_DIM = 64
SGU_HEADS = 4
SGU_WIDTH = 256
CHUNK = 128
CONV_WIDTH = 256
CONV_KERNEL = 31
MLA_HEADS = 8
MLA_NOPE = 64
MLA_ROPE = 32
MLA_QK = MLA_NOPE + MLA_ROPE
MLA_V = 64
Q_RANK = 256
KV_RANK = 128
ROPE_THETA = 10000.0
N_GROUPS = 4
EXPERTS_PER_GROUP = 8
N_EXPERTS = 32
D_EXPERT = 256
TOP_K = 2
EPS = 1e-6

LANES = 128
SUBLANES = 8
HEAD_PAD = 128
MIX_TILE = 512
CONV_HALO = 32
CONV_SHIFTED_ROWS = MIX_TILE + CONV_HALO - SUBLANES
ATT_TQ = 256
ATT_LOOKAHEAD = 1
RANK_CHUNK = 512
EXPERT_ROWS = 256
COMBINE_TILE = 256
ROW_TILES = D_MODEL // LANES
DMA_UNROLL = 8
VMEM_LIMIT = 48 * 1024 * 1024
NEG_BIG = -1e30


def _dot(a, b):
    return jnp.dot(a, b, preferred_element_type=F32)


def _rms(x, g):
    ms = jnp.mean(x * x, axis=-1, keepdims=True)
    return x * lax.rsqrt(ms + EPS) * g


def _seg_mean(x, g_ref, split=True):
    g = g_ref[...]
    outs = []
    for c in range(x.shape[1] // 256):
        xb = x[:, c * 256:(c + 1) * 256]
        hi = xb.astype(BF16)
        acc = _dot(hi, g)
        if split:
            acc = acc + _dot((xb - hi.astype(F32)).astype(BF16), g)
        outs.append(acc)
    return outs[0] if len(outs) == 1 else jnp.concatenate(outs, axis=-1)


def _group_layernorm(x, g_ref, gain, bias):
    d = x - _seg_mean(x, g_ref)
    var = _seg_mean(d * d, g_ref)
    return d * lax.rsqrt(var + EPS) * gain + bias


def _rope_table_kernel(pos_ref, invf_ref, cos_ref, sin_ref):
    ang = pos_ref[...].astype(F32) * invf_ref[...]
    cos_ref[...] = jnp.cos(ang)
    sin_ref[...] = jnp.sin(ang)


def _rope_tables(positions):
    b, s = positions.shape
    n = b * s
    half = MLA_ROPE // 2
    inv_freq = ROPE_THETA ** (-jnp.arange(0, MLA_ROPE, 2, dtype=F32) / MLA_ROPE)
    pos_rep = jnp.repeat(positions.reshape(n), half).reshape(n * half // LANES, LANES)
    invf_rep = jnp.tile(inv_freq, LANES // half).reshape(1, LANES)
    shape = jax.ShapeDtypeStruct(pos_rep.shape, F32)
    cos, sin = pl.pallas_call(_rope_table_kernel, out_shape=(shape, shape), name="rope_tables")(pos_rep, invf_rep)
    return cos.reshape(n, half), sin.reshape(n, half)


def _mixer_pre_kernel(
        h_ref, rc_ref, rs1_ref, rs2_ref, cost_ref, sint_ref,
        mixg_ref, wsgu_ref, wconv_ref, wcq_ref, wckv_ref,
        slng_ref, slnb_ref, wcat_ref, sbias_ref, g64_ref, bga_ref,
        cw_ref, cb_ref, clng_ref, clnb_ref, pww_ref, pwb_ref, bgb_ref,
        qag_ref, wuq_ref, gqk_ref, qgain_ref,
        kvag_ref, wukt_ref, wuv_ref, vones_ref, kng_ref, kpeg_ref,
        ya_ref, yb_ref, q_ref, kt_ref, v_ref,
        ybuf_ref, ysh_ref, *, tiles_per_seq):
    t = MIX_TILE
    a = _rms(h_ref[...], mixg_ref[...]).astype(BF16)

    zs = _dot(a, wsgu_ref[...])
    zc = _dot(a, wconv_ref[...])
    zq = _dot(a, wcq_ref[...])
    zkv = _dot(a, wckv_ref[...])
    zg = jax.nn.gelu(zs)
    u = zg[:, :SGU_WIDTH]
    vn = _group_layernorm(zg[:, SGU_WIDTH:], g64_ref, slng_ref[...], slnb_ref[...])
    lane_head = lax.broadcasted_iota(jnp.int32, (CHUNK, SGU_WIDTH), 1) // HEAD_DIM
    wcat = wcat_ref[...]
    sbias = sbias_ref[...]
    parts = []
    for c in range(t // CHUNK):
        vc = vn[c * CHUNK:(c + 1) * CHUNK]
        stacked = jnp.concatenate(
            [jnp.where(lane_head == hh, vc, 0.0) for hh in range(SGU_HEADS)], axis=0).astype(BF16)
        s = _dot(wcat, stacked) + sbias
        parts.append(u[c * CHUNK:(c + 1) * CHUNK] * s)
    ya_ref[...] = _rms(jnp.concatenate(parts, axis=0), bga_ref[...])

    yg =zc[:, :CONV_WIDTH] * jax.nn.sigmoid(zc[:, CONV_WIDTH:])
    first = (pl.program_id(0) % tiles_per_seq) == 0

    @pl.when(first)
    def _():
        ybuf_ref[0:CONV_HALO, :] = jnp.zeros((CONV_HALO, CONV_WIDTH), F32)

    @pl.when(jnp.logical_not(first))
    def _():
        ybuf_ref[0:CONV_HALO, :] = ybuf_ref[t:t + CONV_HALO, :]

    ybuf_ref[CONV_HALO:CONV_HALO + t, :] = yg
    for sh in range(1, SUBLANES):
        ysh_ref[sh - 1] = ybuf_ref[pl.ds(sh, CONV_SHIFTED_ROWS), :]
    rows = 64
    first_tap_row = CONV_HALO - (CONV_KERNEL - 1)
    conv_parts = []
    for r in range(t // rows):
        acc = jnp.broadcast_to(cb_ref[...], (rows, CONV_WIDTH))
        for tap in range(CONV_KERNEL):
            sh = (first_tap_row + tap) % SUBLANES
            start = r * rows + first_tap_row + tap - sh
            src = ybuf_ref[pl.ds(start, rows), :] if sh == 0 else ysh_ref[sh - 1, pl.ds(start, rows), :]
            acc = acc + cw_ref[tap:tap + 1, :] * src
        conv_parts.append(acc)
    cv = jnp.concatenate(conv_parts, axis=0)
    cn = _group_layernorm(cv, g64_ref, clng_ref[...], clnb_ref[...])
    yb = _dot(jax.nn.silu(cn).astype(BF16), pww_ref[...]) + pwb_ref[...]
    yb_ref[...] = _rms(yb, bgb_ref[...])

    cqn = _rms(zq, qag_ref[...]).astype(BF16)
    qf = _dot(cqn, wuq_ref[...])
    qn = qf * lax.rsqrt(_seg_mean(qf * qf, gqk_ref, split=False) + EPS) * qgain_ref[...]
    rc, rs1, rs2 = rc_ref[...], rs1_ref[...], rs2_ref[...]
    for hh in range(MLA_HEADS):
        blk = qn[:, hh * HEAD_PAD:(hh + 1) * HEAD_PAD]
        half = MLA_ROPE // 2
        rot = blk * rc + pltpu.roll(blk, HEAD_PAD - half, 1) * rs1 + pltpu.roll(blk, half, 1) * rs2
        q_ref[:, hh * HEAD_PAD:(hh + 1) * HEAD_PAD] = rot.astype(BF16)

    ckvn = _rms(zkv[:, :KV_RANK], kvag_ref[...]).astype(BF16)
    v_ref[...] = (_dot(ckvn, wuv_ref[...]) + vones_ref[...]).astype(BF16)
    knt = lax.dot_general(wukt_ref[...], ckvn, (((1,), (1,)), ((), ())), preferred_element_type=F32)
    x = zkv[:, KV_RANK:].T[0:MLA_ROPE]
    xn = x * lax.rsqrt(jnp.mean(x * x, axis=0, keepdims=True) + EPS) * kpeg_ref[...]
    x1, x2 = xn[:MLA_ROPE // 2], xn[MLA_ROPE // 2:]
    cos, sin = cost_ref[0], sint_ref[0]
    kpe = jnp.concatenate([x1 * cos - x2 * sin, x2 * cos + x1 * sin], axis=0)
    pad = jnp.zeros((HEAD_PAD - MLA_QK, t), F32)
    kng = kng_ref[...]
    for hh in range(MLA_HEADS):
        blk = knt[hh * MLA_NOPE:(hh + 1) * MLA_NOPE]
        kn = blk * lax.rsqrt(jnp.mean(blk * blk, axis=0, keepdims=True) + EPS) * kng
        kt_ref[0, hh * HEAD_PAD:(hh + 1) * HEAD_PAD, :] = jnp.concatenate([kn, kpe, pad], axis=0).astype(BF16)


def _full(shape):
    nd = len(shape)
    return pl.BlockSpec(shape, lambda *_: (0,) * nd)


def _mixer_pre(h, rope, lw, batch, seq):
    n = h.shape[0]
    t = MIX_TILE
    tps = seq // t
    row = lambda w: pl.BlockSpec((t, w), lambda i: (i, 0))
    tspec = pl.BlockSpec((1, MLA_ROPE // 2, t), lambda i: (i // tps, 0, i % tps))
    weights = [lw[k] for k in (
        "mix_g", "w_sgu", "w_conv", "w_cq", "w_ckv",
        "sgu_ln_g", "sgu_ln_b", "sgu_wcat", "sgu_bias", "g64", "bg_a",
        "conv_w", "conv_b", "conv_ln_g", "conv_ln_b", "pw_w", "pw_b", "bg_b",
        "qa_g", "w_uq", "gqk", "q_gain",
        "kva_g", "w_ukt", "w_uv", "v_ones", "kn_g", "kpe_g")]
    in_specs = [row(D_MODEL), row(LANES), row(LANES), row(LANES), tspec, tspec] + [_full(w.shape) for w in weights]
    out_shape = (
        jax.ShapeDtypeStruct((n, SGU_WIDTH), F32),
        jax.ShapeDtypeStruct((n, CONV_WIDTH), F32),
        jax.ShapeDtypeStruct((n, MLA_HEADS * HEAD_PAD), BF16),
        jax.ShapeDtypeStruct((batch, MLA_HEADS * HEAD_PAD, seq), BF16),
        jax.ShapeDtypeStruct((n, MLA_HEADS * HEAD_PAD), BF16),
    )
    out_specs = (
        row(SGU_WIDTH), row(CONV_WIDTH), row(MLA_HEADS * HEAD_PAD),
        pl.BlockSpec((1, MLA_HEADS * HEAD_PAD, t), lambda i: (i // tps, 0, i % tps)),
        row(MLA_HEADS * HEAD_PAD),
    )
    return pl.pallas_call(
        functools.partial(_mixer_pre_kernel, tiles_per_seq=tps),
        grid=(n // t,),
        in_specs=in_specs,
        out_specs=out_specs,
        out_shape=out_shape,
        scratch_shapes=[pltpu.VMEM((t + CONV_HALO, CONV_WIDTH), F32),
                        pltpu.VMEM((SUBLANES - 1, CONV_SHIFTED_ROWS, CONV_WIDTH), F32)],
        compiler_params=pltpu.CompilerParams(dimension_semantics=("arbitrary",), vmem_limit_bytes=VMEM_LIMIT),
        name="mixer_pre",
    )(h, rope["rc"], rope["rs1"], rope["rs2"], rope["cos_t"], rope["sin_t"], *weights)


def _attention_kernel(q_ref, kt_ref, v_ref, o_ref, *, seq):
    tq = ATT_TQ
    exp2_scale = MLA_QK ** -0.5 * math.log2(math.e)
    row = lax.broadcasted_iota(jnp.int32, (tq, tq), 0)
    col = lax.broadcasted_iota(jnp.int32, (tq, tq), 1)
    lane = lax.broadcasted_iota(jnp.int32, (tq, HEAD_PAD), 1)
    def scores(qi, hh):
        nk = (qi + 1) * tq
        q = q_ref[qi * tq:(qi + 1) * tq, hh * HEAD_PAD:(hh + 1) * HEAD_PAD]
        s = _dot(q, kt_ref[0, hh * HEAD_PAD:(hh + 1) * HEAD_PAD, 0:nk])
        diag = jnp.where(col <= row, s[:, nk - tq:], NEG_BIG)
        return diag if qi == 0 else jnp.concatenate([s[:, :nk - tq], diag], axis=1)

    items = [(qi, hh) for qi in range(seq // tq) for hh in range(2)]
    ahead = [scores(*it) for it in items[:ATT_LOOKAHEAD]]
    outs = []
    for idx, (qi, hh) in enumerate(items):
        s = ahead.pop(0)
        if idx + ATT_LOOKAHEAD < len(items):
            ahead.append(scores(*items[idx + ATT_LOOKAHEAD]))
        p = jnp.exp2(((s - jnp.max(s, axis=-1, keepdims=True)) * exp2_scale).astype(BF16))
        acc = _dot(p, v_ref[0:(qi + 1) * tq, :])[:, hh * HEAD_PAD:(hh + 1) * HEAD_PAD]
        outs.append(acc / pltpu.roll(acc, MLA_V, 1))
        if hh == 1:
            o_ref[qi * tq:(qi + 1) * tq, :] = jnp.where(lane < MLA_V, outs[0], outs[1])
            outs = []


def _attention(q, kt, v, batch, seq):
    n = q.shape[0]
    pairs = MLA_HEADS // 2
    return pl.pallas_call(
        functools.partial(_attention_kernel, seq=seq),
        grid=(batch, pairs),
        in_specs=[
            pl.BlockSpec((seq, 2 * HEAD_PAD), lambda b, p: (b, p)),
            pl.BlockSpec((1, 2 * HEAD_PAD, seq), lambda b, p: (b, p, 0)),
            pl.BlockSpec((seq, 2 * HEAD_PAD), lambda b, p: (b, p)),
        ],
        out_specs=pl.BlockSpec((seq, 2 * MLA_V), lambda b, p: (b, p)),
        out_shape=jax.ShapeDtypeStruct((n, MLA_HEADS * MLA_V), F32),
        compiler_params=pltpu.CompilerParams(
            dimension_semantics=("arbitrary", "arbitrary"), vmem_limit_bytes=VMEM_LIMIT),
        name="attention",
    )(q, kt, v)


def _outproj_router_kernel(h_ref, ya_ref, yb_ref, yc_ref, bgc_ref, woa_ref, wob_ref, woc_ref,
                           ffng_ref, wrh_ref, wrl_ref, br_ref,
                           h1_ref, m_ref, eidx_ref, wts_ref):
    t = MIX_TILE
    ycn = _rms(yc_ref[...], bgc_ref[...])
    proj = (_dot(ya_ref[...].astype(BF16), woa_ref[...]) + _dot(yb_ref[...].astype(BF16), wob_ref[...])
            + _dot(ycn.astype(BF16), woc_ref[...]))
    h1 = h_ref[...] + proj
    h1_ref[...] = h1
    m = _rms(h1, ffng_ref[...])
    _store_row_tiled(m_ref, m)

    mh = m.astype(BF16)
    ml = (m - mh.astype(F32)).astype(BF16)
    wrh = wrh_ref[...]
    logits = _dot(mh, wrh) + _dot(ml, wrh) + _dot(mh, wrl_ref[...]) + br_ref[...]
    lt = logits.T
    rowi = lax.broadcasted_iota(jnp.int32, (EXPERTS_PER_GROUP, t), 0)
    g8 = lt[0:8]
    gmax = jnp.max(g8, axis=0, keepdims=True)
    gsum = jnp.sum(jnp.exp(g8 - gmax), axis=0, keepdims=True)
    gidx = jnp.min(jnp.where(g8 == gmax, rowi, 8), axis=0, keepdims=True)
    g_w = 1.0 / gsum
    esel = jnp.zeros((EXPERTS_PER_GROUP, t), F32)
    for g in range(N_GROUPS):
        esel = jnp.where(gidx == g, lt[8 + g * EXPERTS_PER_GROUP:8 + (g + 1) * EXPERTS_PER_GROUP], esel)
    ep = jnp.exp(esel - jnp.max(esel, axis=0, keepdims=True))
    eprob = ep / jnp.sum(ep, axis=0, keepdims=True)
    v1 = jnp.max(eprob, axis=0, keepdims=True)
    i1 = jnp.min(jnp.where(eprob == v1, rowi, 8), axis=0, keepdims=True)
    rest = jnp.where(rowi == i1, -1.0, eprob)
    v2 = jnp.max(rest, axis=0, keepdims=True)
    i2 = jnp.min(jnp.where(rest == v2, rowi, 8), axis=0, keepdims=True)
    den = v1 + v2
    e1 = gidx * EXPERTS_PER_GROUP + i1
    e2 = gidx * EXPERTS_PER_GROUP + i2
    eidx_ref[...] = jnp.where(rowi == 0, e1, jnp.where(rowi == 1, e2, 0))
    wts_ref[...] = jnp.where(rowi == 0, g_w * (v1 / den), jnp.where(rowi == 1, g_w * (v2 / den), 0.0))


def _outproj_router(h, ya, yb, yc, lw):
    n = h.shape[0]
    t = MIX_TILE
    row = lambda w: pl.BlockSpec((t, w), lambda i: (i, 0))
    weights = [lw[k] for k in ("bg_c", "w_o_a", "w_o_b", "w_o_c", "ffn_g", "wr_hi", "wr_lo", "br")]
    colspec = pl.BlockSpec((8, t), lambda i: (0, i))
    return pl.pallas_call(
        _outproj_router_kernel,
        grid=(n // t,),
        in_specs=[row(D_MODEL), row(SGU_WIDTH), row(CONV_WIDTH), row(MLA_HEADS * MLA_V)]
        + [_full(w.shape) for w in weights],
        out_specs=(row(D_MODEL), pl.BlockSpec((t * ROW_TILES, LANES), lambda i: (i, 0)), colspec, colspec),
        out_shape=(
            jax.ShapeDtypeStruct((n, D_MODEL), F32),
            jax.ShapeDtypeStruct((n * ROW_TILES, LANES), F32),
            jax.ShapeDtypeStruct((8, n), jnp.int32),
            jax.ShapeDtypeStruct((8, n), F32),
        ),
        compiler_params=pltpu.CompilerParams(dimension_semantics=("arbitrary",), vmem_limit_bytes=VMEM_LIMIT),
        name="outproj_router",
    )(h, ya, yb, yc, *weights)


def _moe_rank_kernel(e_ref, upper_ref, ones_ref, ltri_ref, dest_ref, cnt_ref, run_ref, base_ref):
    phase = pl.program_id(0)
    step = pl.program_id(1)
    c = RANK_CHUNK
    expert = lax.broadcasted_iota(jnp.int32, (N_EXPERTS, c), 0)
    onehot = jnp.where(expert == e_ref[...], 1.0, 0.0)
    oh16 = onehot.astype(BF16)

    @pl.when(jnp.logical_and(phase == 0, step == 0))
    def _():
        run_ref[...] = jnp.zeros_like(run_ref)
        base_ref[...] = jnp.zeros_like(base_ref)

    @pl.when(jnp.logical_and(phase == 1, step == 0))
    def _():
        blocks = jnp.floor((run_ref[...] + (EXPERT_ROWS - 1)) * (1.0 / EXPERT_ROWS))
        base_ref[...] = _dot(ltri_ref[...], blocks.astype(BF16)) * EXPERT_ROWS
        run_ref[...] = jnp.zeros_like(run_ref)

    @pl.when(phase == 0)
    def _():
        dest_ref[...] = jnp.zeros_like(dest_ref)

    @pl.when(phase == 1)
    def _():
        before = _dot(oh16, upper_ref[...])
        pos = before + run_ref[:, 0:1] + base_ref[:, 0:1]
        dest_ref[...] = jnp.sum(onehot * pos, axis=0, keepdims=True).astype(jnp.int32)

    run_ref[...] = run_ref[...] + _dot(oh16, ones_ref[...])
    cnt_ref[...] = run_ref[...]


def _moe_rank(e_flat, consts):
    total = e_flat.shape[1]
    c = RANK_CHUNK
    return pl.pallas_call(
        _moe_rank_kernel,
        grid=(2, total // c),
        in_specs=[pl.BlockSpec((1, c), lambda p, s: (0, s)),
                  _full((c, c)), _full((c, LANES)), _full((N_EXPERTS, N_EXPERTS))],
        out_specs=(pl.BlockSpec((1, c), lambda p, s: (0, s * p)),
                   pl.BlockSpec((N_EXPERTS, LANES), lambda p, s: (0, 0))),
        out_shape=(jax.ShapeDtypeStruct((1, total), jnp.int32),
                   jax.ShapeDtypeStruct((N_EXPERTS, LANES), F32)),
        scratch_shapes=[pltpu.VMEM((N_EXPERTS, LANES), F32), pltpu.VMEM((N_EXPERTS, LANES), F32)],
        compiler_params=pltpu.CompilerParams(dimension_semantics=("arbitrary", "arbitrary")),
        name="moe_rank",
    )(e_flat, consts["upper"], consts["ones"], consts["ltri"])


def _store_row_tiled(ref, x, offset=0):
    rows = x.shape[0]
    for s in range(ROW_TILES):
        ref[pl.ds(offset + s, rows, stride=ROW_TILES), :] = x[:, s * LANES:(s + 1) * LANES]


def _load_row_tiled(ref, rows, offset=0):
    return jnp.concatenate(
        [ref[pl.ds(offset + s, rows, stride=ROW_TILES), :] for s in range(ROW_TILES)], axis=1)


def _row_copy(src, src_row, dst, dst_row, sem):
    return pltpu.make_async_copy(
        src.at[pl.ds(pl.multiple_of(src_row * ROW_TILES, ROW_TILES), ROW_TILES)],
        dst.at[pl.ds(pl.multiple_of(dst_row * ROW_TILES, ROW_TILES), ROW_TILES)], sem)


def _rows_wait(src, dst, dst_row, rows, sem):
    pltpu.make_async_copy(
        src.at[pl.ds(0, rows * ROW_TILES)],
        dst.at[pl.ds(pl.multiple_of(dst_row * ROW_TILES, ROW_TILES), rows * ROW_TILES)], sem).wait()


def _moe_invert_kernel(dest_ref, cnt_ref, region_ref, inv_ref, *, n_tok):
    def clear(s, carry):
        inv_ref[s] = 0
        return carry

    def clear_padding(e, carry):
        lax.fori_loop(region_ref[e] + cnt_ref[e], region_ref[e + 1], clear, 0)
        return carry

    def place(j, carry):
        for k in range(TOP_K):
            inv_ref[dest_ref[k * n_tok + j]] = j
        return carry

    lax.fori_loop(0, N_EXPERTS, clear_padding, 0)
    lax.fori_loop(region_ref[N_EXPERTS], inv_ref.shape[0], clear, 0)
    lax.fori_loop(0, n_tok, place, 0, unroll=DMA_UNROLL)


def _moe_invert(dest, cnt, region, rows, n_tok):
    smem = pl.BlockSpec(memory_space=pltpu.SMEM)
    return pl.pallas_call(
        functools.partial(_moe_invert_kernel, n_tok=n_tok),
        in_specs=[smem, smem, smem],
        out_specs=smem,
        out_shape=jax.ShapeDtypeStruct((rows,), jnp.int32),
        name="moe_invert",
    )(dest, cnt, region)


def _moe_experts_kernel(inv_ref, blk_e_ref, nact_ref, m_ref, wg_ref, wu_ref, wd_ref, y_ref,
                        xbuf_ref, wgu_ref, wdn_ref, sem):
    r = EXPERT_ROWS
    b = pl.program_id(0)
    nact = nact_ref[0]

    def gather(blk, slot):
        def issue(g, carry):
            for u in range(DMA_UNROLL):
                i = g * DMA_UNROLL + u
                _row_copy(m_ref, inv_ref[blk * r + i], xbuf_ref, slot * r + i, sem.at[slot]).start(priority=u % 2)
            return carry
        lax.fori_loop(0, r // DMA_UNROLL, issue, 0)

    @pl.when(b == 0)
    def _():
        gather(0, 0)

    @pl.when(b + 1 < nact)
    def _():
        gather(b + 1, (b + 1) % 2)

    @pl.when(b < nact)
    def _():
        @pl.when(jnp.logical_or(b == 0, blk_e_ref[b] != blk_e_ref[jnp.maximum(b - 1, 0)]))
        def _():
            wgu_ref[:, :D_EXPERT] = wg_ref[0, 0].astype(BF16)
            wgu_ref[:, D_EXPERT:] = wu_ref[0, 0].astype(BF16)
            wdn_ref[...] = wd_ref[0, 0].astype(BF16)

        slot = b % 2
        _rows_wait(m_ref, xbuf_ref, slot * r, r, sem.at[slot])
        x = _load_row_tiled(xbuf_ref, r, slot * (r * ROW_TILES)).astype(BF16)
        gu = _dot(x, wgu_ref[...])
        hb = jax.nn.silu(gu[:, :D_EXPERT]) * gu[:, D_EXPERT:]
        _store_row_tiled(y_ref, _dot(hb.astype(BF16), wdn_ref[...]))

    @pl.when(b >= nact)
    def _():
        y_ref[...] = jnp.zeros_like(y_ref)


def _moe_experts(inv, blk_e, nact, m_rt, layer, w_gate, w_up, w_down):
    rows = inv.shape[0]
    r = EXPERT_ROWS
    wspec = lambda k, n: pl.BlockSpec((1, 1, k, n), lambda b, iv, be, na: (layer, be[b], 0, 0))
    return pl.pallas_call(
        _moe_experts_kernel,
        grid_spec=pltpu.PrefetchScalarGridSpec(
            num_scalar_prefetch=3,
            grid=(rows // r,),
            in_specs=[pl.BlockSpec(memory_space=pl.ANY),
                      wspec(D_MODEL, D_EXPERT), wspec(D_MODEL, D_EXPERT), wspec(D_EXPERT, D_MODEL)],
            out_specs=pl.BlockSpec((r * ROW_TILES, LANES), lambda b, iv, be, na: (b, 0)),
            scratch_shapes=[pltpu.VMEM((2 * r * ROW_TILES, LANES), F32),
                            pltpu.VMEM((D_MODEL, 2 * D_EXPERT), BF16),
                            pltpu.VMEM((D_EXPERT, D_MODEL), BF16),
                            pltpu.SemaphoreType.DMA((2,))],
        ),
        out_shape=jax.ShapeDtypeStruct((rows * ROW_TILES, LANES), F32),
        compiler_params=pltpu.CompilerParams(dimension_semantics=("arbitrary",), vmem_limit_bytes=VMEM_LIMIT),
        name="moe_experts",
    )(inv, blk_e, nact, m_rt, w_gate, w_up, w_down)


def _combine_ple_kernel(dest_ref, h1_ref, wt_ref, p_ref, y_ref, pleg_ref, gatew_ref, projw_ref, postg_ref,
                        out_ref, ybuf_ref, sem, *, n_tok):
    t = COMBINE_TILE
    i = pl.program_id(0)

    def gather(step, slot):
        def issue(g, carry):
            for u in range(DMA_UNROLL // TOP_K):
                r = g * (DMA_UNROLL // TOP_K) + u
                for k in range(TOP_K):
                    _row_copy(y_ref, dest_ref[k * n_tok + step * t + r], ybuf_ref, (slot * TOP_K + k) * t + r,
                              sem.at[slot]).start(priority=k % 2)
            return carry
        lax.fori_loop(0, t // (DMA_UNROLL // TOP_K), issue, 0)

    @pl.when(i == 0)
    def _():
        gather(0, 0)

    @pl.when(i + 1 < pl.num_programs(0))
    def _():
        gather(i + 1, (i + 1) % 2)

    e = _rms(_dot(p_ref[...].astype(BF16), projw_ref[...]), postg_ref[...])
    slot = i % 2
    _rows_wait(y_ref, ybuf_ref, slot * TOP_K * t, TOP_K * t, sem.at[slot])
    wt = wt_ref[...]
    y0 = _load_row_tiled(ybuf_ref, t, (slot * TOP_K) * (t * ROW_TILES))
    y1 = _load_row_tiled(ybuf_ref, t, (slot * TOP_K + 1) * (t * ROW_TILES))
    h2 = h1_ref[...] + (wt[:, 0:1] * y0 + wt[:, 1:2] * y1)
    gate = jax.nn.sigmoid(_dot(_rms(h2, pleg_ref[...]).astype(BF16), gatew_ref[...]))
    out_ref[...] = h2 + gate * e


def _combine_ple(dest, h1, wt, p, y_rt, lw):
    n = h1.shape[0]
    t = COMBINE_TILE
    row = lambda w: pl.BlockSpec((t, w), lambda i, d: (i, 0))
    weights = [lw[k] for k in ("ple_g", "gate_w", "proj_w", "post_g")]
    return pl.pallas_call(
        functools.partial(_combine_ple_kernel, n_tok=n),
        grid_spec=pltpu.PrefetchScalarGridSpec(
            num_scalar_prefetch=1,
            grid=(n // t,),
            in_specs=[row(D_MODEL), row(TOP_K), row(p.shape[1]), pl.BlockSpec(memory_space=pl.ANY)]
            + [pl.BlockSpec(w.shape, lambda i, d, nd=w.ndim: (0,) * nd) for w in weights],
            out_specs=row(D_MODEL),
            scratch_shapes=[pltpu.VMEM((2 * TOP_K * t * ROW_TILES, LANES), F32), pltpu.SemaphoreType.DMA((2,))],
        ),
        out_shape=jax.ShapeDtypeStruct((n, D_MODEL), F32),
        compiler_params=pltpu.CompilerParams(dimension_semantics=("arbitrary",), vmem_limit_bytes=VMEM_LIMIT),
        name="combine_ple",
    )(dest, h1, wt, p, y_rt, *weights)


def _segment_matrix(seg_ids):
    seg_ids = jnp.asarray(seg_ids)
    same = (seg_ids[:, None] == seg_ids[None, :]).astype(F32)
    return (same / jnp.sum(same, axis=1, keepdims=True)).astype(BF16)


def _constants():
    lane = jnp.arange(256)
    qk_seg = (lane // HEAD_PAD) * 3 + jnp.where(lane % HEAD_PAD < MLA_NOPE, 0, jnp.where(lane % HEAD_PAD < MLA_QK, 1, 2))
    i = jnp.arange(RANK_CHUNK)
    e = jnp.arange(N_EXPERTS)
    return {
        "g64": _segment_matrix(lane // HEAD_DIM),
        "gqk": _segment_matrix(qk_seg),
        "upper": (i[:, None] < i[None, :]).astype(BF16),
        "ones": jnp.ones((RANK_CHUNK, LANES), BF16),
        "ltri": (e[None, :] < e[:, None]).astype(BF16),
    }


def _rope_layouts(cos, sin, batch, seq):
    n, half = cos.shape
    ones = jnp.ones((n, MLA_NOPE), F32)
    zeros_n = jnp.zeros((n, MLA_NOPE), F32)
    zpad = jnp.zeros((n, HEAD_PAD - MLA_QK), F32)
    zhalf = jnp.zeros((n, half), F32)
    return {
        "rc": jnp.concatenate([ones, cos, cos, zpad], axis=1),
        "rs1": jnp.concatenate([zeros_n, -sin, zhalf, zpad], axis=1),
        "rs2": jnp.concatenate([zeros_n, zhalf, sin, zpad], axis=1),
        "cos_t": jnp.transpose(cos.reshape(batch, seq, half), (0, 2, 1)),
        "sin_t": jnp.transpose(sin.reshape(batch, seq, half), (0, 2, 1)),
    }


def _layer_weights(i, consts, p):
    row = lambda v: v.reshape(1, -1).astype(F32)
    w_in = p["w_in"][i]
    o_sgu, o_conv, o_q, o_kv = 2 * SGU_WIDTH, 2 * SGU_WIDTH + 2 * CONV_WIDTH, 0, 0
    o_q = o_conv + Q_RANK
    o_kv = o_q + KV_RANK
    w_ckv = jnp.concatenate([w_in[:, o_q:], jnp.zeros((D_MODEL, 256 - KV_RANK - MLA_ROPE), F32)], axis=1)

    causal = jnp.tril(jnp.ones((CHUNK, CHUNK), F32))
    wcat = jnp.transpose(p["sgu_w"][i] * causal, (1, 0, 2)).reshape(CHUNK, SGU_HEADS * CHUNK)
    sbias = jnp.repeat(jnp.transpose(p["sgu_b"][i]), HEAD_DIM, axis=1)

    w_uq = p["w_uq"][i].reshape(Q_RANK, MLA_HEADS, MLA_QK)
    w_uq = jnp.pad(w_uq, ((0, 0), (0, 0), (0, HEAD_PAD - MLA_QK))).reshape(Q_RANK, MLA_HEADS * HEAD_PAD)
    qn_g = p["q_norm_g"][i]
    q_gain = jnp.tile(jnp.concatenate([qn_g, jnp.zeros((HEAD_PAD - MLA_QK,), F32)]), MLA_HEADS)
    w_ukv = p["w_ukv"][i].reshape(KV_RANK, MLA_HEADS, MLA_NOPE + MLA_V)
    w_ukt = jnp.transpose(w_ukv[:, :, :MLA_NOPE], (1, 2, 0)).reshape(MLA_HEADS * MLA_NOPE, KV_RANK)
    w_v = w_ukv[:, :, MLA_NOPE:].reshape(KV_RANK, MLA_HEADS // 2, 2, MLA_V)
    zeros_v = jnp.zeros((KV_RANK, MLA_HEADS // 2, MLA_V), F32)
    w_uv = jnp.stack([w_v[:, :, 0], zeros_v, zeros_v, w_v[:, :, 1]], axis=2).reshape(KV_RANK, MLA_HEADS * HEAD_PAD)
    v_ones = jnp.tile(jnp.repeat(jnp.array([0.0, 1.0, 1.0, 0.0], F32), MLA_V), MLA_HEADS // 2)
    kn_g = p["k_norm_g"][i]

    bg = p["branch_norm_g"][i]
    w_o = p["w_o"][i]
    wr = jnp.concatenate([p["router_group_w"][i], jnp.zeros((D_MODEL, 8 - N_GROUPS), F32),
                          p["router_expert_w"][i], jnp.zeros((D_MODEL, LANES - 8 - N_EXPERTS), F32)], axis=1)
    wr_hi = wr.astype(BF16)
    br = jnp.concatenate([p["router_group_b"][i], jnp.full((8 - N_GROUPS,), NEG_BIG, F32),
                          p["router_expert_b"][i], jnp.zeros((LANES - 8 - N_EXPERTS,), F32)])
    return {
        "mix_g": row(p["mix_norm_g"][i]),
        "w_sgu": w_in[:, :o_sgu].astype(BF16),
        "w_conv": w_in[:, o_sgu:o_conv].astype(BF16),
        "w_cq": w_in[:, o_conv:o_q].astype(BF16),
        "w_ckv": w_ckv.astype(BF16),
        "sgu_ln_g": row(p["sgu_ln_g"][i]), "sgu_ln_b": row(p["sgu_ln_b"][i]),
        "sgu_wcat": wcat.astype(BF16), "sgu_bias": sbias, "g64": consts["g64"], "bg_a": row(bg[:SGU_WIDTH]),
        "conv_w": jnp.pad(p["conv_w"][i], ((0, 1), (0, 0))), "conv_b": row(p["conv_b"][i]),
        "conv_ln_g": row(p["conv_ln_g"][i]), "conv_ln_b": row(p["conv_ln_b"][i]),
        "pw_w": p["conv_pw_w"][i].astype(BF16), "pw_b": row(p["conv_pw_b"][i]),
        "bg_b": row(bg[SGU_WIDTH:SGU_WIDTH + CONV_WIDTH]),
        "qa_g": row(p["q_a_norm_g"][i]), "w_uq": w_uq.astype(BF16), "gqk": consts["gqk"],
        "q_gain": row(q_gain),
        "kva_g": row(p["kv_a_norm_g"][i]), "w_ukt": w_ukt.astype(BF16), "w_uv": w_uv.astype(BF16),
        "v_ones": row(v_ones),
        "kn_g": jnp.broadcast_to(kn_g[:MLA_NOPE, None], (MLA_NOPE, MIX_TILE)),
        "kpe_g": jnp.broadcast_to(kn_g[MLA_NOPE:, None], (MLA_ROPE, MIX_TILE)),
        "bg_c": row(bg[SGU_WIDTH + CONV_WIDTH:]),
        "w_o_a": w_o[:SGU_WIDTH].astype(BF16),
        "w_o_b": w_o[SGU_WIDTH:SGU_WIDTH + CONV_WIDTH].astype(BF16),
        "w_o_c": w_o[SGU_WIDTH + CONV_WIDTH:].astype(BF16),
        "ffn_g": row(p["ffn_norm_g"][i]),
        "wr_hi": wr_hi, "wr_lo": (wr - wr_hi.astype(F32)).astype(BF16), "br": row(br),
        "layer": i, "w_gate": p["moe_w_gate"], "w_up": p["moe_w_up"], "w_down": p["moe_w_down"],
        "ple_g": row(p["ple_norm_g"][i]), "gate_w": p["ple_gate_w"][i].astype(BF16),
        "proj_w": p["ple_proj_w"][i].astype(BF16), "post_g": row(p["ple_post_norm_g"][i]),
    }


def _moe(h1, m, eidx, wts, pl_i, lw, consts):
    n = h1.shape[0]
    total = TOP_K * n
    rows = total + N_EXPERTS * EXPERT_ROWS
    dest2d, counts = _moe_rank(eidx[:TOP_K].reshape(1, total), consts)
    dest = dest2d.reshape(total)
    cnt = counts[:, 0].astype(jnp.int32)
    padded = (cnt + EXPERT_ROWS - 1) // EXPERT_ROWS * EXPERT_ROWS
    pend = jnp.cumsum(padded)
    nblk = rows // EXPERT_ROWS
    blk_start = jnp.arange(nblk, dtype=jnp.int32) * EXPERT_ROWS
    blk_e = jnp.minimum(jnp.sum((pend[None, :] <= blk_start[:, None]).astype(jnp.int32), axis=1), N_EXPERTS - 1)
    nact = (pend[-1:] // EXPERT_ROWS).astype(jnp.int32)
    region = jnp.concatenate([jnp.zeros((1,), jnp.int32), pend.astype(jnp.int32)])
    inv = _moe_invert(dest, cnt, region, rows, n)
    y = _moe_experts(inv, blk_e, nact, m, lw["layer"], lw["w_gate"], lw["w_up"], lw["w_down"])
    return _combine_ple(dest, h1, jnp.transpose(wts[:TOP_K]), pl_i, y, lw)


def kernel(x, p, positions, mix_norm_g, w_in, sgu_ln_g, sgu_ln_b, sgu_w, sgu_b, conv_w, conv_b, conv_ln_g, conv_ln_b, conv_pw_w, conv_pw_b, q_a_norm_g, w_uq, kv_a_norm_g, w_ukv, q_norm_g, k_norm_g, branch_norm_g, w_o, ffn_norm_g, router_group_w, router_group_b, router_expert_w, router_expert_b, moe_w_gate, moe_w_up, moe_w_down, ple_norm_g, ple_gate_w, ple_proj_w, ple_post_norm_g):
    params = dict(
        mix_norm_g=mix_norm_g, w_in=w_in, sgu_ln_g=sgu_ln_g, sgu_ln_b=sgu_ln_b, sgu_w=sgu_w, sgu_b=sgu_b,
        conv_w=conv_w, conv_b=conv_b, conv_ln_g=conv_ln_g, conv_ln_b=conv_ln_b, conv_pw_w=conv_pw_w,
        conv_pw_b=conv_pw_b, q_a_norm_g=q_a_norm_g, w_uq=w_uq, kv_a_norm_g=kv_a_norm_g, w_ukv=w_ukv,
        q_norm_g=q_norm_g, k_norm_g=k_norm_g, branch_norm_g=branch_norm_g, w_o=w_o, ffn_norm_g=ffn_norm_g,
        router_group_w=router_group_w, router_group_b=router_group_b, router_expert_w=router_expert_w,
        router_expert_b=router_expert_b, moe_w_gate=moe_w_gate, moe_w_up=moe_w_up, moe_w_down=moe_w_down,
        ple_norm_g=ple_norm_g, ple_gate_w=ple_gate_w, ple_proj_w=ple_proj_w, ple_post_norm_g=ple_post_norm_g)
    batch, seq, d = x.shape
    n = batch * seq
    depth = w_in.shape[0]
    consts = _constants()
    cos, sin = _rope_tables(positions)
    rope = _rope_layouts(cos, sin, batch, seq)
    h = x.reshape(n, d)
    for i in range(depth):
        lw = _layer_weights(i, consts, params)
        ya, yb, q, kt, v = _mixer_pre(h, rope, lw, batch, seq)
        yc = _attention(q, kt, v, batch, seq)
        h1, m, eidx, wts = _outproj_router(h, ya, yb, yc, lw)
        h = _moe(h1, m, eidx, wts, p[i].reshape(n, -1), lw, consts)
    return h.reshape(batch, seq, d)
```

```python
import functools
import math

import jax
import jax.numpy as jnp
from jax import lax
from jax.experimental import pallas as pl
from jax.experimental.pallas import tpu as pltpu

F32 = jnp.float32
BF16 = jnp.bfloat16

D_MODEL = 1024
HEAD_DIM = 64
SGU_HEADS = 4
SGU_WIDTH = 256
CHUNK = 128
CONV_WIDTH = 256
CONV_KERNEL = 31
MLA_HEADS = 8
MLA_NOPE = 64
MLA_ROPE = 32
MLA_QK = MLA_NOPE + MLA_ROPE
MLA_V = 64
Q_RANK = 256
KV_RANK = 128
ROPE_THETA = 10000.0
N_GROUPS = 4
EXPERTS_PER_GROUP = 8
N_EXPERTS = 32
D_EXPERT = 256
TOP_K = 2
EPS = 1e-6

LANES = 128
SUBLANES = 8
HEAD_PAD = 128
MIX_TILE = 512
CONV_HALO = 32
CONV_SHIFTED_ROWS = MIX_TILE + CONV_HALO - SUBLANES
ATT_TQ = 256
ATT_LOOKAHEAD = 1
RANK_CHUNK = 512
EXPERT_ROWS = 256
DISPATCH_TILE = 256
COMBINE_TILE = 256
ROW_TILES = D_MODEL // LANES
DMA_UNROLL = 8
VMEM_LIMIT = 48 * 1024 * 1024
NEG_BIG = -1e30


def _dot(a, b):
    return jnp.dot(a, b, preferred_element_type=F32)


def _rms(x, g):
    ms = jnp.mean(x * x, axis=-1, keepdims=True)
    return x * lax.rsqrt(ms + EPS) * g


def _seg_mean(x, g_ref, split=True):
    g = g_ref[...]
    outs = []
    for c in range(x.shape[1] // 256):
        xb = x[:, c * 256:(c + 1) * 256]
        hi = xb.astype(BF16)
        acc = _dot(hi, g)
        if split:
            acc = acc + _dot((xb - hi.astype(F32)).astype(BF16), g)
        outs.append(acc)
    return outs[0] if len(outs) == 1 else jnp.concatenate(outs, axis=-1)


def _group_layernorm(x, g_ref, gain, bias):
    d = x - _seg_mean(x, g_ref)
    var = _seg_mean(d * d, g_ref)
    return d * lax.rsqrt(var + EPS) * gain + bias


def _rope_table_kernel(pos_ref, invf_ref, cos_ref, sin_ref):
    ang = pos_ref[...].astype(F32) * invf_ref[...]
    cos_ref[...] = jnp.cos(ang)
    sin_ref[...] = jnp.sin(ang)


def _rope_tables(positions):
    b, s = positions.shape
    n = b * s
    half = MLA_ROPE // 2
    inv_freq = ROPE_THETA ** (-jnp.arange(0, MLA_ROPE, 2, dtype=F32) / MLA_ROPE)
    pos_rep = jnp.repeat(positions.reshape(n), half).reshape(n * half // LANES, LANES)
    invf_rep = jnp.tile(inv_freq, LANES // half).reshape(1, LANES)
    shape = jax.ShapeDtypeStruct(pos_rep.shape, F32)
    cos, sin = pl.pallas_call(_rope_table_kernel, out_shape=(shape, shape), name="rope_tables")(pos_rep, invf_rep)
    return cos.reshape(n, half), sin.reshape(n, half)


def _mixer_pre_kernel(
        h_ref, rc_ref, rs1_ref, rs2_ref, cost_ref, sint_ref,
        mixg_ref, wsgu_ref, wconv_ref, wcq_ref, wckv_ref,
        slng_ref, slnb_ref, wcat_ref, sbias_ref, g64_ref, bga_ref,
        cw_ref, cb_ref, clng_ref, clnb_ref, pww_ref, pwb_ref, bgb_ref,
        qag_ref, wuq_ref, gqk_ref, qgain_ref,
        kvag_ref, wukt_ref, wuv_ref, vones_ref, kng_ref, kpeg_ref,
        ya_ref, yb_ref, q_ref, kt_ref, v_ref,
        ybuf_ref, ysh_ref, *, tiles_per_seq):
    t = MIX_TILE
    a = _rms(h_ref[...], mixg_ref[...]).astype(BF16)

    zs = _dot(a, wsgu_ref[...])
    zc = _dot(a, wconv_ref[...])
    zq = _dot(a, wcq_ref[...])
    zkv = _dot(a, wckv_ref[...])
    zg = jax.nn.gelu(zs)
    u = zg[:, :SGU_WIDTH]
    vn = _group_layernorm(zg[:, SGU_WIDTH:], g64_ref, slng_ref[...], slnb_ref[...])
    lane_head = lax.broadcasted_iota(jnp.int32, (CHUNK, SGU_WIDTH), 1) // HEAD_DIM
    wcat = wcat_ref[...]
    sbias = sbias_ref[...]
    parts = []
    for c in range(t // CHUNK):
        vc = vn[c * CHUNK:(c + 1) * CHUNK]
        stacked = jnp.concatenate(
            [jnp.where(lane_head == hh, vc, 0.0) for hh in range(SGU_HEADS)], axis=0).astype(BF16)
        s = _dot(wcat, stacked) + sbias
        parts.append(u[c * CHUNK:(c + 1) * CHUNK] * s)
    ya_ref[...] = _rms(jnp.concatenate(parts, axis=0), bga_ref[...])

    yg =zc[:, :CONV_WIDTH] * jax.nn.sigmoid(zc[:, CONV_WIDTH:])
    first = (pl.program_id(0) % tiles_per_seq) == 0

    @pl.when(first)
    def _():
        ybuf_ref[0:CONV_HALO, :] = jnp.zeros((CONV_HALO, CONV_WIDTH), F32)

    @pl.when(jnp.logical_not(first))
    def _():
        ybuf_ref[0:CONV_HALO, :] = ybuf_ref[t:t + CONV_HALO, :]

    ybuf_ref[CONV_HALO:CONV_HALO + t, :] = yg
    for sh in range(1, SUBLANES):
        ysh_ref[sh - 1] = ybuf_ref[pl.ds(sh, CONV_SHIFTED_ROWS), :]
    rows = 64
    first_tap_row = CONV_HALO - (CONV_KERNEL - 1)
    conv_parts = []
    for r in range(t // rows):
        acc = jnp.broadcast_to(cb_ref[...], (rows, CONV_WIDTH))
        for tap in range(CONV_KERNEL):
            sh = (first_tap_row + tap) % SUBLANES
            start = r * rows + first_tap_row + tap - sh
            src = ybuf_ref[pl.ds(start, rows), :] if sh == 0 else ysh_ref[sh - 1, pl.ds(start, rows), :]
            acc = acc + cw_ref[tap:tap + 1, :] * src
        conv_parts.append(acc)
    cv = jnp.concatenate(conv_parts, axis=0)
    cn = _group_layernorm(cv, g64_ref, clng_ref[...], clnb_ref[...])
    yb = _dot(jax.nn.silu(cn).astype(BF16), pww_ref[...]) + pwb_ref[...]
    yb_ref[...] = _rms(yb, bgb_ref[...])

    cqn = _rms(zq, qag_ref[...]).astype(BF16)
    qf = _dot(cqn, wuq_ref[...])
    qn = qf * lax.rsqrt(_seg_mean(qf * qf, gqk_ref, split=False) + EPS) * qgain_ref[...]
    rc, rs1, rs2 = rc_ref[...], rs1_ref[...], rs2_ref[...]
    for hh in range(MLA_HEADS):
        blk = qn[:, hh * HEAD_PAD:(hh + 1) * HEAD_PAD]
        half = MLA_ROPE // 2
        rot = blk * rc + pltpu.roll(blk, HEAD_PAD - half, 1) * rs1 + pltpu.roll(blk, half, 1) * rs2
        q_ref[:, hh * HEAD_PAD:(hh + 1) * HEAD_PAD] = rot.astype(BF16)

    ckvn = _rms(zkv[:, :KV_RANK], kvag_ref[...]).astype(BF16)
    v_ref[...] = (_dot(ckvn, wuv_ref[...]) + vones_ref[...]).astype(BF16)
    knt = lax.dot_general(wukt_ref[...], ckvn, (((1,), (1,)), ((), ())), preferred_element_type=F32)
    x = zkv[:, KV_RANK:].T[0:MLA_ROPE]
    xn = x * lax.rsqrt(jnp.mean(x * x, axis=0, keepdims=True) + EPS) * kpeg_ref[...]
    x1, x2 = xn[:MLA_ROPE // 2], xn[MLA_ROPE // 2:]
    cos, sin = cost_ref[0], sint_ref[0]
    kpe = jnp.concatenate([x1 * cos - x2 * sin, x2 * cos + x1 * sin], axis=0)
    pad = jnp.zeros((HEAD_PAD - MLA_QK, t), F32)
    kng = kng_ref[...]
    for hh in range(MLA_HEADS):
        blk = knt[hh * MLA_NOPE:(hh + 1) * MLA_NOPE]
        kn = blk * lax.rsqrt(jnp.mean(blk * blk, axis=0, keepdims=True) + EPS) * kng
        kt_ref[0, hh * HEAD_PAD:(hh + 1) * HEAD_PAD, :] = jnp.concatenate([kn, kpe, pad], axis=0).astype(BF16)


def _full(shape):
    nd = len(shape)
    return pl.BlockSpec(shape, lambda *_: (0,) * nd)


def _mixer_pre(h, rope, lw, batch, seq):
    n = h.shape[0]
    t = MIX_TILE
    tps = seq // t
    row = lambda w: pl.BlockSpec((t, w), lambda i: (i, 0))
    tspec = pl.BlockSpec((1, MLA_ROPE // 2, t), lambda i: (i // tps, 0, i % tps))
    weights = [lw[k] for k in (
        "mix_g", "w_sgu", "w_conv", "w_cq", "w_ckv",
        "sgu_ln_g", "sgu_ln_b", "sgu_wcat", "sgu_bias", "g64", "bg_a",
        "conv_w", "conv_b", "conv_ln_g", "conv_ln_b", "pw_w", "pw_b", "bg_b",
        "qa_g", "w_uq", "gqk", "q_gain",
        "kva_g", "w_ukt", "w_uv", "v_ones", "kn_g", "kpe_g")]
    in_specs = [row(D_MODEL), row(LANES), row(LANES), row(LANES), tspec, tspec] + [_full(w.shape) for w in weights]
    out_shape = (
        jax.ShapeDtypeStruct((n, SGU_WIDTH), F32),
        jax.ShapeDtypeStruct((n, CONV_WIDTH), F32),
        jax.ShapeDtypeStruct((n, MLA_HEADS * HEAD_PAD), BF16),
        jax.ShapeDtypeStruct((batch, MLA_HEADS * HEAD_PAD, seq), BF16),
        jax.ShapeDtypeStruct((n, MLA_HEADS * HEAD_PAD), BF16),
    )
    out_specs = (
        row(SGU_WIDTH), row(CONV_WIDTH), row(MLA_HEADS * HEAD_PAD),
        pl.BlockSpec((1, MLA_HEADS * HEAD_PAD, t), lambda i: (i // tps, 0, i % tps)),
        row(MLA_HEADS * HEAD_PAD),
    )
    return pl.pallas_call(
        functools.partial(_mixer_pre_kernel, tiles_per_seq=tps),
        grid=(n // t,),
        in_specs=in_specs,
        out_specs=out_specs,
        out_shape=out_shape,
        scratch_shapes=[pltpu.VMEM((t + CONV_HALO, CONV_WIDTH), F32),
                        pltpu.VMEM((SUBLANES - 1, CONV_SHIFTED_ROWS, CONV_WIDTH), F32)],
        compiler_params=pltpu.CompilerParams(dimension_semantics=("arbitrary",), vmem_limit_bytes=VMEM_LIMIT),
        name="mixer_pre",
    )(h, rope["rc"], rope["rs1"], rope["rs2"], rope["cos_t"], rope["sin_t"], *weights)


def _attention_kernel(q_ref, kt_ref, v_ref, o_ref, *, seq):
    tq = ATT_TQ
    exp2_scale = MLA_QK ** -0.5 * math.log2(math.e)
    row = lax.broadcasted_iota(jnp.int32, (tq, tq), 0)
    col = lax.broadcasted_iota(jnp.int32, (tq, tq), 1)
    lane = lax.broadcasted_iota(jnp.int32, (tq, HEAD_PAD), 1)
    def scores(qi, hh):
        nk = (qi + 1) * tq
        q = q_ref[qi * tq:(qi + 1) * tq, hh * HEAD_PAD:(hh + 1) * HEAD_PAD]
        s = _dot(q, kt_ref[0, hh * HEAD_PAD:(hh + 1) * HEAD_PAD, 0:nk])
        diag = jnp.where(col <= row, s[:, nk - tq:], NEG_BIG)
        return diag if qi == 0 else jnp.concatenate([s[:, :nk - tq], diag], axis=1)

    items = [(qi, hh) for qi in range(seq // tq) for hh in range(2)]
    ahead = [scores(*it) for it in items[:ATT_LOOKAHEAD]]
    outs = []
    for idx, (qi, hh) in enumerate(items):
        s = ahead.pop(0)
        if idx + ATT_LOOKAHEAD < len(items):
            ahead.append(scores(*items[idx + ATT_LOOKAHEAD]))
        p = jnp.exp2(((s - jnp.max(s, axis=-1, keepdims=True)) * exp2_scale).astype(BF16))
        acc = _dot(p, v_ref[0:(qi + 1) * tq, :])[:, hh * HEAD_PAD:(hh + 1) * HEAD_PAD]
        outs.append(acc / pltpu.roll(acc, MLA_V, 1))
        if hh == 1:
            o_ref[qi * tq:(qi + 1) * tq, :] = jnp.where(lane < MLA_V, outs[0], outs[1])
            outs = []


def _attention(q, kt, v, batch, seq):
    n = q.shape[0]
    pairs = MLA_HEADS // 2
    return pl.pallas_call(
        functools.partial(_attention_kernel, seq=seq),
        grid=(batch, pairs),
        in_specs=[
            pl.BlockSpec((seq, 2 * HEAD_PAD), lambda b, p: (b, p)),
            pl.BlockSpec((1, 2 * HEAD_PAD, seq), lambda b, p: (b, p, 0)),
            pl.BlockSpec((seq, 2 * HEAD_PAD), lambda b, p: (b, p)),
        ],
        out_specs=pl.BlockSpec((seq, 2 * MLA_V), lambda b, p: (b, p)),
        out_shape=jax.ShapeDtypeStruct((n, MLA_HEADS * MLA_V), F32),
        compiler_params=pltpu.CompilerParams(
            dimension_semantics=("arbitrary", "arbitrary"), vmem_limit_bytes=VMEM_LIMIT),
        name="attention",
    )(q, kt, v)


def _outproj_router_kernel(h_ref, ya_ref, yb_ref, yc_ref, bgc_ref, woa_ref, wob_ref, woc_ref,
                           ffng_ref, wrh_ref, wrl_ref, br_ref,
                           h1_ref, m_ref, eidx_ref, wts_ref):
    t = MIX_TILE
    ycn = _rms(yc_ref[...], bgc_ref[...])
    proj = (_dot(ya_ref[...].astype(BF16), woa_ref[...]) + _dot(yb_ref[...].astype(BF16), wob_ref[...])
            + _dot(ycn.astype(BF16), woc_ref[...]))
    h1 = h_ref[...] + proj
    h1_ref[...] = h1
    m = _rms(h1, ffng_ref[...])
    _store_row_tiled(m_ref, m)

    mh = m.astype(BF16)
    ml = (m - mh.astype(F32)).astype(BF16)
    wrh = wrh_ref[...]
    logits = _dot(mh, wrh) + _dot(ml, wrh) + _dot(mh, wrl_ref[...]) + br_ref[...]
    lt = logits.T
    rowi = lax.broadcasted_iota(jnp.int32, (EXPERTS_PER_GROUP, t), 0)
    g8 = lt[0:8]
    gmax = jnp.max(g8, axis=0, keepdims=True)
    gsum = jnp.sum(jnp.exp(g8 - gmax), axis=0, keepdims=True)
    gidx = jnp.min(jnp.where(g8 == gmax, rowi, 8), axis=0, keepdims=True)
    g_w = 1.0 / gsum
    esel = jnp.zeros((EXPERTS_PER_GROUP, t), F32)
    for g in range(N_GROUPS):
        esel = jnp.where(gidx == g, lt[8 + g * EXPERTS_PER_GROUP:8 + (g + 1) * EXPERTS_PER_GROUP], esel)
    ep = jnp.exp(esel - jnp.max(esel, axis=0, keepdims=True))
    eprob = ep / jnp.sum(ep, axis=0, keepdims=True)
    v1 = jnp.max(eprob, axis=0, keepdims=True)
    i1 = jnp.min(jnp.where(eprob == v1, rowi, 8), axis=0, keepdims=True)
    rest = jnp.where(rowi == i1, -1.0, eprob)
    v2 = jnp.max(rest, axis=0, keepdims=True)
    i2 = jnp.min(jnp.where(rest == v2, rowi, 8), axis=0, keepdims=True)
    den = v1 + v2
    e1 = gidx * EXPERTS_PER_GROUP + i1
    e2 = gidx * EXPERTS_PER_GROUP + i2
    eidx_ref[...] = jnp.where(rowi == 0, e1, jnp.where(rowi == 1, e2, 0))
    wts_ref[...] = jnp.where(rowi == 0, g_w * (v1 / den), jnp.where(rowi == 1, g_w * (v2 / den), 0.0))


def _outproj_router(h, ya, yb, yc, lw):
    n = h.shape[0]
    t = MIX_TILE
    row = lambda w: pl.BlockSpec((t, w), lambda i: (i, 0))
    weights = [lw[k] for k in ("bg_c", "w_o_a", "w_o_b", "w_o_c", "ffn_g", "wr_hi", "wr_lo", "br")]
    colspec = pl.BlockSpec((8, t), lambda i: (0, i))
    return pl.pallas_call(
        _outproj_router_kernel,
        grid=(n // t,),
        in_specs=[row(D_MODEL), row(SGU_WIDTH), row(CONV_WIDTH), row(MLA_HEADS * MLA_V)]
        + [_full(w.shape) for w in weights],
        out_specs=(row(D_MODEL), pl.BlockSpec((t * ROW_TILES, LANES), lambda i: (i, 0)), colspec, colspec),
        out_shape=(
            jax.ShapeDtypeStruct((n, D_MODEL), F32),
            jax.ShapeDtypeStruct((n * ROW_TILES, LANES), F32),
            jax.ShapeDtypeStruct((8, n), jnp.int32),
            jax.ShapeDtypeStruct((8, n), F32),
        ),
        compiler_params=pltpu.CompilerParams(dimension_semantics=("arbitrary",), vmem_limit_bytes=VMEM_LIMIT),
        name="outproj_router",
    )(h, ya, yb, yc, *weights)


def _moe_rank_kernel(e_ref, upper_ref, ones_ref, ltri_ref, dest_ref, cnt_ref, run_ref, base_ref):
    phase = pl.program_id(0)
    step = pl.program_id(1)
    c = RANK_CHUNK
    expert = lax.broadcasted_iota(jnp.int32, (N_EXPERTS, c), 0)
    onehot = jnp.where(expert == e_ref[...], 1.0, 0.0)
    oh16 = onehot.astype(BF16)

    @pl.when(jnp.logical_and(phase == 0, step == 0))
    def _():
        run_ref[...] = jnp.zeros_like(run_ref)
        base_ref[...] = jnp.zeros_like(base_ref)

    @pl.when(jnp.logical_and(phase == 1, step == 0))
    def _():
        blocks = jnp.floor((run_ref[...] + (EXPERT_ROWS - 1)) * (1.0 / EXPERT_ROWS))
        base_ref[...] = _dot(ltri_ref[...], blocks.astype(BF16)) * EXPERT_ROWS
        run_ref[...] = jnp.zeros_like(run_ref)

    @pl.when(phase == 0)
    def _():
        dest_ref[...] = jnp.zeros_like(dest_ref)

    @pl.when(phase == 1)
    def _():
        before = _dot(oh16, upper_ref[...])
        pos = before + run_ref[:, 0:1] + base_ref[:, 0:1]
        dest_ref[...] = jnp.sum(onehot * pos, axis=0, keepdims=True).astype(jnp.int32)

    run_ref[...] = run_ref[...] + _dot(oh16, ones_ref[...])
    cnt_ref[...] = run_ref[...]


def _moe_rank(e_flat, consts):
    total = e_flat.shape[1]
    c = RANK_CHUNK
    return pl.pallas_call(
        _moe_rank_kernel,
        grid=(2, total // c),
        in_specs=[pl.BlockSpec((1, c), lambda p, s: (0, s)),
                  _full((c, c)), _full((c, LANES)), _full((N_EXPERTS, N_EXPERTS))],
        out_specs=(pl.BlockSpec((1, c), lambda p, s: (0, s * p)),
                   pl.BlockSpec((N_EXPERTS, LANES), lambda p, s: (0, 0))),
        out_shape=(jax.ShapeDtypeStruct((1, total), jnp.int32),
                   jax.ShapeDtypeStruct((N_EXPERTS, LANES), F32)),
        scratch_shapes=[pltpu.VMEM((N_EXPERTS, LANES), F32), pltpu.VMEM((N_EXPERTS, LANES), F32)],
        compiler_params=pltpu.CompilerParams(dimension_semantics=("arbitrary", "arbitrary")),
        name="moe_rank",
    )(e_flat, consts["upper"], consts["ones"], consts["ltri"])


def _store_row_tiled(ref, x, offset=0):
    rows = x.shape[0]
    for s in range(ROW_TILES):
        ref[pl.ds(offset + s, rows, stride=ROW_TILES), :] = x[:, s * LANES:(s + 1) * LANES]


def _load_row_tiled(ref, rows, offset=0):
    return jnp.concatenate(
        [ref[pl.ds(offset + s, rows, stride=ROW_TILES), :] for s in range(ROW_TILES)], axis=1)


def _row_copy(src, src_row, dst, dst_row, sem):
    return pltpu.make_async_copy(
        src.at[pl.ds(pl.multiple_of(src_row * ROW_TILES, ROW_TILES), ROW_TILES)],
        dst.at[pl.ds(pl.multiple_of(dst_row * ROW_TILES, ROW_TILES), ROW_TILES)], sem)


def _rows_wait(src, dst, dst_row, rows, sem):
    pltpu.make_async_copy(
        src.at[pl.ds(0, rows * ROW_TILES)],
        dst.at[pl.ds(pl.multiple_of(dst_row * ROW_TILES, ROW_TILES), rows * ROW_TILES)], sem).wait()


def _moe_dispatch_kernel(dest_ref, cnt_ref, region_ref, m_ref, xs_ref, zbuf_ref, sem, zsem, *, n_tok):
    t = DISPATCH_TILE
    r = EXPERT_ROWS
    i = pl.program_id(0)

    @pl.when(i == 0)
    def _():
        zbuf_ref[...] = jnp.zeros_like(zbuf_ref)

        def zero_copy(e):
            last = pl.multiple_of((region_ref[e + 1] - r) * ROW_TILES, r * ROW_TILES)
            return pltpu.make_async_copy(zbuf_ref, xs_ref.at[pl.ds(last, r * ROW_TILES)], zsem)

        def start(e, carry):
            @pl.when(cnt_ref[e] > 0)
            def _():
                zero_copy(e).start()
            return carry

        def wait(e, carry):
            @pl.when(cnt_ref[e] > 0)
            def _():
                zero_copy(e).wait()
            return carry

        def tail_copy(blk):
            return pltpu.make_async_copy(
                zbuf_ref, xs_ref.at[pl.ds(pl.multiple_of(blk * (r * ROW_TILES), r * ROW_TILES), r * ROW_TILES)], zsem)

        def tail_start(blk, carry):
            tail_copy(blk).start()
            return carry

        def tail_wait(blk, carry):
            tail_copy(blk).wait()
            return carry

        first_unused = region_ref[N_EXPERTS] // r
        n_blocks = xs_ref.shape[0] // (r * ROW_TILES)
        lax.fori_loop(0, N_EXPERTS, start, 0)
        lax.fori_loop(first_unused, n_blocks, tail_start, 0)
        lax.fori_loop(0, N_EXPERTS, wait, 0)
        lax.fori_loop(first_unused, n_blocks, tail_wait, 0)

    per_iter = DMA_UNROLL // TOP_K

    def issue(g, carry):
        for u in range(per_iter):
            row = g * per_iter + u
            for k in range(TOP_K):
                _row_copy(m_ref, row, xs_ref, dest_ref[k * n_tok + i * t + row], sem).start(priority=k % 2)
        return carry

    lax.fori_loop(0, t // per_iter, issue, 0)
    for k in range(TOP_K):
        pltpu.make_async_copy(m_ref, xs_ref.at[pl.ds(0, t * ROW_TILES)], sem).wait()


def _moe_dispatch(dest, cnt, region, m_rt, rows):
    n = m_rt.shape[0] // ROW_TILES
    t = DISPATCH_TILE
    return pl.pallas_call(
        functools.partial(_moe_dispatch_kernel, n_tok=n),
        grid_spec=pltpu.PrefetchScalarGridSpec(
            num_scalar_prefetch=3,
            grid=(n // t,),
            in_specs=[pl.BlockSpec((t * ROW_TILES, LANES), lambda i, d, c, rg: (i, 0))],
            out_specs=pl.BlockSpec(memory_space=pl.ANY),
            scratch_shapes=[pltpu.VMEM((EXPERT_ROWS * ROW_TILES, LANES), F32),
                            pltpu.SemaphoreType.DMA(()), pltpu.SemaphoreType.DMA(())],
        ),
        out_shape=jax.ShapeDtypeStruct((rows * ROW_TILES, LANES), F32),
        compiler_params=pltpu.CompilerParams(dimension_semantics=("arbitrary",)),
        name="moe_dispatch",
    )(dest, cnt, region, m_rt)


def _moe_experts_kernel(blk_e_ref, nact_ref, x_ref, wg_ref, wu_ref, wd_ref, y_ref, wgu_ref, wdn_ref):
    r = EXPERT_ROWS
    b = pl.program_id(0)
    live = b < nact_ref[0]

    @pl.when(live)
    def _():
        @pl.when(jnp.logical_or(b == 0, blk_e_ref[b] != blk_e_ref[jnp.maximum(b - 1, 0)]))
        def _():
            wgu_ref[:, :D_EXPERT] = wg_ref[0, 0].astype(BF16)
            wgu_ref[:, D_EXPERT:] = wu_ref[0, 0].astype(BF16)
            wdn_ref[...] = wd_ref[0, 0].astype(BF16)

        x = _load_row_tiled(x_ref, r).astype(BF16)
        gu = _dot(x, wgu_ref[...])
        hb = jax.nn.silu(gu[:, :D_EXPERT]) * gu[:, D_EXPERT:]
        _store_row_tiled(y_ref, _dot(hb.astype(BF16), wdn_ref[...]))

    @pl.when(jnp.logical_not(live))
    def _():
        y_ref[...] = jnp.zeros_like(y_ref)


def _moe_experts(blk_e, nact, xs_rt, layer, w_gate, w_up, w_down):
    rows = xs_rt.shape[0] // ROW_TILES
    r = EXPERT_ROWS
    wspec = lambda k, n: pl.BlockSpec((1, 1, k, n), lambda b, be, na: (layer, be[b], 0, 0))
    return pl.pallas_call(
        _moe_experts_kernel,
        grid_spec=pltpu.PrefetchScalarGridSpec(
            num_scalar_prefetch=2,
            grid=(rows // r,),
            in_specs=[pl.BlockSpec((r * ROW_TILES, LANES), lambda b, be, na: (jnp.minimum(b, na[0] - 1), 0)),
                      wspec(D_MODEL, D_EXPERT), wspec(D_MODEL, D_EXPERT), wspec(D_EXPERT, D_MODEL)],
            out_specs=pl.BlockSpec((r * ROW_TILES, LANES), lambda b, be, na: (b, 0)),
            scratch_shapes=[pltpu.VMEM((D_MODEL, 2 * D_EXPERT), BF16),
                            pltpu.VMEM((D_EXPERT, D_MODEL), BF16)],
        ),
        out_shape=jax.ShapeDtypeStruct((rows * ROW_TILES, LANES), F32),
        compiler_params=pltpu.CompilerParams(dimension_semantics=("arbitrary",), vmem_limit_bytes=VMEM_LIMIT),
        name="moe_experts",
    )(blk_e, nact, xs_rt, w_gate, w_up, w_down)


def _combine_ple_kernel(dest_ref, h1_ref, wt_ref, p_ref, y_ref, pleg_ref, gatew_ref, projw_ref, postg_ref,
                        out_ref, ybuf_ref, sem, *, n_tok):
    t = COMBINE_TILE
    i = pl.program_id(0)

    def gather(step, slot):
        def issue(g, carry):
            for u in range(DMA_UNROLL // TOP_K):
                r = g * (DMA_UNROLL // TOP_K) + u
                for k in range(TOP_K):
                    _row_copy(y_ref, dest_ref[k * n_tok + step * t + r], ybuf_ref, (slot * TOP_K + k) * t + r,
                              sem.at[slot]).start(priority=k % 2)
            return carry
        lax.fori_loop(0, t // (DMA_UNROLL // TOP_K), issue, 0)

    @pl.when(i == 0)
    def _():
        gather(0, 0)

    @pl.when(i + 1 < pl.num_programs(0))
    def _():
        gather(i + 1, (i + 1) % 2)

    e = _rms(_dot(p_ref[...].astype(BF16), projw_ref[...]), postg_ref[...])
    slot = i % 2
    _rows_wait(y_ref, ybuf_ref, slot * TOP_K * t, TOP_K * t, sem.at[slot])
    wt = wt_ref[...]
    y0 = _load_row_tiled(ybuf_ref, t, (slot * TOP_K) * (t * ROW_TILES))
    y1 = _load_row_tiled(ybuf_ref, t, (slot * TOP_K + 1) * (t * ROW_TILES))
    h2 = h1_ref[...] + (wt[:, 0:1] * y0 + wt[:, 1:2] * y1)
    gate = jax.nn.sigmoid(_dot(_rms(h2, pleg_ref[...]).astype(BF16), gatew_ref[...]))
    out_ref[...] = h2 + gate * e


def _combine_ple(dest, h1, wt, p, y_rt, lw):
    n = h1.shape[0]
    t = COMBINE_TILE
    row = lambda w: pl.BlockSpec((t, w), lambda i, d: (i, 0))
    weights = [lw[k] for k in ("ple_g", "gate_w", "proj_w", "post_g")]
    return pl.pallas_call(
        functools.partial(_combine_ple_kernel, n_tok=n),
        grid_spec=pltpu.PrefetchScalarGridSpec(
            num_scalar_prefetch=1,
            grid=(n // t,),
            in_specs=[row(D_MODEL), row(TOP_K), row(p.shape[1]), pl.BlockSpec(memory_space=pl.ANY)]
            + [pl.BlockSpec(w.shape, lambda i, d, nd=w.ndim: (0,) * nd) for w in weights],
            out_specs=row(D_MODEL),
            scratch_shapes=[pltpu.VMEM((2 * TOP_K * t * ROW_TILES, LANES), F32), pltpu.SemaphoreType.DMA((2,))],
        ),
        out_shape=jax.ShapeDtypeStruct((n, D_MODEL), F32),
        compiler_params=pltpu.CompilerParams(dimension_semantics=("arbitrary",), vmem_limit_bytes=VMEM_LIMIT),
        name="combine_ple",
    )(dest, h1, wt, p, y_rt, *weights)


def _segment_matrix(seg_ids):
    seg_ids = jnp.asarray(seg_ids)
    same = (seg_ids[:, None] == seg_ids[None, :]).astype(F32)
    return (same / jnp.sum(same, axis=1, keepdims=True)).astype(BF16)


def _constants():
    lane = jnp.arange(256)
    qk_seg = (lane // HEAD_PAD) * 3 + jnp.where(lane % HEAD_PAD < MLA_NOPE, 0, jnp.where(lane % HEAD_PAD < MLA_QK, 1, 2))
    i = jnp.arange(RANK_CHUNK)
    e = jnp.arange(N_EXPERTS)
    return {
        "g64": _segment_matrix(lane // HEAD_DIM),
        "gqk": _segment_matrix(qk_seg),
        "upper": (i[:, None] < i[None, :]).astype(BF16),
        "ones": jnp.ones((RANK_CHUNK, LANES), BF16),
        "ltri": (e[None, :] < e[:, None]).astype(BF16),
    }


def _rope_layouts(cos, sin, batch, seq):
    n, half = cos.shape
    ones = jnp.ones((n, MLA_NOPE), F32)
    zeros_n = jnp.zeros((n, MLA_NOPE), F32)
    zpad = jnp.zeros((n, HEAD_PAD - MLA_QK), F32)
    zhalf = jnp.zeros((n, half), F32)
    return {
        "rc": jnp.concatenate([ones, cos, cos, zpad], axis=1),
        "rs1": jnp.concatenate([zeros_n, -sin, zhalf, zpad], axis=1),
        "rs2": jnp.concatenate([zeros_n, zhalf, sin, zpad], axis=1),
        "cos_t": jnp.transpose(cos.reshape(batch, seq, half), (0, 2, 1)),
        "sin_t": jnp.transpose(sin.reshape(batch, seq, half), (0, 2, 1)),
    }


def _layer_weights(i, consts, p):
    row = lambda v: v.reshape(1, -1).astype(F32)
    w_in = p["w_in"][i]
    o_sgu, o_conv, o_q, o_kv = 2 * SGU_WIDTH, 2 * SGU_WIDTH + 2 * CONV_WIDTH, 0, 0
    o_q = o_conv + Q_RANK
    o_kv = o_q + KV_RANK
    w_ckv = jnp.concatenate([w_in[:, o_q:], jnp.zeros((D_MODEL, 256 - KV_RANK - MLA_ROPE), F32)], axis=1)

    causal = jnp.tril(jnp.ones((CHUNK, CHUNK), F32))
    wcat = jnp.transpose(p["sgu_w"][i] * causal, (1, 0, 2)).reshape(CHUNK, SGU_HEADS * CHUNK)
    sbias = jnp.repeat(jnp.transpose(p["sgu_b"][i]), HEAD_DIM, axis=1)

    w_uq = p["w_uq"][i].reshape(Q_RANK, MLA_HEADS, MLA_QK)
    w_uq = jnp.pad(w_uq, ((0, 0), (0, 0), (0, HEAD_PAD - MLA_QK))).reshape(Q_RANK, MLA_HEADS * HEAD_PAD)
    qn_g = p["q_norm_g"][i]
    q_gain = jnp.tile(jnp.concatenate([qn_g, jnp.zeros((HEAD_PAD - MLA_QK,), F32)]), MLA_HEADS)
    w_ukv = p["w_ukv"][i].reshape(KV_RANK, MLA_HEADS, MLA_NOPE + MLA_V)
    w_ukt = jnp.transpose(w_ukv[:, :, :MLA_NOPE], (1, 2, 0)).reshape(MLA_HEADS * MLA_NOPE, KV_RANK)
    w_v = w_ukv[:, :, MLA_NOPE:].reshape(KV_RANK, MLA_HEADS // 2, 2, MLA_V)
    zeros_v = jnp.zeros((KV_RANK, MLA_HEADS // 2, MLA_V), F32)
    w_uv = jnp.stack([w_v[:, :, 0], zeros_v, zeros_v, w_v[:, :, 1]], axis=2).reshape(KV_RANK, MLA_HEADS * HEAD_PAD)
    v_ones = jnp.tile(jnp.repeat(jnp.array([0.0, 1.0, 1.0, 0.0], F32), MLA_V), MLA_HEADS // 2)
    kn_g = p["k_norm_g"][i]

    bg = p["branch_norm_g"][i]
    w_o = p["w_o"][i]
    wr = jnp.concatenate([p["router_group_w"][i], jnp.zeros((D_MODEL, 8 - N_GROUPS), F32),
                          p["router_expert_w"][i], jnp.zeros((D_MODEL, LANES - 8 - N_EXPERTS), F32)], axis=1)
    wr_hi = wr.astype(BF16)
    br = jnp.concatenate([p["router_group_b"][i], jnp.full((8 - N_GROUPS,), NEG_BIG, F32),
                          p["router_expert_b"][i], jnp.zeros((LANES - 8 - N_EXPERTS,), F32)])
    return {
        "mix_g": row(p["mix_norm_g"][i]),
        "w_sgu": w_in[:, :o_sgu].astype(BF16),
        "w_conv": w_in[:, o_sgu:o_conv].astype(BF16),
        "w_cq": w_in[:, o_conv:o_q].astype(BF16),
        "w_ckv": w_ckv.astype(BF16),
        "sgu_ln_g": row(p["sgu_ln_g"][i]), "sgu_ln_b": row(p["sgu_ln_b"][i]),
        "sgu_wcat": wcat.astype(BF16), "sgu_bias": sbias, "g64": consts["g64"], "bg_a": row(bg[:SGU_WIDTH]),
        "conv_w": jnp.pad(p["conv_w"][i], ((0, 1), (0, 0))), "conv_b": row(p["conv_b"][i]),
        "conv_ln_g": row(p["conv_ln_g"][i]), "conv_ln_b": row(p["conv_ln_b"][i]),
        "pw_w": p["conv_pw_w"][i].astype(BF16), "pw_b": row(p["conv_pw_b"][i]),
        "bg_b": row(bg[SGU_WIDTH:SGU_WIDTH + CONV_WIDTH]),
        "qa_g": row(p["q_a_norm_g"][i]), "w_uq": w_uq.astype(BF16), "gqk": consts["gqk"],
        "q_gain": row(q_gain),
        "kva_g": row(p["kv_a_norm_g"][i]), "w_ukt": w_ukt.astype(BF16), "w_uv": w_uv.astype(BF16),
        "v_ones": row(v_ones),
        "kn_g": jnp.broadcast_to(kn_g[:MLA_NOPE, None], (MLA_NOPE, MIX_TILE)),
        "kpe_g": jnp.broadcast_to(kn_g[MLA_NOPE:, None], (MLA_ROPE, MIX_TILE)),
        "bg_c": row(bg[SGU_WIDTH + CONV_WIDTH:]),
        "w_o_a": w_o[:SGU_WIDTH].astype(BF16),
        "w_o_b": w_o[SGU_WIDTH:SGU_WIDTH + CONV_WIDTH].astype(BF16),
        "w_o_c": w_o[SGU_WIDTH + CONV_WIDTH:].astype(BF16),
        "ffn_g": row(p["ffn_norm_g"][i]),
        "wr_hi": wr_hi, "wr_lo": (wr - wr_hi.astype(F32)).astype(BF16), "br": row(br),
        "layer": i, "w_gate": p["moe_w_gate"], "w_up": p["moe_w_up"], "w_down": p["moe_w_down"],
        "ple_g": row(p["ple_norm_g"][i]), "gate_w": p["ple_gate_w"][i].astype(BF16),
        "proj_w": p["ple_proj_w"][i].astype(BF16), "post_g": row(p["ple_post_norm_g"][i]),
    }


def _moe(h1, m, eidx, wts, pl_i, lw, consts):
    n = h1.shape[0]
    total = TOP_K * n
    rows = total + N_EXPERTS * EXPERT_ROWS
    dest2d, counts = _moe_rank(eidx[:TOP_K].reshape(1, total), consts)
    dest = dest2d.reshape(total)
    cnt = counts[:, 0].astype(jnp.int32)
    padded = (cnt + EXPERT_ROWS - 1) // EXPERT_ROWS * EXPERT_ROWS
    pend = jnp.cumsum(padded)
    nblk = rows // EXPERT_ROWS
    blk_start = jnp.arange(nblk, dtype=jnp.int32) * EXPERT_ROWS
    blk_e = jnp.minimum(jnp.sum((pend[None, :] <= blk_start[:, None]).astype(jnp.int32), axis=1), N_EXPERTS - 1)
    nact = (pend[-1:] // EXPERT_ROWS).astype(jnp.int32)
    region = jnp.concatenate([jnp.zeros((1,), jnp.int32), pend.astype(jnp.int32)])
    xs = _moe_dispatch(dest, cnt, region, m, rows)
    y = _moe_experts(blk_e, nact, xs, lw["layer"], lw["w_gate"], lw["w_up"], lw["w_down"])
    return _combine_ple(dest, h1, jnp.transpose(wts[:TOP_K]), pl_i, y, lw)


def kernel(x, p, positions, mix_norm_g, w_in, sgu_ln_g, sgu_ln_b, sgu_w, sgu_b, conv_w, conv_b, conv_ln_g, conv_ln_b, conv_pw_w, conv_pw_b, q_a_norm_g, w_uq, kv_a_norm_g, w_ukv, q_norm_g, k_norm_g, branch_norm_g, w_o, ffn_norm_g, router_group_w, router_group_b, router_expert_w, router_expert_b, moe_w_gate, moe_w_up, moe_w_down, ple_norm_g, ple_gate_w, ple_proj_w, ple_post_norm_g):
    params = dict(
        mix_norm_g=mix_norm_g, w_in=w_in, sgu_ln_g=sgu_ln_g, sgu_ln_b=sgu_ln_b, sgu_w=sgu_w, sgu_b=sgu_b,
        conv_w=conv_w, conv_b=conv_b, conv_ln_g=conv_ln_g, conv_ln_b=conv_ln_b, conv_pw_w=conv_pw_w,
        conv_pw_b=conv_pw_b, q_a_norm_g=q_a_norm_g, w_uq=w_uq, kv_a_norm_g=kv_a_norm_g, w_ukv=w_ukv,
        q_norm_g=q_norm_g, k_norm_g=k_norm_g, branch_norm_g=branch_norm_g, w_o=w_o, ffn_norm_g=ffn_norm_g,
        router_group_w=router_group_w, router_group_b=router_group_b, router_expert_w=router_expert_w,
        router_expert_b=router_expert_b, moe_w_gate=moe_w_gate, moe_w_up=moe_w_up, moe_w_down=moe_w_down,
        ple_norm_g=ple_norm_g, ple_gate_w=ple_gate_w, ple_proj_w=ple_proj_w, ple_post_norm_g=ple_post_norm_g)
    batch, seq, d = x.shape
    n = batch * seq
    depth = w_in.shape[0]
    consts = _constants()
    cos, sin = _rope_tables(positions)
    rope = _rope_layouts(cos, sin, batch, seq)
    h = x.reshape(n, d)
    for i in range(depth):
        lw = _layer_weights(i, consts, params)
        ya, yb, q, kt, v = _mixer_pre(h, rope, lw, batch, seq)
        yc = _attention(q, kt, v, batch, seq)
        h1, m, eidx, wts = _outproj_router(h, ya, yb, yc, lw)
        h = _moe(h1, m, eidx, wts, p[i].reshape(n, -1), lw, consts)
    return h.reshape(batch, seq, d)
```

```python
import functools
import math

import jax
import jax.numpy as jnp
from jax import lax
from jax.experimental import pallas as pl
from jax.experimental.pallas import tpu as pltpu

F32 = jnp.float32
BF16 = jnp.bfloat16

D_MODEL = 1024
HEAD_DIM = 64
SGU_HEADS = 4
SGU_WIDTH = 256
CHUNK = 128
CONV_WIDTH = 256
CONV_KERNEL = 31
MLA_HEADS = 8
MLA_NOPE = 64
MLA_ROPE = 32
MLA_QK = MLA_NOPE + MLA_ROPE
MLA_V = 64
Q_RANK = 256
KV_RANK = 128
ROPE_THETA = 10000.0
N_GROUPS = 4
EXPERTS_PER_GROUP = 8
N_EXPERTS = 32
D_EXPERT = 256
TOP_K = 2
EPS = 1e-6

LANES = 128
SUBLANES = 8
HEAD_PAD = 128
MIX_TILE = 512
CONV_HALO = 32
CONV_SHIFTED_ROWS = MIX_TILE + CONV_HALO - SUBLANES
MIXER_ORDER = ("q", "kv", "sgu", "conv")
ATT_TQ = 256
ATT_LOOKAHEAD = 1
RANK_CHUNK = 512
RANK_STEP = 2048
EXPERT_ROWS = 256
DISPATCH_TILE = 256
COMBINE_TILE = 256
ROW_TILES = D_MODEL // LANES
DMA_UNROLL = 8
VMEM_LIMIT = 48 * 1024 * 1024
NEG_BIG = -1e30


def _dot(a, b):
    return jnp.dot(a, b, preferred_element_type=F32)


def _rms(x, g):
    ms = jnp.mean(x * x, axis=-1, keepdims=True)
    return x * lax.rsqrt(ms + EPS) * g


def _seg_mean(x, g_ref, split=True):
    g = g_ref[...]
    outs = []
    for c in range(x.shape[1] // 256):
        xb = x[:, c * 256:(c + 1) * 256]
        hi = xb.astype(BF16)
        acc = _dot(hi, g)
        if split:
            acc = acc + _dot((xb - hi.astype(F32)).astype(BF16), g)
        outs.append(acc)
    return outs[0] if len(outs) == 1 else jnp.concatenate(outs, axis=-1)


def _group_layernorm(x, g_ref, gain, bias):
    d = x - _seg_mean(x, g_ref)
    var = _seg_mean(d * d, g_ref)
    return d * lax.rsqrt(var + EPS) * gain + bias


def _rope_table_kernel(pos_ref, invf_ref, cos_ref, sin_ref):
    ang = pos_ref[...].astype(F32) * invf_ref[...]
    cos_ref[...] = jnp.cos(ang)
    sin_ref[...] = jnp.sin(ang)


def _rope_tables(positions):
    b, s = positions.shape
    n = b * s
    half = MLA_ROPE // 2
    inv_freq = ROPE_THETA ** (-jnp.arange(0, MLA_ROPE, 2, dtype=F32) / MLA_ROPE)
    pos_rep = jnp.broadcast_to(positions.reshape(n, 1), (n, half)).reshape(n * half // LANES, LANES)
    invf_rep = jnp.tile(inv_freq, LANES // half).reshape(1, LANES)
    shape = jax.ShapeDtypeStruct(pos_rep.shape, F32)
    cos, sin = pl.pallas_call(_rope_table_kernel, out_shape=(shape, shape), name="rope_tables")(pos_rep, invf_rep)
    return cos.reshape(n, half), sin.reshape(n, half)


def _mixer_pre_kernel(
        h_ref, rc_ref, rs1_ref, rs2_ref, cost_ref, sint_ref,
        mixg_ref, wsgu_ref, wconv_ref, wcq_ref, wckv_ref,
        slng_ref, slnb_ref, wcat_ref, sbias_ref, g64_ref, bga_ref,
        cw_ref, cb_ref, clng_ref, clnb_ref, pww_ref, pwb_ref, bgb_ref,
        qag_ref, wuq_ref, gqk_ref, qgain_ref,
        kvag_ref, wukt_ref, wuv_ref, vones_ref, kng_ref, kpeg_ref,
        ya_ref, yb_ref, q_ref, kt_ref, v_ref,
        ybuf_ref, ysh_ref, *, tiles_per_seq):
    t = MIX_TILE
    a = _rms(h_ref[...], mixg_ref[...]).astype(BF16)
    zs = _dot(a, wsgu_ref[...])
    zc = _dot(a, wconv_ref[...])
    zq = _dot(a, wcq_ref[...])
    zkv = _dot(a, wckv_ref[...])

    def sgu_branch():
        zg = jax.nn.gelu(zs)
        u = zg[:, :SGU_WIDTH]
        vn = _group_layernorm(zg[:, SGU_WIDTH:], g64_ref, slng_ref[...], slnb_ref[...])
        lane_head = lax.broadcasted_iota(jnp.int32, (CHUNK, SGU_WIDTH), 1) // HEAD_DIM
        wcat = wcat_ref[...]
        sbias = sbias_ref[...]
        parts = []
        for c in range(t // CHUNK):
            vc = vn[c * CHUNK:(c + 1) * CHUNK]
            stacked = jnp.concatenate(
                [jnp.where(lane_head == hh, vc, 0.0) for hh in range(SGU_HEADS)], axis=0).astype(BF16)
            s = _dot(wcat, stacked) + sbias
            parts.append(u[c * CHUNK:(c + 1) * CHUNK] * s)
        ya_ref[...] = _rms(jnp.concatenate(parts, axis=0), bga_ref[...])

    def conv_branch():
        yg = zc[:, :CONV_WIDTH] * jax.nn.sigmoid(zc[:, CONV_WIDTH:])
        first = (pl.program_id(0) % tiles_per_seq) == 0

        @pl.when(first)
        def _():
            ybuf_ref[0:CONV_HALO, :] = jnp.zeros((CONV_HALO, CONV_WIDTH), F32)

        @pl.when(jnp.logical_not(first))
        def _():
            ybuf_ref[0:CONV_HALO, :] = ybuf_ref[t:t + CONV_HALO, :]

        ybuf_ref[CONV_HALO:CONV_HALO + t, :] = yg
        for sh in range(1, SUBLANES):
            ysh_ref[sh - 1] = ybuf_ref[pl.ds(sh, CONV_SHIFTED_ROWS), :]
        rows = 64
        first_tap_row = CONV_HALO - (CONV_KERNEL - 1)
        conv_parts = []
        for r in range(t // rows):
            acc = jnp.broadcast_to(cb_ref[...], (rows, CONV_WIDTH))
            for tap in range(CONV_KERNEL):
                sh = (first_tap_row + tap) % SUBLANES
                start = r * rows + first_tap_row + tap - sh
                src = ybuf_ref[pl.ds(start, rows), :] if sh == 0 else ysh_ref[sh - 1, pl.ds(start, rows), :]
                acc = acc + cw_ref[tap:tap + 1, :] * src
            conv_parts.append(acc)
        cv = jnp.concatenate(conv_parts, axis=0)
        cn = _group_layernorm(cv, g64_ref, clng_ref[...], clnb_ref[...])
        yb = _dot(jax.nn.silu(cn).astype(BF16), pww_ref[...]) + pwb_ref[...]
        yb_ref[...] = _rms(yb, bgb_ref[...])

    def query_branch():
        cqn = _rms(zq, qag_ref[...]).astype(BF16)
        qf = _dot(cqn, wuq_ref[...])
        qn = qf * lax.rsqrt(_seg_mean(qf * qf, gqk_ref, split=False) + EPS) * qgain_ref[...]
        rc, rs1, rs2 = rc_ref[...], rs1_ref[...], rs2_ref[...]
        for hh in range(MLA_HEADS):
            blk = qn[:, hh * HEAD_PAD:(hh + 1) * HEAD_PAD]
            half = MLA_ROPE // 2
            rot = blk * rc + pltpu.roll(blk, HEAD_PAD - half, 1) * rs1 + pltpu.roll(blk, half, 1) * rs2
            q_ref[:, hh * HEAD_PAD:(hh + 1) * HEAD_PAD] = rot.astype(BF16)

    def key_value_branch():
        ckvn = _rms(zkv[:, :KV_RANK], kvag_ref[...]).astype(BF16)
        v_ref[...] = (_dot(ckvn, wuv_ref[...]) + vones_ref[...]).astype(BF16)
        knt = lax.dot_general(wukt_ref[...], ckvn, (((1,), (1,)), ((), ())), preferred_element_type=F32)
        x = zkv[:, KV_RANK:].T[0:MLA_ROPE]
        xn = x * lax.rsqrt(jnp.mean(x * x, axis=0, keepdims=True) + EPS) * kpeg_ref[...]
        x1, x2 = xn[:MLA_ROPE // 2], xn[MLA_ROPE // 2:]
        cos, sin = cost_ref[0], sint_ref[0]
        kpe = jnp.concatenate([x1 * cos - x2 * sin, x2 * cos + x1 * sin], axis=0)
        pad = jnp.zeros((HEAD_PAD - MLA_QK, t), F32)
        kng = kng_ref[...]
        for hh in range(MLA_HEADS):
            blk = knt[hh * MLA_NOPE:(hh + 1) * MLA_NOPE]
            kn = blk * lax.rsqrt(jnp.mean(blk * blk, axis=0, keepdims=True) + EPS) * kng
            kt_ref[0, hh * HEAD_PAD:(hh + 1) * HEAD_PAD, :] = jnp.concatenate([kn, kpe, pad], axis=0).astype(BF16)

    branches = {"sgu": sgu_branch, "conv": conv_branch, "q": query_branch, "kv": key_value_branch}
    for name in MIXER_ORDER:
        branches[name]()


def _full(shape):
    nd = len(shape)
    return pl.BlockSpec(shape, lambda *_: (0,) * nd)


def _mixer_pre(h, rope, lw, batch, seq):
    n = h.shape[0]
    t = MIX_TILE
    tps = seq // t
    row = lambda w: pl.BlockSpec((t, w), lambda i: (i, 0))
    tspec = pl.BlockSpec((1, MLA_ROPE // 2, t), lambda i: (i // tps, 0, i % tps))
    weights = [lw[k] for k in (
        "mix_g", "w_sgu", "w_conv", "w_cq", "w_ckv",
        "sgu_ln_g", "sgu_ln_b", "sgu_wcat", "sgu_bias", "g64", "bg_a",
        "conv_w", "conv_b", "conv_ln_g", "conv_ln_b", "pw_w", "pw_b", "bg_b",
        "qa_g", "w_uq", "gqk", "q_gain",
        "kva_g", "w_ukt", "w_uv", "v_ones", "kn_g", "kpe_g")]
    in_specs = [row(D_MODEL), row(LANES), row(LANES), row(LANES), tspec, tspec] + [_full(w.shape) for w in weights]
    out_shape = (
        jax.ShapeDtypeStruct((n, SGU_WIDTH), F32),
        jax.ShapeDtypeStruct((n, CONV_WIDTH), F32),
        jax.ShapeDtypeStruct((n, MLA_HEADS * HEAD_PAD), BF16),
        jax.ShapeDtypeStruct((batch, MLA_HEADS * HEAD_PAD, seq), BF16),
        jax.ShapeDtypeStruct((n, MLA_HEADS * HEAD_PAD), BF16),
    )
    out_specs = (
        row(SGU_WIDTH), row(CONV_WIDTH), row(MLA_HEADS * HEAD_PAD),
        pl.BlockSpec((1, MLA_HEADS * HEAD_PAD, t), lambda i: (i // tps, 0, i % tps)),
        row(MLA_HEADS * HEAD_PAD),
    )
    return pl.pallas_call(
        functools.partial(_mixer_pre_kernel, tiles_per_seq=tps),
        grid=(n // t,),
        in_specs=in_specs,
        out_specs=out_specs,
        out_shape=out_shape,
        scratch_shapes=[pltpu.VMEM((t + CONV_HALO, CONV_WIDTH), F32),
                        pltpu.VMEM((SUBLANES - 1, CONV_SHIFTED_ROWS, CONV_WIDTH), F32)],
        compiler_params=pltpu.CompilerParams(dimension_semantics=("arbitrary",), vmem_limit_bytes=VMEM_LIMIT),
        name="mixer_pre",
    )(h, rope["rc"], rope["rs1"], rope["rs2"], rope["cos_t"], rope["sin_t"], *weights)


def _attention_kernel(q_ref, kt_ref, v_ref, o_ref, *, seq):
    tq = ATT_TQ
    exp2_scale = MLA_QK ** -0.5 * math.log2(math.e)
    row = lax.broadcasted_iota(jnp.int32, (tq, tq), 0)
    col = lax.broadcasted_iota(jnp.int32, (tq, tq), 1)
    lane = lax.broadcasted_iota(jnp.int32, (tq, HEAD_PAD), 1)
    def scores(qi, hh):
        nk = (qi + 1) * tq
        q = q_ref[qi * tq:(qi + 1) * tq, hh * HEAD_PAD:(hh + 1) * HEAD_PAD]
        s = _dot(q, kt_ref[0, hh * HEAD_PAD:(hh + 1) * HEAD_PAD, 0:nk])
        diag = jnp.where(col <= row, s[:, nk - tq:], NEG_BIG)
        return diag if qi == 0 else jnp.concatenate([s[:, :nk - tq], diag], axis=1)

    items = [(qi, hh) for qi in range(seq // tq) for hh in range(2)]
    ahead = [scores(*it) for it in items[:ATT_LOOKAHEAD]]
    outs = []
    for idx, (qi, hh) in enumerate(items):
        s = ahead.pop(0)
        if idx + ATT_LOOKAHEAD < len(items):
            ahead.append(scores(*items[idx + ATT_LOOKAHEAD]))
        p = jnp.exp2(((s - jnp.max(s, axis=-1, keepdims=True)) * exp2_scale).astype(BF16))
        acc = _dot(p, v_ref[0:(qi + 1) * tq, :])[:, hh * HEAD_PAD:(hh + 1) * HEAD_PAD]
        outs.append(acc / pltpu.roll(acc, MLA_V, 1))
        if hh == 1:
            o_ref[qi * tq:(qi + 1) * tq, :] = jnp.where(lane < MLA_V, outs[0], outs[1])
            outs = []


def _attention(q, kt, v, batch, seq):
    n = q.shape[0]
    pairs = MLA_HEADS // 2
    return pl.pallas_call(
        functools.partial(_attention_kernel, seq=seq),
        grid=(batch, pairs),
        in_specs=[
            pl.BlockSpec((seq, 2 * HEAD_PAD), lambda b, p: (b, p)),
            pl.BlockSpec((1, 2 * HEAD_PAD, seq), lambda b, p: (b, p, 0)),
            pl.BlockSpec((seq, 2 * HEAD_PAD), lambda b, p: (b, p)),
        ],
        out_specs=pl.BlockSpec((seq, 2 * MLA_V), lambda b, p: (b, p)),
        out_shape=jax.ShapeDtypeStruct((n, MLA_HEADS * MLA_V), F32),
        compiler_params=pltpu.CompilerParams(
            dimension_semantics=("arbitrary", "arbitrary"), vmem_limit_bytes=VMEM_LIMIT),
        name="attention",
    )(q, kt, v)


def _outproj_router_kernel(h_ref, ya_ref, yb_ref, yc_ref, bgc_ref, woa_ref, wob_ref, woc_ref,
                           ffng_ref, wr_ref, br_ref,
                           h1_ref, m_ref, eidx_ref, wts_ref):
    t = MIX_TILE
    ycn = _rms(yc_ref[...], bgc_ref[...])
    proj = (_dot(ya_ref[...].astype(BF16), woa_ref[...]) + _dot(yb_ref[...].astype(BF16), wob_ref[...])
            + _dot(ycn.astype(BF16), woc_ref[...]))
    h1 = h_ref[...] + proj
    h1_ref[...] = h1
    m = _rms(h1, ffng_ref[...])
    _store_row_tiled(m_ref, m)

    mh = m.astype(BF16)
    ml = (m - mh.astype(F32)).astype(BF16)
    wr = wr_ref[...]
    both = _dot(mh, wr)
    logits = both[:, :LANES] + both[:, LANES:] + _dot(ml, wr[:, :LANES]) + br_ref[...]
    lt = logits.T
    rowi = lax.broadcasted_iota(jnp.int32, (EXPERTS_PER_GROUP, t), 0)
    g8 = lt[0:8]
    gmax = jnp.max(g8, axis=0, keepdims=True)
    gsum = jnp.sum(jnp.exp(g8 - gmax), axis=0, keepdims=True)
    gidx = jnp.min(jnp.where(g8 == gmax, rowi, 8), axis=0, keepdims=True)
    g_w = 1.0 / gsum
    esel = jnp.zeros((EXPERTS_PER_GROUP, t), F32)
    for g in range(N_GROUPS):
        esel = jnp.where(gidx == g, lt[8 + g * EXPERTS_PER_GROUP:8 + (g + 1) * EXPERTS_PER_GROUP], esel)
    ep = jnp.exp(esel - jnp.max(esel, axis=0, keepdims=True))
    eprob = ep / jnp.sum(ep, axis=0, keepdims=True)
    v1 = jnp.max(eprob, axis=0, keepdims=True)
    i1 = jnp.min(jnp.where(eprob == v1, rowi, 8), axis=0, keepdims=True)
    rest = jnp.where(rowi == i1, -1.0, eprob)
    v2 = jnp.max(rest, axis=0, keepdims=True)
    i2 = jnp.min(jnp.where(rest == v2, rowi, 8), axis=0, keepdims=True)
    den = v1 + v2
    e1 = gidx * EXPERTS_PER_GROUP + i1
    e2 = gidx * EXPERTS_PER_GROUP + i2
    eidx_ref[...] = jnp.where(rowi == 0, e1, jnp.where(rowi == 1, e2, 0))
    wts_ref[...] = jnp.where(rowi == 0, g_w * (v1 / den), jnp.where(rowi == 1, g_w * (v2 / den), 0.0))


def _outproj_router(h, ya, yb, yc, lw):
    n = h.shape[0]
    t = MIX_TILE
    row = lambda w: pl.BlockSpec((t, w), lambda i: (i, 0))
    weights = [lw[k] for k in ("bg_c", "w_o_a", "w_o_b", "w_o_c", "ffn_g", "wr_split", "br")]
    colspec = pl.BlockSpec((8, t), lambda i: (0, i))
    return pl.pallas_call(
        _outproj_router_kernel,
        grid=(n // t,),
        in_specs=[row(D_MODEL), row(SGU_WIDTH), row(CONV_WIDTH), row(MLA_HEADS * MLA_V)]
        + [_full(w.shape) for w in weights],
        out_specs=(row(D_MODEL), pl.BlockSpec((t * ROW_TILES, LANES), lambda i: (i, 0)), colspec, colspec),
        out_shape=(
            jax.ShapeDtypeStruct((n, D_MODEL), F32),
            jax.ShapeDtypeStruct((n * ROW_TILES, LANES), F32),
            jax.ShapeDtypeStruct((8, n), jnp.int32),
            jax.ShapeDtypeStruct((8, n), F32),
        ),
        compiler_params=pltpu.CompilerParams(dimension_semantics=("arbitrary",), vmem_limit_bytes=VMEM_LIMIT),
        name="outproj_router",
    )(h, ya, yb, yc, *weights)


def _moe_rank_kernel(e_ref, upper_ref, ones_ref, ltri_ref, dest_ref, cnt_ref, run_ref, base_ref):
    phase = pl.program_id(0)
    step = pl.program_id(1)
    c = RANK_CHUNK
    expert = lax.broadcasted_iota(jnp.int32, (N_EXPERTS, c), 0)

    @pl.when(jnp.logical_and(phase == 0, step == 0))
    def _():
        run_ref[...] = jnp.zeros_like(run_ref)
        base_ref[...] = jnp.zeros_like(base_ref)

    @pl.when(jnp.logical_and(phase == 1, step == 0))
    def _():
        blocks = jnp.floor((run_ref[...] + (EXPERT_ROWS - 1)) * (1.0 / EXPERT_ROWS))
        base_ref[...] = _dot(ltri_ref[...], blocks.astype(BF16)) * EXPERT_ROWS
        run_ref[...] = jnp.zeros_like(run_ref)

    for sub in range(RANK_STEP // c):
        cols = slice(sub * c, (sub + 1) * c)
        onehot = jnp.where(expert == e_ref[:, cols], 1.0, 0.0)
        oh16 = onehot.astype(BF16)
        before = _dot(oh16, upper_ref[...])
        pos = before + run_ref[:, 0:1] + base_ref[:, 0:1]
        dest_ref[:, cols] = jnp.sum(onehot * pos, axis=0, keepdims=True).astype(jnp.int32)
        run_ref[...] = run_ref[...] + _dot(oh16, ones_ref[...])
    cnt_ref[...] = run_ref[...]


def _moe_rank(e_flat, consts):
    total = e_flat.shape[1]
    c = RANK_CHUNK
    st = RANK_STEP
    return pl.pallas_call(
        _moe_rank_kernel,
        grid=(2, total // st),
        in_specs=[pl.BlockSpec((1, st), lambda p, s: (0, s)),
                  _full((c, c)), _full((c, LANES)), _full((N_EXPERTS, N_EXPERTS))],
        out_specs=(pl.BlockSpec((1, st), lambda p, s: (0, s * p)),
                   pl.BlockSpec((N_EXPERTS, LANES), lambda p, s: (0, 0))),
        out_shape=(jax.ShapeDtypeStruct((1, total), jnp.int32),
                   jax.ShapeDtypeStruct((N_EXPERTS, LANES), F32)),
        scratch_shapes=[pltpu.VMEM((N_EXPERTS, LANES), F32), pltpu.VMEM((N_EXPERTS, LANES), F32)],
        compiler_params=pltpu.CompilerParams(dimension_semantics=("arbitrary", "arbitrary")),
        name="moe_rank",
    )(e_flat, consts["upper"], consts["ones"], consts["ltri"])


def _store_row_tiled(ref, x, offset=0):
    rows = x.shape[0]
    for s in range(ROW_TILES):
        ref[pl.ds(offset + s, rows, stride=ROW_TILES), :] = x[:, s * LANES:(s + 1) * LANES]


def _load_row_tiled(ref, rows, offset=0):
    return jnp.concatenate(
        [ref[pl.ds(offset + s, rows, stride=ROW_TILES), :] for s in range(ROW_TILES)], axis=1)


def _row_copy(src, src_row, dst, dst_row, sem):
    return pltpu.make_async_copy(
        src.at[pl.ds(pl.multiple_of(src_row * ROW_TILES, ROW_TILES), ROW_TILES)],
        dst.at[pl.ds(pl.multiple_of(dst_row * ROW_TILES, ROW_TILES), ROW_TILES)], sem)


def _rows_wait(src, dst, dst_row, rows, sem):
    pltpu.make_async_copy(
        src.at[pl.ds(0, rows * ROW_TILES)],
        dst.at[pl.ds(pl.multiple_of(dst_row * ROW_TILES, ROW_TILES), rows * ROW_TILES)], sem).wait()


def _moe_dispatch_kernel(dest_ref, cnt_ref, region_ref, m_ref, xs_ref, zbuf_ref, sem, zsem, *, n_tok):
    t = DISPATCH_TILE
    r = EXPERT_ROWS
    i = pl.program_id(0)

    @pl.when(i == 0)
    def _():
        zbuf_ref[...] = jnp.zeros_like(zbuf_ref)

        def zero_copy(e):
            last = pl.multiple_of((region_ref[e + 1] - r) * ROW_TILES, r * ROW_TILES)
            return pltpu.make_async_copy(zbuf_ref, xs_ref.at[pl.ds(last, r * ROW_TILES)], zsem)

        def start(e, carry):
            @pl.when(cnt_ref[e] > 0)
            def _():
                zero_copy(e).start()
            return carry

        def wait(e, carry):
            @pl.when(cnt_ref[e] > 0)
            def _():
                zero_copy(e).wait()
            return carry

        def tail_copy(blk):
            return pltpu.make_async_copy(
                zbuf_ref, xs_ref.at[pl.ds(pl.multiple_of(blk * (r * ROW_TILES), r * ROW_TILES), r * ROW_TILES)], zsem)

        def tail_start(blk, carry):
            tail_copy(blk).start()
            return carry

        def tail_wait(blk, carry):
            tail_copy(blk).wait()
            return carry

        first_unused = region_ref[N_EXPERTS] // r
        n_blocks = xs_ref.shape[0] // (r * ROW_TILES)
        lax.fori_loop(0, N_EXPERTS, start, 0)
        lax.fori_loop(first_unused, n_blocks, tail_start, 0)
        lax.fori_loop(0, N_EXPERTS, wait, 0)
        lax.fori_loop(first_unused, n_blocks, tail_wait, 0)

    per_iter = DMA_UNROLL // TOP_K

    def issue(g, carry):
        for u in range(per_iter):
            row = g * per_iter + u
            for k in range(TOP_K):
                _row_copy(m_ref, row, xs_ref, dest_ref[k * n_tok + i * t + row], sem).start(priority=k % 2)
        return carry

    lax.fori_loop(0, t // per_iter, issue, 0)
    for k in range(TOP_K):
        pltpu.make_async_copy(m_ref, xs_ref.at[pl.ds(0, t * ROW_TILES)], sem).wait()


def _moe_dispatch(dest, cnt, region, m_rt, rows):
    n = m_rt.shape[0] // ROW_TILES
    t = DISPATCH_TILE
    return pl.pallas_call(
        functools.partial(_moe_dispatch_kernel, n_tok=n),
        grid_spec=pltpu.PrefetchScalarGridSpec(
            num_scalar_prefetch=3,
            grid=(n // t,),
            in_specs=[pl.BlockSpec((t * ROW_TILES, LANES), lambda i, d, c, rg: (i, 0))],
            out_specs=pl.BlockSpec(memory_space=pl.ANY),
            scratch_shapes=[pltpu.VMEM((EXPERT_ROWS * ROW_TILES, LANES), F32),
                            pltpu.SemaphoreType.DMA(()), pltpu.SemaphoreType.DMA(())],
        ),
        out_shape=jax.ShapeDtypeStruct((rows * ROW_TILES, LANES), F32),
        compiler_params=pltpu.CompilerParams(dimension_semantics=("arbitrary",)),
        name="moe_dispatch",
    )(dest, cnt, region, m_rt)


def _moe_experts_kernel(blk_e_ref, nact_ref, x_ref, wg_ref, wu_ref, wd_ref, y_ref, wgu_ref, wdn_ref):
    r = EXPERT_ROWS
    b = pl.program_id(0)
    live = b < nact_ref[0]

    @pl.when(live)
    def _():
        @pl.when(jnp.logical_or(b == 0, blk_e_ref[b] != blk_e_ref[jnp.maximum(b - 1, 0)]))
        def _():
            wgu_ref[:, :D_EXPERT] = wg_ref[0, 0].astype(BF16)
            wgu_ref[:, D_EXPERT:] = wu_ref[0, 0].astype(BF16)
            wdn_ref[...] = wd_ref[0, 0].astype(BF16)

        x = _load_row_tiled(x_ref, r).astype(BF16)
        gu = _dot(x, wgu_ref[...])
        hb = jax.nn.silu(gu[:, :D_EXPERT]) * gu[:, D_EXPERT:]
        _store_row_tiled(y_ref, _dot(hb.astype(BF16), wdn_ref[...]))

    @pl.when(jnp.logical_not(live))
    def _():
        y_ref[...] = jnp.zeros_like(y_ref)


def _moe_experts(blk_e, nact, xs_rt, layer, w_gate, w_up, w_down):
    rows = xs_rt.shape[0] // ROW_TILES
    r = EXPERT_ROWS
    wspec = lambda k, n: pl.BlockSpec((1, 1, k, n), lambda b, be, na: (layer, be[b], 0, 0))
    return pl.pallas_call(
        _moe_experts_kernel,
        grid_spec=pltpu.PrefetchScalarGridSpec(
            num_scalar_prefetch=2,
            grid=(rows // r,),
            in_specs=[pl.BlockSpec((r * ROW_TILES, LANES), lambda b, be, na: (jnp.minimum(b, na[0] - 1), 0)),
                      wspec(D_MODEL, D_EXPERT), wspec(D_MODEL, D_EXPERT), wspec(D_EXPERT, D_MODEL)],
            out_specs=pl.BlockSpec((r * ROW_TILES, LANES), lambda b, be, na: (b, 0)),
            scratch_shapes=[pltpu.VMEM((D_MODEL, 2 * D_EXPERT), BF16),
                            pltpu.VMEM((D_EXPERT, D_MODEL), BF16)],
        ),
        out_shape=jax.ShapeDtypeStruct((rows * ROW_TILES, LANES), F32),
        compiler_params=pltpu.CompilerParams(dimension_semantics=("arbitrary",), vmem_limit_bytes=VMEM_LIMIT),
        name="moe_experts",
    )(blk_e, nact, xs_rt, w_gate, w_up, w_down)


def _combine_ple_kernel(dest_ref, h1_ref, wt_ref, p_ref, y_ref, pleg_ref, gatew_ref, projw_ref, postg_ref,
                        out_ref, ybuf_ref, sem, *, n_tok):
    t = COMBINE_TILE
    i = pl.program_id(0)

    def gather(step, slot):
        def issue(g, carry):
            for u in range(DMA_UNROLL // TOP_K):
                r = g * (DMA_UNROLL // TOP_K) + u
                for k in range(TOP_K):
                    _row_copy(y_ref, dest_ref[k * n_tok + step * t + r], ybuf_ref, (slot * TOP_K + k) * t + r,
                              sem.at[slot]).start(priority=k % 2)
            return carry
        lax.fori_loop(0, t // (DMA_UNROLL // TOP_K), issue, 0)

    @pl.when(i == 0)
    def _():
        gather(0, 0)

    @pl.when(i + 1 < pl.num_programs(0))
    def _():
        gather(i + 1, (i + 1) % 2)

    e = _rms(_dot(p_ref[...].astype(BF16), projw_ref[...]), postg_ref[...])
    slot = i % 2
    _rows_wait(y_ref, ybuf_ref, slot * TOP_K * t, TOP_K * t, sem.at[slot])
    wt = wt_ref[...]
    y0 = _load_row_tiled(ybuf_ref, t, (slot * TOP_K) * (t * ROW_TILES))
    y1 = _load_row_tiled(ybuf_ref, t, (slot * TOP_K + 1) * (t * ROW_TILES))
    h2 = h1_ref[...] + (wt[:, 0:1] * y0 + wt[:, 1:2] * y1)
    gate = jax.nn.sigmoid(_dot(_rms(h2, pleg_ref[...]).astype(BF16), gatew_ref[...]))
    out_ref[...] = h2 + gate * e


def _combine_ple(dest, h1, wt, p, y_rt, lw):
    n = h1.shape[0]
    t = COMBINE_TILE
    row = lambda w: pl.BlockSpec((t, w), lambda i, d: (i, 0))
    weights = [lw[k] for k in ("ple_g", "gate_w", "proj_w", "post_g")]
    return pl.pallas_call(
        functools.partial(_combine_ple_kernel, n_tok=n),
        grid_spec=pltpu.PrefetchScalarGridSpec(
            num_scalar_prefetch=1,
            grid=(n // t,),
            in_specs=[row(D_MODEL), row(TOP_K), row(p.shape[1]), pl.BlockSpec(memory_space=pl.ANY)]
            + [pl.BlockSpec(w.shape, lambda i, d, nd=w.ndim: (0,) * nd) for w in weights],
            out_specs=row(D_MODEL),
            scratch_shapes=[pltpu.VMEM((2 * TOP_K * t * ROW_TILES, LANES), F32), pltpu.SemaphoreType.DMA((2,))],
        ),
        out_shape=jax.ShapeDtypeStruct((n, D_MODEL), F32),
        compiler_params=pltpu.CompilerParams(dimension_semantics=("arbitrary",), vmem_limit_bytes=VMEM_LIMIT),
        name="combine_ple",
    )(dest, h1, wt, p, y_rt, *weights)


def _segment_matrix(seg_ids):
    seg_ids = jnp.asarray(seg_ids)
    same = (seg_ids[:, None] == seg_ids[None, :]).astype(F32)
    return (same / jnp.sum(same, axis=1, keepdims=True)).astype(BF16)


def _constants():
    lane = jnp.arange(256)
    qk_seg = (lane // HEAD_PAD) * 3 + jnp.where(lane % HEAD_PAD < MLA_NOPE, 0, jnp.where(lane % HEAD_PAD < MLA_QK, 1, 2))
    i = jnp.arange(RANK_CHUNK)
    e = jnp.arange(N_EXPERTS)
    return {
        "g64": _segment_matrix(lane // HEAD_DIM),
        "gqk": _segment_matrix(qk_seg),
        "upper": (i[:, None] < i[None, :]).astype(BF16),
        "ones": jnp.ones((RANK_CHUNK, LANES), BF16),
        "ltri": (e[None, :] < e[:, None]).astype(BF16),
    }


def _rope_layouts(cos, sin, batch, seq):
    n, half = cos.shape
    ones = jnp.ones((n, MLA_NOPE), F32)
    zeros_n = jnp.zeros((n, MLA_NOPE), F32)
    zpad = jnp.zeros((n, HEAD_PAD - MLA_QK), F32)
    zhalf = jnp.zeros((n, half), F32)
    return {
        "rc": jnp.concatenate([ones, cos, cos, zpad], axis=1),
        "rs1": jnp.concatenate([zeros_n, -sin, zhalf, zpad], axis=1),
        "rs2": jnp.concatenate([zeros_n, zhalf, sin, zpad], axis=1),
        "cos_t": jnp.transpose(cos.reshape(batch, seq, half), (0, 2, 1)),
        "sin_t": jnp.transpose(sin.reshape(batch, seq, half), (0, 2, 1)),
    }


def _layer_weights(i, consts, p):
    row = lambda v: v.reshape(1, -1).astype(F32)
    w_in = p["w_in"][i]
    o_sgu, o_conv, o_q, o_kv = 2 * SGU_WIDTH, 2 * SGU_WIDTH + 2 * CONV_WIDTH, 0, 0
    o_q = o_conv + Q_RANK
    o_kv = o_q + KV_RANK
    w_ckv = jnp.concatenate([w_in[:, o_q:], jnp.zeros((D_MODEL, 256 - KV_RANK - MLA_ROPE), F32)], axis=1)

    causal = jnp.tril(jnp.ones((CHUNK, CHUNK), F32))
    wcat = jnp.transpose(p["sgu_w"][i] * causal, (1, 0, 2)).reshape(CHUNK, SGU_HEADS * CHUNK)
    sbias = jnp.broadcast_to(jnp.transpose(p["sgu_b"][i])[:, :, None],
                             (CHUNK, SGU_HEADS, HEAD_DIM)).reshape(CHUNK, SGU_WIDTH)

    w_uq = p["w_uq"][i].reshape(Q_RANK, MLA_HEADS, MLA_QK)
    w_uq = jnp.pad(w_uq, ((0, 0), (0, 0), (0, HEAD_PAD - MLA_QK))).reshape(Q_RANK, MLA_HEADS * HEAD_PAD)
    qn_g = p["q_norm_g"][i]
    q_gain = jnp.tile(jnp.concatenate([qn_g, jnp.zeros((HEAD_PAD - MLA_QK,), F32)]), MLA_HEADS)
    w_ukv = p["w_ukv"][i].reshape(KV_RANK, MLA_HEADS, MLA_NOPE + MLA_V)
    w_ukt = jnp.transpose(w_ukv[:, :, :MLA_NOPE], (1, 2, 0)).reshape(MLA_HEADS * MLA_NOPE, KV_RANK)
    w_v = w_ukv[:, :, MLA_NOPE:].reshape(KV_RANK, MLA_HEADS // 2, 2, MLA_V)
    zeros_v = jnp.zeros((KV_RANK, MLA_HEADS // 2, MLA_V), F32)
    w_uv = jnp.stack([w_v[:, :, 0], zeros_v, zeros_v, w_v[:, :, 1]], axis=2).reshape(KV_RANK, MLA_HEADS * HEAD_PAD)
    v_ones = jnp.tile(jnp.repeat(jnp.array([0.0, 1.0, 1.0, 0.0], F32), MLA_V), MLA_HEADS // 2)
    kn_g = p["k_norm_g"][i]

    bg = p["branch_norm_g"][i]
    w_o = p["w_o"][i]
    wr = jnp.concatenate([p["router_group_w"][i], jnp.zeros((D_MODEL, 8 - N_GROUPS), F32),
                          p["router_expert_w"][i], jnp.zeros((D_MODEL, LANES - 8 - N_EXPERTS), F32)], axis=1)
    wr_hi = wr.astype(BF16)
    br = jnp.concatenate([p["router_group_b"][i], jnp.full((8 - N_GROUPS,), NEG_BIG, F32),
                          p["router_expert_b"][i], jnp.zeros((LANES - 8 - N_EXPERTS,), F32)])
    return {
        "mix_g": row(p["mix_norm_g"][i]),
        "w_sgu": w_in[:, :o_sgu].astype(BF16),
        "w_conv": w_in[:, o_sgu:o_conv].astype(BF16),
        "w_cq": w_in[:, o_conv:o_q].astype(BF16),
        "w_ckv": w_ckv.astype(BF16),
        "sgu_ln_g": row(p["sgu_ln_g"][i]), "sgu_ln_b": row(p["sgu_ln_b"][i]),
        "sgu_wcat": wcat.astype(BF16), "sgu_bias": sbias, "g64": consts["g64"], "bg_a": row(bg[:SGU_WIDTH]),
        "conv_w": jnp.pad(p["conv_w"][i], ((0, 1), (0, 0))), "conv_b": row(p["conv_b"][i]),
        "conv_ln_g": row(p["conv_ln_g"][i]), "conv_ln_b": row(p["conv_ln_b"][i]),
        "pw_w": p["conv_pw_w"][i].astype(BF16), "pw_b": row(p["conv_pw_b"][i]),
        "bg_b": row(bg[SGU_WIDTH:SGU_WIDTH + CONV_WIDTH]),
        "qa_g": row(p["q_a_norm_g"][i]), "w_uq": w_uq.astype(BF16), "gqk": consts["gqk"],
        "q_gain": row(q_gain),
        "kva_g": row(p["kv_a_norm_g"][i]), "w_ukt": w_ukt.astype(BF16), "w_uv": w_uv.astype(BF16),
        "v_ones": row(v_ones),
        "kn_g": jnp.broadcast_to(kn_g[:MLA_NOPE, None], (MLA_NOPE, MIX_TILE)),
        "kpe_g": jnp.broadcast_to(kn_g[MLA_NOPE:, None], (MLA_ROPE, MIX_TILE)),
        "bg_c": row(bg[SGU_WIDTH + CONV_WIDTH:]),
        "w_o_a": w_o[:SGU_WIDTH].astype(BF16),
        "w_o_b": w_o[SGU_WIDTH:SGU_WIDTH + CONV_WIDTH].astype(BF16),
        "w_o_c": w_o[SGU_WIDTH + CONV_WIDTH:].astype(BF16),
        "ffn_g": row(p["ffn_norm_g"][i]),
        "wr_split": jnp.concatenate([wr_hi, (wr - wr_hi.astype(F32)).astype(BF16)], axis=1), "br": row(br),
        "layer": i, "w_gate": p["moe_w_gate"], "w_up": p["moe_w_up"], "w_down": p["moe_w_down"],
        "ple_g": row(p["ple_norm_g"][i]), "gate_w": p["ple_gate_w"][i].astype(BF16),
        "proj_w": p["ple_proj_w"][i].astype(BF16), "post_g": row(p["ple_post_norm_g"][i]),
    }


def _moe(h1, m, eidx, wts, pl_i, lw, consts):
    n = h1.shape[0]
    total = TOP_K * n
    rows = total + N_EXPERTS * EXPERT_ROWS
    dest2d, counts = _moe_rank(eidx[:TOP_K].reshape(1, total), consts)
    dest = dest2d.reshape(total)
    cnt = counts[:, 0].astype(jnp.int32)
    padded = (cnt + EXPERT_ROWS - 1) // EXPERT_ROWS * EXPERT_ROWS
    pend = jnp.cumsum(padded)
    nblk = rows // EXPERT_ROWS
    blk_start = jnp.arange(nblk, dtype=jnp.int32) * EXPERT_ROWS
    blk_e = jnp.minimum(jnp.sum((pend[None, :] <= blk_start[:, None]).astype(jnp.int32), axis=1), N_EXPERTS - 1)
    nact = (pend[-1:] // EXPERT_ROWS).astype(jnp.int32)
    region = jnp.concatenate([jnp.zeros((1,), jnp.int32), pend.astype(jnp.int32)])
    xs = _moe_dispatch(dest, cnt, region, m, rows)
    y = _moe_experts(blk_e, nact, xs, lw["layer"], lw["w_gate"], lw["w_up"], lw["w_down"])
    return _combine_ple(dest, h1, jnp.transpose(wts[:TOP_K]), pl_i, y, lw)


def kernel(x, p, positions, mix_norm_g, w_in, sgu_ln_g, sgu_ln_b, sgu_w, sgu_b, conv_w, conv_b, conv_ln_g, conv_ln_b, conv_pw_w, conv_pw_b, q_a_norm_g, w_uq, kv_a_norm_g, w_ukv, q_norm_g, k_norm_g, branch_norm_g, w_o, ffn_norm_g, router_group_w, router_group_b, router_expert_w, router_expert_b, moe_w_gate, moe_w_up, moe_w_down, ple_norm_g, ple_gate_w, ple_proj_w, ple_post_norm_g):
    params = dict(
        mix_norm_g=mix_norm_g, w_in=w_in, sgu_ln_g=sgu_ln_g, sgu_ln_b=sgu_ln_b, sgu_w=sgu_w, sgu_b=sgu_b,
        conv_w=conv_w, conv_b=conv_b, conv_ln_g=conv_ln_g, conv_ln_b=conv_ln_b, conv_pw_w=conv_pw_w,
        conv_pw_b=conv_pw_b, q_a_norm_g=q_a_norm_g, w_uq=w_uq, kv_a_norm_g=kv_a_norm_g, w_ukv=w_ukv,
        q_norm_g=q_norm_g, k_norm_g=k_norm_g, branch_norm_g=branch_norm_g, w_o=w_o, ffn_norm_g=ffn_norm_g,
        router_group_w=router_group_w, router_group_b=router_group_b, router_expert_w=router_expert_w,
        router_expert_b=router_expert_b, moe_w_gate=moe_w_gate, moe_w_up=moe_w_up, moe_w_down=moe_w_down,
        ple_norm_g=ple_norm_g, ple_gate_w=ple_gate_w, ple_proj_w=ple_proj_w, ple_post_norm_g=ple_post_norm_g)
    batch, seq, d = x.shape
    n = batch * seq
    depth = w_in.shape[0]
    consts = _constants()
    cos, sin = _rope_tables(positions)
    rope = _rope_layouts(cos, sin, batch, seq)
    h = x.reshape(n, d)
    for i in range(depth):
        lw = _layer_weights(i, consts, params)
        ya, yb, q, kt, v = _mixer_pre(h, rope, lw, batch, seq)
        yc = _attention(q, kt, v, batch, seq)
        h1, m, eidx, wts = _outproj_router(h, ya, yb, yc, lw)
        h = _moe(h1, m, eidx, wts, p[i].reshape(n, -1), lw, consts)
    return h.reshape(batch, seq, d)
```

```python
import functools
import math

import jax
import jax.numpy as jnp
from jax import lax
from jax.experimental import pallas as pl
from jax.experimental.pallas import tpu as pltpu

F32 = jnp.float32
BF16 = jnp.bfloat16

D_MODEL = 1024
HEAD_DIM = 64
SGU_HEADS = 4
SGU_WIDTH = 256
CHUNK = 128
CONV_WIDTH = 256
CONV_KERNEL = 31
MLA_HEADS = 8
MLA_NOPE = 64
MLA_ROPE = 32
MLA_QK = MLA_NOPE + MLA_ROPE
MLA_V = 64
Q_RANK = 256
KV_RANK = 128
ROPE_THETA = 10000.0
N_GROUPS = 4
EXPERTS_PER_GROUP = 8
N_EXPERTS = 32
D_EXPERT = 256
TOP_K = 2
EPS = 1e-6

LANES = 128
SUBLANES = 8
HEAD_PAD = 128
MIX_TILE = 512
CONV_HALO = 32
CONV_SHIFTED_ROWS = MIX_TILE + CONV_HALO - SUBLANES
MIXER_ORDER = ("q", "kv", "sgu", "conv")
ATT_TQ = 256
ATT_LOOKAHEAD = 1
RANK_CHUNK = 512
RANK_STEP = 2048
EXPERT_ROWS = 512
DISPATCH_TILE = 256
COMBINE_TILE = 256
ROW_TILES = D_MODEL // LANES
DMA_UNROLL = 8
VMEM_LIMIT = 48 * 1024 * 1024
NEG_BIG = -1e30


def _dot(a, b):
    return jnp.dot(a, b, preferred_element_type=F32)


def _rms(x, g):
    ms = jnp.mean(x * x, axis=-1, keepdims=True)
    return x * lax.rsqrt(ms + EPS) * g


def _seg_mean(x, g_ref, split=True):
    g = g_ref[...]
    outs = []
    for c in range(x.shape[1] // 256):
        xb = x[:, c * 256:(c + 1) * 256]
        hi = xb.astype(BF16)
        acc = _dot(hi, g)
        if split:
            acc = acc + _dot((xb - hi.astype(F32)).astype(BF16), g)
        outs.append(acc)
    return outs[0] if len(outs) == 1 else jnp.concatenate(outs, axis=-1)


def _group_layernorm(x, g_ref, gain, bias):
    d = x - _seg_mean(x, g_ref)
    var = _seg_mean(d * d, g_ref)
    return d * lax.rsqrt(var + EPS) * gain + bias


def _rope_table_kernel(pos_ref, invf_ref, cos_ref, sin_ref):
    ang = pos_ref[...].astype(F32) * invf_ref[...]
    cos_ref[...] = jnp.cos(ang)
    sin_ref[...] = jnp.sin(ang)


def _rope_tables(positions):
    b, s = positions.shape
    n = b * s
    half = MLA_ROPE // 2
    inv_freq = ROPE_THETA ** (-jnp.arange(0, MLA_ROPE, 2, dtype=F32) / MLA_ROPE)
    pos_rep = jnp.broadcast_to(positions.reshape(n, 1), (n, half)).reshape(n * half // LANES, LANES)
    invf_rep = jnp.tile(inv_freq, LANES // half).reshape(1, LANES)
    shape = jax.ShapeDtypeStruct(pos_rep.shape, F32)
    cos, sin = pl.pallas_call(_rope_table_kernel, out_shape=(shape, shape), name="rope_tables")(pos_rep, invf_rep)
    return cos.reshape(n, half), sin.reshape(n, half)


def _mixer_pre_kernel(
        h_ref, cos_ref, sin_ref,
        mixg_ref, wsgu_ref, wconv_ref, wcq_ref, wckv_ref,
        slng_ref, slnb_ref, wcat_ref, sbias_ref, g64_ref, bga_ref,
        cw_ref, cb_ref, clng_ref, clnb_ref, pww_ref, pwb_ref, bgb_ref,
        qag_ref, wuq_ref, gqk_ref, qgain_ref,
        kvag_ref, wukt_ref, wuv_ref, vones_ref, kng_ref, kpeg_ref,
        ya_ref, yb_ref, q_ref, kt_ref, v_ref,
        ybuf_ref, ysh_ref, *, tiles_per_seq):
    t = MIX_TILE
    a = _rms(h_ref[...], mixg_ref[...]).astype(BF16)
    zs = _dot(a, wsgu_ref[...])
    zc = _dot(a, wconv_ref[...])
    zq = _dot(a, wcq_ref[...])
    zkv = _dot(a, wckv_ref[...])

    half = MLA_ROPE // 2
    cos, sin = cos_ref[...], sin_ref[...]
    zn = jnp.zeros((t, MLA_NOPE), F32)
    zh = jnp.zeros((t, half), F32)
    zp = jnp.zeros((t, HEAD_PAD - MLA_QK), F32)
    rc = jnp.concatenate([jnp.ones((t, MLA_NOPE), F32), cos, cos, zp], axis=1)
    rs1 = jnp.concatenate([zn, -sin, zh, zp], axis=1)
    rs2 = jnp.concatenate([zn, zh, sin, zp], axis=1)

    def sgu_branch():
        zg = jax.nn.gelu(zs)
        u = zg[:, :SGU_WIDTH]
        vn = _group_layernorm(zg[:, SGU_WIDTH:], g64_ref, slng_ref[...], slnb_ref[...])
        lane_head = lax.broadcasted_iota(jnp.int32, (CHUNK, SGU_WIDTH), 1) // HEAD_DIM
        wcat = wcat_ref[...]
        sbias = sbias_ref[...]
        parts = []
        for c in range(t // CHUNK):
            vc = vn[c * CHUNK:(c + 1) * CHUNK]
            stacked = jnp.concatenate(
                [jnp.where(lane_head == hh, vc, 0.0) for hh in range(SGU_HEADS)], axis=0).astype(BF16)
            s = _dot(wcat, stacked) + sbias
            parts.append(u[c * CHUNK:(c + 1) * CHUNK] * s)
        ya_ref[...] = _rms(jnp.concatenate(parts, axis=0), bga_ref[...])

    def conv_branch():
        yg = zc[:, :CONV_WIDTH] * jax.nn.sigmoid(zc[:, CONV_WIDTH:])
        first = (pl.program_id(0) % tiles_per_seq) == 0

        @pl.when(first)
        def _():
            ybuf_ref[0:CONV_HALO, :] = jnp.zeros((CONV_HALO, CONV_WIDTH), F32)

        @pl.when(jnp.logical_not(first))
        def _():
            ybuf_ref[0:CONV_HALO, :] = ybuf_ref[t:t + CONV_HALO, :]

        ybuf_ref[CONV_HALO:CONV_HALO + t, :] = yg
        for sh in range(1, SUBLANES):
            ysh_ref[sh - 1] = ybuf_ref[pl.ds(sh, CONV_SHIFTED_ROWS), :]
        rows = 64
        first_tap_row = CONV_HALO - (CONV_KERNEL - 1)
        conv_parts = []
        for r in range(t // rows):
            acc = jnp.broadcast_to(cb_ref[...], (rows, CONV_WIDTH))
            for tap in range(CONV_KERNEL):
                sh = (first_tap_row + tap) % SUBLANES
                start = r * rows + first_tap_row + tap - sh
                src = ybuf_ref[pl.ds(start, rows), :] if sh == 0 else ysh_ref[sh - 1, pl.ds(start, rows), :]
                acc = acc + cw_ref[tap:tap + 1, :] * src
            conv_parts.append(acc)
        cv = jnp.concatenate(conv_parts, axis=0)
        cn = _group_layernorm(cv, g64_ref, clng_ref[...], clnb_ref[...])
        yb = _dot(jax.nn.silu(cn).astype(BF16), pww_ref[...]) + pwb_ref[...]
        yb_ref[...] = _rms(yb, bgb_ref[...])

    def query_branch():
        cqn = _rms(zq, qag_ref[...]).astype(BF16)
        qf = _dot(cqn, wuq_ref[...])
        qn = qf * lax.rsqrt(_seg_mean(qf * qf, gqk_ref, split=False) + EPS) * qgain_ref[...]
        for hh in range(MLA_HEADS):
            blk = qn[:, hh * HEAD_PAD:(hh + 1) * HEAD_PAD]
            rot = blk * rc + pltpu.roll(blk, HEAD_PAD - half, 1) * rs1 + pltpu.roll(blk, half, 1) * rs2
            q_ref[:, hh * HEAD_PAD:(hh + 1) * HEAD_PAD] = rot.astype(BF16)

    def key_value_branch():
        ckvn = _rms(zkv[:, :KV_RANK], kvag_ref[...]).astype(BF16)
        v_ref[...] = (_dot(ckvn, wuv_ref[...]) + vones_ref[...]).astype(BF16)
        knt = lax.dot_general(wukt_ref[...], ckvn, (((1,), (1,)), ((), ())), preferred_element_type=F32)
        x = zkv[:, KV_RANK:].T[0:MLA_ROPE]
        xn = x * lax.rsqrt(jnp.mean(x * x, axis=0, keepdims=True) + EPS) * kpeg_ref[...]
        x1, x2 = xn[:MLA_ROPE // 2], xn[MLA_ROPE // 2:]
        cos_t = rc.T[MLA_NOPE:MLA_NOPE + half]
        sin_t = rs2.T[MLA_NOPE + half:MLA_QK]
        kpe = jnp.concatenate([x1 * cos_t - x2 * sin_t, x2 * cos_t + x1 * sin_t], axis=0)
        pad = jnp.zeros((HEAD_PAD - MLA_QK, t), F32)
        kng = kng_ref[...]
        for hh in range(MLA_HEADS):
            blk = knt[hh * MLA_NOPE:(hh + 1) * MLA_NOPE]
            kn = blk * lax.rsqrt(jnp.mean(blk * blk, axis=0, keepdims=True) + EPS) * kng
            kt_ref[0, hh * HEAD_PAD:(hh + 1) * HEAD_PAD, :] = jnp.concatenate([kn, kpe, pad], axis=0).astype(BF16)

    branches = {"sgu": sgu_branch, "conv": conv_branch, "q": query_branch, "kv": key_value_branch}
    for name in MIXER_ORDER:
        branches[name]()


def _full(shape):
    nd = len(shape)
    return pl.BlockSpec(shape, lambda *_: (0,) * nd)


def _mixer_pre(h, cos, sin, lw, batch, seq):
    n = h.shape[0]
    t = MIX_TILE
    tps = seq // t
    row = lambda w: pl.BlockSpec((t, w), lambda i: (i, 0))
    weights = [lw[k] for k in (
        "mix_g", "w_sgu", "w_conv", "w_cq", "w_ckv",
        "sgu_ln_g", "sgu_ln_b", "sgu_wcat", "sgu_bias", "g64", "bg_a",
        "conv_w", "conv_b", "conv_ln_g", "conv_ln_b", "pw_w", "pw_b", "bg_b",
        "qa_g", "w_uq", "gqk", "q_gain",
        "kva_g", "w_ukt", "w_uv", "v_ones", "kn_g", "kpe_g")]
    in_specs = [row(D_MODEL), row(MLA_ROPE // 2), row(MLA_ROPE // 2)] + [_full(w.shape) for w in weights]
    out_shape = (
        jax.ShapeDtypeStruct((n, SGU_WIDTH), F32),
        jax.ShapeDtypeStruct((n, CONV_WIDTH), F32),
        jax.ShapeDtypeStruct((n, MLA_HEADS * HEAD_PAD), BF16),
        jax.ShapeDtypeStruct((batch, MLA_HEADS * HEAD_PAD, seq), BF16),
        jax.ShapeDtypeStruct((n, MLA_HEADS * HEAD_PAD), BF16),
    )
    out_specs = (
        row(SGU_WIDTH), row(CONV_WIDTH), row(MLA_HEADS * HEAD_PAD),
        pl.BlockSpec((1, MLA_HEADS * HEAD_PAD, t), lambda i: (i // tps, 0, i % tps)),
        row(MLA_HEADS * HEAD_PAD),
    )
    return pl.pallas_call(
        functools.partial(_mixer_pre_kernel, tiles_per_seq=tps),
        grid=(n // t,),
        in_specs=in_specs,
        out_specs=out_specs,
        out_shape=out_shape,
        scratch_shapes=[pltpu.VMEM((t + CONV_HALO, CONV_WIDTH), F32),
                        pltpu.VMEM((SUBLANES - 1, CONV_SHIFTED_ROWS, CONV_WIDTH), F32)],
        compiler_params=pltpu.CompilerParams(dimension_semantics=("arbitrary",), vmem_limit_bytes=VMEM_LIMIT),
        name="mixer_pre",
    )(h, cos, sin, *weights)


def _attention_kernel(q_ref, kt_ref, v_ref, o_ref, *, seq):
    tq = ATT_TQ
    exp2_scale = MLA_QK ** -0.5 * math.log2(math.e)
    row = lax.broadcasted_iota(jnp.int32, (tq, tq), 0)
    col = lax.broadcasted_iota(jnp.int32, (tq, tq), 1)
    lane = lax.broadcasted_iota(jnp.int32, (tq, HEAD_PAD), 1)
    def scores(qi, hh):
        nk = (qi + 1) * tq
        q = q_ref[qi * tq:(qi + 1) * tq, hh * HEAD_PAD:(hh + 1) * HEAD_PAD]
        s = _dot(q, kt_ref[0, hh * HEAD_PAD:(hh + 1) * HEAD_PAD, 0:nk])
        diag = jnp.where(col <= row, s[:, nk - tq:], NEG_BIG)
        return diag if qi == 0 else jnp.concatenate([s[:, :nk - tq], diag], axis=1)

    items = [(qi, hh) for qi in range(seq // tq) for hh in range(2)]
    ahead = [scores(*it) for it in items[:ATT_LOOKAHEAD]]
    outs = []
    for idx, (qi, hh) in enumerate(items):
        s = ahead.pop(0)
        if idx + ATT_LOOKAHEAD < len(items):
            ahead.append(scores(*items[idx + ATT_LOOKAHEAD]))
        p = jnp.exp2(((s - jnp.max(s, axis=-1, keepdims=True)) * exp2_scale).astype(BF16))
        acc = _dot(p, v_ref[0:(qi + 1) * tq, :])[:, hh * HEAD_PAD:(hh + 1) * HEAD_PAD]
        outs.append(acc / pltpu.roll(acc, MLA_V, 1))
        if hh == 1:
            o_ref[qi * tq:(qi + 1) * tq, :] = jnp.where(lane < MLA_V, outs[0], outs[1])
            outs = []


def _attention(q, kt, v, batch, seq):
    n = q.shape[0]
    pairs = MLA_HEADS // 2
    return pl.pallas_call(
        functools.partial(_attention_kernel, seq=seq),
        grid=(batch, pairs),
        in_specs=[
            pl.BlockSpec((seq, 2 * HEAD_PAD), lambda b, p: (b, p)),
            pl.BlockSpec((1, 2 * HEAD_PAD, seq), lambda b, p: (b, p, 0)),
            pl.BlockSpec((seq, 2 * HEAD_PAD), lambda b, p: (b, p)),
        ],
        out_specs=pl.BlockSpec((seq, 2 * MLA_V), lambda b, p: (b, p)),
        out_shape=jax.ShapeDtypeStruct((n, MLA_HEADS * MLA_V), F32),
        compiler_params=pltpu.CompilerParams(
            dimension_semantics=("arbitrary", "arbitrary"), vmem_limit_bytes=VMEM_LIMIT),
        name="attention",
    )(q, kt, v)


def _outproj_router_kernel(h_ref, ya_ref, yb_ref, yc_ref, bgc_ref, woa_ref, wob_ref, woc_ref,
                           ffng_ref, wr_ref, br_ref,
                           h1_ref, m_ref, eidx_ref, wts_ref):
    t = MIX_TILE
    ycn = _rms(yc_ref[...], bgc_ref[...])
    proj = (_dot(ya_ref[...].astype(BF16), woa_ref[...]) + _dot(yb_ref[...].astype(BF16), wob_ref[...])
            + _dot(ycn.astype(BF16), woc_ref[...]))
    h1 = h_ref[...] + proj
    h1_ref[...] = h1
    m = _rms(h1, ffng_ref[...])
    _store_row_tiled(m_ref, m)

    mh = m.astype(BF16)
    ml = (m - mh.astype(F32)).astype(BF16)
    wr = wr_ref[...]
    both = _dot(mh, wr)
    logits = both[:, :LANES] + both[:, LANES:] + _dot(ml, wr[:, :LANES]) + br_ref[...]
    lt = logits.T
    rowi = lax.broadcasted_iota(jnp.int32, (EXPERTS_PER_GROUP, t), 0)
    g8 = lt[0:8]
    gmax = jnp.max(g8, axis=0, keepdims=True)
    gsum = jnp.sum(jnp.exp(g8 - gmax), axis=0, keepdims=True)
    gidx = jnp.min(jnp.where(g8 == gmax, rowi, 8), axis=0, keepdims=True)
    g_w = 1.0 / gsum
    esel = jnp.zeros((EXPERTS_PER_GROUP, t), F32)
    for g in range(N_GROUPS):
        esel = jnp.where(gidx == g, lt[8 + g * EXPERTS_PER_GROUP:8 + (g + 1) * EXPERTS_PER_GROUP], esel)
    ep = jnp.exp(esel - jnp.max(esel, axis=0, keepdims=True))
    eprob = ep / jnp.sum(ep, axis=0, keepdims=True)
    v1 = jnp.max(eprob, axis=0, keepdims=True)
    i1 = jnp.min(jnp.where(eprob == v1, rowi, 8), axis=0, keepdims=True)
    rest = jnp.where(rowi == i1, -1.0, eprob)
    v2 = jnp.max(rest, axis=0, keepdims=True)
    i2 = jnp.min(jnp.where(rest == v2, rowi, 8), axis=0, keepdims=True)
    den = v1 + v2
    e1 = gidx * EXPERTS_PER_GROUP + i1
    e2 = gidx * EXPERTS_PER_GROUP + i2
    eidx_ref[...] = jnp.where(rowi == 0, e1, jnp.where(rowi == 1, e2, 0))
    wts_ref[...] = jnp.where(rowi == 0, g_w * (v1 / den), jnp.where(rowi == 1, g_w * (v2 / den), 0.0))


def _outproj_router(h, ya, yb, yc, lw):
    n = h.shape[0]
    t = MIX_TILE
    row = lambda w: pl.BlockSpec((t, w), lambda i: (i, 0))
    weights = [lw[k] for k in ("bg_c", "w_o_a", "w_o_b", "w_o_c", "ffn_g", "wr_split", "br")]
    colspec = pl.BlockSpec((8, t), lambda i: (0, i))
    return pl.pallas_call(
        _outproj_router_kernel,
        grid=(n // t,),
        in_specs=[row(D_MODEL), row(SGU_WIDTH), row(CONV_WIDTH), row(MLA_HEADS * MLA_V)]
        + [_full(w.shape) for w in weights],
        out_specs=(row(D_MODEL), pl.BlockSpec((t * ROW_TILES, LANES), lambda i: (i, 0)), colspec, colspec),
        out_shape=(
            jax.ShapeDtypeStruct((n, D_MODEL), F32),
            jax.ShapeDtypeStruct((n * ROW_TILES, LANES), F32),
            jax.ShapeDtypeStruct((8, n), jnp.int32),
            jax.ShapeDtypeStruct((8, n), F32),
        ),
        compiler_params=pltpu.CompilerParams(dimension_semantics=("arbitrary",), vmem_limit_bytes=VMEM_LIMIT),
        name="outproj_router",
    )(h, ya, yb, yc, *weights)


def _moe_rank_kernel(e_ref, upper_ref, ones_ref, ltri_ref, dest_ref, cnt_ref, run_ref, base_ref):
    phase = pl.program_id(0)
    step = pl.program_id(1)
    c = RANK_CHUNK
    expert = lax.broadcasted_iota(jnp.int32, (N_EXPERTS, c), 0)

    @pl.when(jnp.logical_and(phase == 0, step == 0))
    def _():
        run_ref[...] = jnp.zeros_like(run_ref)
        base_ref[...] = jnp.zeros_like(base_ref)

    @pl.when(jnp.logical_and(phase == 1, step == 0))
    def _():
        blocks = jnp.floor((run_ref[...] + (EXPERT_ROWS - 1)) * (1.0 / EXPERT_ROWS))
        base_ref[...] = _dot(ltri_ref[...], blocks.astype(BF16)) * EXPERT_ROWS
        run_ref[...] = jnp.zeros_like(run_ref)

    for sub in range(RANK_STEP // c):
        cols = slice(sub * c, (sub + 1) * c)
        onehot = jnp.where(expert == e_ref[:, cols], 1.0, 0.0)
        oh16 = onehot.astype(BF16)
        before = _dot(oh16, upper_ref[...])
        pos = before + run_ref[:, 0:1] + base_ref[:, 0:1]
        dest_ref[:, cols] = jnp.sum(onehot * pos, axis=0, keepdims=True).astype(jnp.int32)
        run_ref[...] = run_ref[...] + _dot(oh16, ones_ref[...])
    cnt_ref[...] = run_ref[...]


def _moe_rank(e_flat, consts):
    total = e_flat.shape[1]
    c = RANK_CHUNK
    st = RANK_STEP
    return pl.pallas_call(
        _moe_rank_kernel,
        grid=(2, total // st),
        in_specs=[pl.BlockSpec((1, st), lambda p, s: (0, s)),
                  _full((c, c)), _full((c, LANES)), _full((N_EXPERTS, N_EXPERTS))],
        out_specs=(pl.BlockSpec((1, st), lambda p, s: (0, s * p)),
                   pl.BlockSpec((N_EXPERTS, LANES), lambda p, s: (0, 0))),
        out_shape=(jax.ShapeDtypeStruct((1, total), jnp.int32),
                   jax.ShapeDtypeStruct((N_EXPERTS, LANES), F32)),
        scratch_shapes=[pltpu.VMEM((N_EXPERTS, LANES), F32), pltpu.VMEM((N_EXPERTS, LANES), F32)],
        compiler_params=pltpu.CompilerParams(dimension_semantics=("arbitrary", "arbitrary")),
        name="moe_rank",
    )(e_flat, consts["upper"], consts["ones"], consts["ltri"])


def _store_row_tiled(ref, x, offset=0):
    rows = x.shape[0]
    for s in range(ROW_TILES):
        ref[pl.ds(offset + s, rows, stride=ROW_TILES), :] = x[:, s * LANES:(s + 1) * LANES]


def _load_row_tiled(ref, rows, offset=0):
    return jnp.concatenate(
        [ref[pl.ds(offset + s, rows, stride=ROW_TILES), :] for s in range(ROW_TILES)], axis=1)


def _row_copy(src, src_row, dst, dst_row, sem):
    return pltpu.make_async_copy(
        src.at[pl.ds(pl.multiple_of(src_row * ROW_TILES, ROW_TILES), ROW_TILES)],
        dst.at[pl.ds(pl.multiple_of(dst_row * ROW_TILES, ROW_TILES), ROW_TILES)], sem)


def _rows_wait(src, dst, dst_row, rows, sem):
    pltpu.make_async_copy(
        src.at[pl.ds(0, rows * ROW_TILES)],
        dst.at[pl.ds(pl.multiple_of(dst_row * ROW_TILES, ROW_TILES), rows * ROW_TILES)], sem).wait()


def _moe_dispatch_kernel(dest_ref, cnt_ref, region_ref, m_ref, xs_ref, zbuf_ref, sem, zsem, *, n_tok):
    t = DISPATCH_TILE
    r = EXPERT_ROWS
    i = pl.program_id(0)

    @pl.when(i == 0)
    def _():
        zbuf_ref[...] = jnp.zeros_like(zbuf_ref)

        def zero_copy(e):
            last = pl.multiple_of((region_ref[e + 1] - r) * ROW_TILES, r * ROW_TILES)
            return pltpu.make_async_copy(zbuf_ref, xs_ref.at[pl.ds(last, r * ROW_TILES)], zsem)

        def start(e, carry):
            @pl.when(cnt_ref[e] > 0)
            def _():
                zero_copy(e).start()
            return carry

        def wait(e, carry):
            @pl.when(cnt_ref[e] > 0)
            def _():
                zero_copy(e).wait()
            return carry

        def tail_copy(blk):
            return pltpu.make_async_copy(
                zbuf_ref, xs_ref.at[pl.ds(pl.multiple_of(blk * (r * ROW_TILES), r * ROW_TILES), r * ROW_TILES)], zsem)

        def tail_start(blk, carry):
            tail_copy(blk).start()
            return carry

        def tail_wait(blk, carry):
            tail_copy(blk).wait()
            return carry

        first_unused = region_ref[N_EXPERTS] // r
        n_blocks = xs_ref.shape[0] // (r * ROW_TILES)
        lax.fori_loop(0, N_EXPERTS, start, 0)
        lax.fori_loop(first_unused, n_blocks, tail_start, 0)
        lax.fori_loop(0, N_EXPERTS, wait, 0)
        lax.fori_loop(first_unused, n_blocks, tail_wait, 0)

    per_iter = DMA_UNROLL // TOP_K

    def issue(g, carry):
        for u in range(per_iter):
            row = g * per_iter + u
            for k in range(TOP_K):
                _row_copy(m_ref, row, xs_ref, dest_ref[k * n_tok + i * t + row], sem).start(priority=k % 2)
        return carry

    lax.fori_loop(0, t // per_iter, issue, 0)
    for k in range(TOP_K):
        pltpu.make_async_copy(m_ref, xs_ref.at[pl.ds(0, t * ROW_TILES)], sem).wait()


def _moe_dispatch(dest, cnt, region, m_rt, rows):
    n = m_rt.shape[0] // ROW_TILES
    t = DISPATCH_TILE
    return pl.pallas_call(
        functools.partial(_moe_dispatch_kernel, n_tok=n),
        grid_spec=pltpu.PrefetchScalarGridSpec(
            num_scalar_prefetch=3,
            grid=(n // t,),
            in_specs=[pl.BlockSpec((t * ROW_TILES, LANES), lambda i, d, c, rg: (i, 0))],
            out_specs=pl.BlockSpec(memory_space=pl.ANY),
            scratch_shapes=[pltpu.VMEM((EXPERT_ROWS * ROW_TILES, LANES), F32),
                            pltpu.SemaphoreType.DMA(()), pltpu.SemaphoreType.DMA(())],
        ),
        out_shape=jax.ShapeDtypeStruct((rows * ROW_TILES, LANES), F32),
        compiler_params=pltpu.CompilerParams(dimension_semantics=("arbitrary",)),
        name="moe_dispatch",
    )(dest, cnt, region, m_rt)


def _moe_experts_kernel(blk_e_ref, nact_ref, x_ref, wg_ref, wu_ref, wd_ref, y_ref, wgu_ref, wdn_ref):
    r = EXPERT_ROWS
    b = pl.program_id(0)
    live = b < nact_ref[0]

    @pl.when(live)
    def _():
        @pl.when(jnp.logical_or(b == 0, blk_e_ref[b] != blk_e_ref[jnp.maximum(b - 1, 0)]))
        def _():
            wgu_ref[:, :D_EXPERT] = wg_ref[0, 0].astype(BF16)
            wgu_ref[:, D_EXPERT:] = wu_ref[0, 0].astype(BF16)
            wdn_ref[...] = wd_ref[0, 0].astype(BF16)

        x = _load_row_tiled(x_ref, r).astype(BF16)
        gu = _dot(x, wgu_ref[...])
        hb = jax.nn.silu(gu[:, :D_EXPERT]) * gu[:, D_EXPERT:]
        _store_row_tiled(y_ref, _dot(hb.astype(BF16), wdn_ref[...]))

    @pl.when(jnp.logical_not(live))
    def _():
        y_ref[...] = jnp.zeros_like(y_ref)


def _moe_experts(blk_e, nact, xs_rt, layer, w_gate, w_up, w_down):
    rows = xs_rt.shape[0] // ROW_TILES
    r = EXPERT_ROWS
    wspec = lambda k, n: pl.BlockSpec((1, 1, k, n), lambda b, be, na: (layer, be[b], 0, 0))
    return pl.pallas_call(
        _moe_experts_kernel,
        grid_spec=pltpu.PrefetchScalarGridSpec(
            num_scalar_prefetch=2,
            grid=(rows // r,),
            in_specs=[pl.BlockSpec((r * ROW_TILES, LANES), lambda b, be, na: (jnp.minimum(b, na[0] - 1), 0)),
                      wspec(D_MODEL, D_EXPERT), wspec(D_MODEL, D_EXPERT), wspec(D_EXPERT, D_MODEL)],
            out_specs=pl.BlockSpec((r * ROW_TILES, LANES), lambda b, be, na: (b, 0)),
            scratch_shapes=[pltpu.VMEM((D_MODEL, 2 * D_EXPERT), BF16),
                            pltpu.VMEM((D_EXPERT, D_MODEL), BF16)],
        ),
        out_shape=jax.ShapeDtypeStruct((rows * ROW_TILES, LANES), F32),
        compiler_params=pltpu.CompilerParams(dimension_semantics=("arbitrary",), vmem_limit_bytes=VMEM_LIMIT),
        name="moe_experts",
    )(blk_e, nact, xs_rt, w_gate, w_up, w_down)


def _combine_ple_kernel(dest_ref, h1_ref, wt_ref, p_ref, y_ref, pleg_ref, gatew_ref, projw_ref, postg_ref,
                        out_ref, ybuf_ref, sem, *, n_tok):
    t = COMBINE_TILE
    i = pl.program_id(0)

    def gather(step, slot):
        def issue(g, carry):
            for u in range(DMA_UNROLL // TOP_K):
                r = g * (DMA_UNROLL // TOP_K) + u
                for k in range(TOP_K):
                    _row_copy(y_ref, dest_ref[k * n_tok + step * t + r], ybuf_ref, (slot * TOP_K + k) * t + r,
                              sem.at[slot]).start(priority=k % 2)
            return carry
        lax.fori_loop(0, t // (DMA_UNROLL // TOP_K), issue, 0)

    @pl.when(i == 0)
    def _():
        gather(0, 0)

    @pl.when(i + 1 < pl.num_programs(0))
    def _():
        gather(i + 1, (i + 1) % 2)

    e = _rms(_dot(p_ref[...].astype(BF16), projw_ref[...]), postg_ref[...])
    slot = i % 2
    _rows_wait(y_ref, ybuf_ref, slot * TOP_K * t, TOP_K * t, sem.at[slot])
    wt = wt_ref[...]
    y0 = _load_row_tiled(ybuf_ref, t, (slot * TOP_K) * (t * ROW_TILES))
    y1 = _load_row_tiled(ybuf_ref, t, (slot * TOP_K + 1) * (t * ROW_TILES))
    h2 = h1_ref[...] + (wt[:, 0:1] * y0 + wt[:, 1:2] * y1)
    gate = jax.nn.sigmoid(_dot(_rms(h2, pleg_ref[...]).astype(BF16), gatew_ref[...]))
    out_ref[...] = h2 + gate * e


def _combine_ple(dest, h1, wt, p, y_rt, lw):
    n = h1.shape[0]
    t = COMBINE_TILE
    row = lambda w: pl.BlockSpec((t, w), lambda i, d: (i, 0))
    layer_rows = lw["layer"] * (n // t)
    weights = [lw[k] for k in ("ple_g", "gate_w", "proj_w", "post_g")]
    return pl.pallas_call(
        functools.partial(_combine_ple_kernel, n_tok=n),
        grid_spec=pltpu.PrefetchScalarGridSpec(
            num_scalar_prefetch=1,
            grid=(n // t,),
            in_specs=[row(D_MODEL), row(TOP_K), pl.BlockSpec((t, p.shape[1]), lambda i, d: (layer_rows + i, 0)),
                      pl.BlockSpec(memory_space=pl.ANY)]
            + [pl.BlockSpec(w.shape, lambda i, d, nd=w.ndim: (0,) * nd) for w in weights],
            out_specs=row(D_MODEL),
            scratch_shapes=[pltpu.VMEM((2 * TOP_K * t * ROW_TILES, LANES), F32), pltpu.SemaphoreType.DMA((2,))],
        ),
        out_shape=jax.ShapeDtypeStruct((n, D_MODEL), F32),
        compiler_params=pltpu.CompilerParams(dimension_semantics=("arbitrary",), vmem_limit_bytes=VMEM_LIMIT),
        name="combine_ple",
    )(dest, h1, wt, p, y_rt, *weights)


def _segment_matrix(seg_ids):
    seg_ids = jnp.asarray(seg_ids)
    same = (seg_ids[:, None] == seg_ids[None, :]).astype(F32)
    return (same / jnp.sum(same, axis=1, keepdims=True)).astype(BF16)


def _constants():
    lane = jnp.arange(256)
    qk_seg = (lane // HEAD_PAD) * 3 + jnp.where(lane % HEAD_PAD < MLA_NOPE, 0, jnp.where(lane % HEAD_PAD < MLA_QK, 1, 2))
    i = jnp.arange(RANK_CHUNK)
    e = jnp.arange(N_EXPERTS)
    return {
        "g64": _segment_matrix(lane // HEAD_DIM),
        "gqk": _segment_matrix(qk_seg),
        "upper": (i[:, None] < i[None, :]).astype(BF16),
        "ones": jnp.ones((RANK_CHUNK, LANES), BF16),
        "ltri": (e[None, :] < e[:, None]).astype(BF16),
    }


def _layer_weights(i, consts, p):
    row = lambda v: v.reshape(1, -1).astype(F32)
    w_in = p["w_in"][i]
    o_sgu, o_conv, o_q, o_kv = 2 * SGU_WIDTH, 2 * SGU_WIDTH + 2 * CONV_WIDTH, 0, 0
    o_q = o_conv + Q_RANK
    o_kv = o_q + KV_RANK
    w_ckv = jnp.concatenate([w_in[:, o_q:], jnp.zeros((D_MODEL, 256 - KV_RANK - MLA_ROPE), F32)], axis=1)

    causal = jnp.tril(jnp.ones((CHUNK, CHUNK), F32))
    wcat = jnp.transpose(p["sgu_w"][i] * causal, (1, 0, 2)).reshape(CHUNK, SGU_HEADS * CHUNK)
    sbias = jnp.broadcast_to(jnp.transpose(p["sgu_b"][i])[:, :, None],
                             (CHUNK, SGU_HEADS, HEAD_DIM)).reshape(CHUNK, SGU_WIDTH)

    w_uq = p["w_uq"][i].reshape(Q_RANK, MLA_HEADS, MLA_QK)
    w_uq = jnp.pad(w_uq, ((0, 0), (0, 0), (0, HEAD_PAD - MLA_QK))).reshape(Q_RANK, MLA_HEADS * HEAD_PAD)
    qn_g = p["q_norm_g"][i]
    q_gain = jnp.tile(jnp.concatenate([qn_g, jnp.zeros((HEAD_PAD - MLA_QK,), F32)]), MLA_HEADS)
    w_ukv = p["w_ukv"][i].reshape(KV_RANK, MLA_HEADS, MLA_NOPE + MLA_V)
    w_ukt = jnp.transpose(w_ukv[:, :, :MLA_NOPE], (1, 2, 0)).reshape(MLA_HEADS * MLA_NOPE, KV_RANK)
    w_v = w_ukv[:, :, MLA_NOPE:].reshape(KV_RANK, MLA_HEADS // 2, 2, MLA_V)
    zeros_v = jnp.zeros((KV_RANK, MLA_HEADS // 2, MLA_V), F32)
    w_uv = jnp.stack([w_v[:, :, 0], zeros_v, zeros_v, w_v[:, :, 1]], axis=2).reshape(KV_RANK, MLA_HEADS * HEAD_PAD)
    v_ones = jnp.tile(jnp.repeat(jnp.array([0.0, 1.0, 1.0, 0.0], F32), MLA_V), MLA_HEADS // 2)
    kn_g = p["k_norm_g"][i]

    bg = p["branch_norm_g"][i]
    w_o = p["w_o"][i]
    wr = jnp.concatenate([p["router_group_w"][i], jnp.zeros((D_MODEL, 8 - N_GROUPS), F32),
                          p["router_expert_w"][i], jnp.zeros((D_MODEL, LANES - 8 - N_EXPERTS), F32)], axis=1)
    wr_hi = wr.astype(BF16)
    br = jnp.concatenate([p["router_group_b"][i], jnp.full((8 - N_GROUPS,), NEG_BIG, F32),
                          p["router_expert_b"][i], jnp.zeros((LANES - 8 - N_EXPERTS,), F32)])
    return {
        "mix_g": row(p["mix_norm_g"][i]),
        "w_sgu": w_in[:, :o_sgu].astype(BF16),
        "w_conv": w_in[:, o_sgu:o_conv].astype(BF16),
        "w_cq": w_in[:, o_conv:o_q].astype(BF16),
        "w_ckv": w_ckv.astype(BF16),
        "sgu_ln_g": row(p["sgu_ln_g"][i]), "sgu_ln_b": row(p["sgu_ln_b"][i]),
        "sgu_wcat": wcat.astype(BF16), "sgu_bias": sbias, "g64": consts["g64"], "bg_a": row(bg[:SGU_WIDTH]),
        "conv_w": jnp.pad(p["conv_w"][i], ((0, 1), (0, 0))), "conv_b": row(p["conv_b"][i]),
        "conv_ln_g": row(p["conv_ln_g"][i]), "conv_ln_b": row(p["conv_ln_b"][i]),
        "pw_w": p["conv_pw_w"][i].astype(BF16), "pw_b": row(p["conv_pw_b"][i]),
        "bg_b": row(bg[SGU_WIDTH:SGU_WIDTH + CONV_WIDTH]),
        "qa_g": row(p["q_a_norm_g"][i]), "w_uq": w_uq.astype(BF16), "gqk": consts["gqk"],
        "q_gain": row(q_gain),
        "kva_g": row(p["kv_a_norm_g"][i]), "w_ukt": w_ukt.astype(BF16), "w_uv": w_uv.astype(BF16),
        "v_ones": row(v_ones),
        "kn_g": jnp.broadcast_to(kn_g[:MLA_NOPE, None], (MLA_NOPE, MIX_TILE)),
        "kpe_g": jnp.broadcast_to(kn_g[MLA_NOPE:, None], (MLA_ROPE, MIX_TILE)),
        "bg_c": row(bg[SGU_WIDTH + CONV_WIDTH:]),
        "w_o_a": w_o[:SGU_WIDTH].astype(BF16),
        "w_o_b": w_o[SGU_WIDTH:SGU_WIDTH + CONV_WIDTH].astype(BF16),
        "w_o_c": w_o[SGU_WIDTH + CONV_WIDTH:].astype(BF16),
        "ffn_g": row(p["ffn_norm_g"][i]),
        "wr_split": jnp.concatenate([wr_hi, (wr - wr_hi.astype(F32)).astype(BF16)], axis=1), "br": row(br),
        "layer": i, "w_gate": p["moe_w_gate"], "w_up": p["moe_w_up"], "w_down": p["moe_w_down"],
        "ple_g": row(p["ple_norm_g"][i]), "gate_w": p["ple_gate_w"][i].astype(BF16),
        "proj_w": p["ple_proj_w"][i].astype(BF16), "post_g": row(p["ple_post_norm_g"][i]),
    }


def _moe(h1, m, eidx, wts, pl_i, lw, consts):
    n = h1.shape[0]
    total = TOP_K * n
    rows = total + N_EXPERTS * EXPERT_ROWS
    dest2d, counts = _moe_rank(eidx[:TOP_K].reshape(1, total), consts)
    dest = dest2d.reshape(total)
    cnt = counts[:, 0].astype(jnp.int32)
    padded = (cnt + EXPERT_ROWS - 1) // EXPERT_ROWS * EXPERT_ROWS
    pend = jnp.cumsum(padded)
    nblk = rows // EXPERT_ROWS
    blk_start = jnp.arange(nblk, dtype=jnp.int32) * EXPERT_ROWS
    blk_e = jnp.minimum(jnp.sum((pend[None, :] <= blk_start[:, None]).astype(jnp.int32), axis=1), N_EXPERTS - 1)
    nact = (pend[-1:] // EXPERT_ROWS).astype(jnp.int32)
    region = jnp.concatenate([jnp.zeros((1,), jnp.int32), pend.astype(jnp.int32)])
    xs = _moe_dispatch(dest, cnt, region, m, rows)
    y = _moe_experts(blk_e, nact, xs, lw["layer"], lw["w_gate"], lw["w_up"], lw["w_down"])
    return _combine_ple(dest, h1, jnp.transpose(wts[:TOP_K]), pl_i, y, lw)


def kernel(x, p, positions, mix_norm_g, w_in, sgu_ln_g, sgu_ln_b, sgu_w, sgu_b, conv_w, conv_b, conv_ln_g, conv_ln_b, conv_pw_w, conv_pw_b, q_a_norm_g, w_uq, kv_a_norm_g, w_ukv, q_norm_g, k_norm_g, branch_norm_g, w_o, ffn_norm_g, router_group_w, router_group_b, router_expert_w, router_expert_b, moe_w_gate, moe_w_up, moe_w_down, ple_norm_g, ple_gate_w, ple_proj_w, ple_post_norm_g):
    params = dict(
        mix_norm_g=mix_norm_g, w_in=w_in, sgu_ln_g=sgu_ln_g, sgu_ln_b=sgu_ln_b, sgu_w=sgu_w, sgu_b=sgu_b,
        conv_w=conv_w, conv_b=conv_b, conv_ln_g=conv_ln_g, conv_ln_b=conv_ln_b, conv_pw_w=conv_pw_w,
        conv_pw_b=conv_pw_b, q_a_norm_g=q_a_norm_g, w_uq=w_uq, kv_a_norm_g=kv_a_norm_g, w_ukv=w_ukv,
        q_norm_g=q_norm_g, k_norm_g=k_norm_g, branch_norm_g=branch_norm_g, w_o=w_o, ffn_norm_g=ffn_norm_g,
        router_group_w=router_group_w, router_group_b=router_group_b, router_expert_w=router_expert_w,
        router_expert_b=router_expert_b, moe_w_gate=moe_w_gate, moe_w_up=moe_w_up, moe_w_down=moe_w_down,
        ple_norm_g=ple_norm_g, ple_gate_w=ple_gate_w, ple_proj_w=ple_proj_w, ple_post_norm_g=ple_post_norm_g)
    batch, seq, d = x.shape
    n = batch * seq
    depth = w_in.shape[0]
    consts = _constants()
    cos, sin = _rope_tables(positions)
    h = x.reshape(n, d)
    p_rows = p.reshape(depth * n, p.shape[-1])
    for i in range(depth):
        lw = _layer_weights(i, consts, params)
        ya, yb, q, kt, v = _mixer_pre(h, cos, sin, lw, batch, seq)
        yc = _attention(q, kt, v, batch, seq)
        h1, m, eidx, wts = _outproj_router(h, ya, yb, yc, lw)
        h = _moe(h1, m, eidx, wts, p_rows, lw, consts)
    return h.reshape(batch, seq, d)
```

```python
import functools
import math

import jax
import jax.numpy as jnp
from jax import lax
from jax.experimental import pallas as pl
from jax.experimental.pallas import tpu as pltpu

F32 = jnp.float32
BF16 = jnp.bfloat16

D_MODEL = 1024
HEAD_DIM = 64
SGU_HEADS = 4
SGU_WIDTH = 256
CHUNK = 128
CONV_WIDTH = 256
CONV_KERNEL = 31
MLA_HEADS = 8
MLA_NOPE = 64
MLA_ROPE = 32
MLA_QK = MLA_NOPE + MLA_ROPE
MLA_V = 64
Q_RANK = 256
KV_RANK = 128
ROPE_THETA = 10000.0
N_GROUPS = 4
EXPERTS_PER_GROUP = 8
N_EXPERTS = 32
D_EXPERT = 256
TOP_K = 2
EPS = 1e-6

LANES = 128
SUBLANES = 8
HEAD_PAD = 128
MIX_TILE = 512
CONV_HALO = 32
CONV_SHIFTED_ROWS = MIX_TILE + CONV_HALO - SUBLANES
MIXER_ORDER = ("q", "kv", "sgu", "conv")
ATT_TQ = 256
ATT_LOOKAHEAD = 1
RANK_CHUNK = 512
RANK_STEP = 2048
EXPERT_ROWS = 512
DISPATCH_TILE = 256
COMBINE_TILE = 256
ROW_TILES = D_MODEL // LANES
DMA_UNROLL = 8
VMEM_LIMIT = 48 * 1024 * 1024
NEG_BIG = -1e30


def _dot(a, b):
    return jnp.dot(a, b, preferred_element_type=F32)


def _rms(x, g):
    ms = jnp.mean(x * x, axis=-1, keepdims=True)
    return x * lax.rsqrt(ms + EPS) * g


def _seg_mean(x, g_ref, split=True):
    g = g_ref[...]
    outs = []
    for c in range(x.shape[1] // 256):
        xb = x[:, c * 256:(c + 1) * 256]
        hi = xb.astype(BF16)
        acc = _dot(hi, g)
        if split:
            acc = acc + _dot((xb - hi.astype(F32)).astype(BF16), g)
        outs.append(acc)
    return outs[0] if len(outs) == 1 else jnp.concatenate(outs, axis=-1)


def _group_layernorm(x, g_ref, gain, bias):
    d = x - _seg_mean(x, g_ref)
    var = _seg_mean(d * d, g_ref)
    return d * lax.rsqrt(var + EPS) * gain + bias


def _rope_table_kernel(pos_ref, invf_ref, cos_ref, sin_ref):
    ang = pos_ref[...].astype(F32) * invf_ref[...]
    cos_ref[...] = jnp.cos(ang)
    sin_ref[...] = jnp.sin(ang)


def _rope_tables(positions):
    b, s = positions.shape
    n = b * s
    half = MLA_ROPE // 2
    inv_freq = ROPE_THETA ** (-jnp.arange(0, MLA_ROPE, 2, dtype=F32) / MLA_ROPE)
    pos_rep = jnp.broadcast_to(positions.reshape(n, 1), (n, half)).reshape(n * half // LANES, LANES)
    invf_rep = jnp.tile(inv_freq, LANES // half).reshape(1, LANES)
    shape = jax.ShapeDtypeStruct(pos_rep.shape, F32)
    cos, sin = pl.pallas_call(_rope_table_kernel, out_shape=(shape, shape), name="rope_tables")(pos_rep, invf_rep)
    return cos.reshape(n, half), sin.reshape(n, half)


def _mixer_pre_kernel(
        h_ref, cos_ref, sin_ref,
        mixg_ref, wsgu_ref, wconv_ref, wcq_ref, wckv_ref,
        slng_ref, slnb_ref, wcat_ref, sbias_ref, g64_ref, bga_ref,
        cw_ref, cb_ref, clng_ref, clnb_ref, pww_ref, pwb_ref, bgb_ref,
        qag_ref, wuq_ref, gqk_ref, qgain_ref,
        kvag_ref, wukt_ref, wuv_ref, vones_ref, kng_ref, kpeg_ref,
        ya_ref, yb_ref, q_ref, kt_ref, v_ref,
        ybuf_ref, ysh_ref, *, tiles_per_seq):
    t = MIX_TILE
    a = _rms(h_ref[...], mixg_ref[...]).astype(BF16)
    zs = _dot(a, wsgu_ref[...])
    zc = _dot(a, wconv_ref[...])
    zq = _dot(a, wcq_ref[...])
    zkv = _dot(a, wckv_ref[...])

    half = MLA_ROPE // 2
    cos, sin = cos_ref[...], sin_ref[...]
    zn = jnp.zeros((t, MLA_NOPE), F32)
    zh = jnp.zeros((t, half), F32)
    zp = jnp.zeros((t, HEAD_PAD - MLA_QK), F32)
    rc = jnp.concatenate([jnp.ones((t, MLA_NOPE), F32), cos, cos, zp], axis=1)
    rs1 = jnp.concatenate([zn, -sin, zh, zp], axis=1)
    rs2 = jnp.concatenate([zn, zh, sin, zp], axis=1)

    def sgu_branch():
        zg = jax.nn.gelu(zs)
        u = zg[:, :SGU_WIDTH]
        vn = _group_layernorm(zg[:, SGU_WIDTH:], g64_ref, slng_ref[...], slnb_ref[...])
        lane_head = lax.broadcasted_iota(jnp.int32, (CHUNK, SGU_WIDTH), 1) // HEAD_DIM
        wcat = wcat_ref[...]
        sbias = sbias_ref[...]
        parts = []
        for c in range(t // CHUNK):
            vc = vn[c * CHUNK:(c + 1) * CHUNK]
            stacked = jnp.concatenate(
                [jnp.where(lane_head == hh, vc, 0.0) for hh in range(SGU_HEADS)], axis=0).astype(BF16)
            s = _dot(wcat, stacked) + sbias
            parts.append(u[c * CHUNK:(c + 1) * CHUNK] * s)
        ya_ref[...] = _rms(jnp.concatenate(parts, axis=0), bga_ref[...])

    def conv_branch():
        yg = zc[:, :CONV_WIDTH] * jax.nn.sigmoid(zc[:, CONV_WIDTH:])
        first = (pl.program_id(0) % tiles_per_seq) == 0

        @pl.when(first)
        def _():
            ybuf_ref[0:CONV_HALO, :] = jnp.zeros((CONV_HALO, CONV_WIDTH), F32)

        @pl.when(jnp.logical_not(first))
        def _():
            ybuf_ref[0:CONV_HALO, :] = ybuf_ref[t:t + CONV_HALO, :]

        ybuf_ref[CONV_HALO:CONV_HALO + t, :] = yg
        for sh in range(1, SUBLANES):
            ysh_ref[sh - 1] = ybuf_ref[pl.ds(sh, CONV_SHIFTED_ROWS), :]
        rows = 64
        first_tap_row = CONV_HALO - (CONV_KERNEL - 1)
        conv_parts = []
        for r in range(t // rows):
            acc = jnp.broadcast_to(cb_ref[...], (rows, CONV_WIDTH))
            for tap in range(CONV_KERNEL):
                sh = (first_tap_row + tap) % SUBLANES
                start = r * rows + first_tap_row + tap - sh
                src = ybuf_ref[pl.ds(start, rows), :] if sh == 0 else ysh_ref[sh - 1, pl.ds(start, rows), :]
                acc = acc + cw_ref[tap:tap + 1, :] * src
            conv_parts.append(acc)
        cv = jnp.concatenate(conv_parts, axis=0)
        cn = _group_layernorm(cv, g64_ref, clng_ref[...], clnb_ref[...])
        yb = _dot(jax.nn.silu(cn).astype(BF16), pww_ref[...]) + pwb_ref[...]
        yb_ref[...] = _rms(yb, bgb_ref[...])

    def query_branch():
        cqn = _rms(zq, qag_ref[...]).astype(BF16)
        qf = _dot(cqn, wuq_ref[...])
        qn = qf * lax.rsqrt(_seg_mean(qf * qf, gqk_ref, split=False) + EPS) * qgain_ref[...]
        for hh in range(MLA_HEADS):
            blk = qn[:, hh * HEAD_PAD:(hh + 1) * HEAD_PAD]
            rot = blk * rc + pltpu.roll(blk, HEAD_PAD - half, 1) * rs1 + pltpu.roll(blk, half, 1) * rs2
            q_ref[:, hh * HEAD_PAD:(hh + 1) * HEAD_PAD] = rot.astype(BF16)

    def key_value_branch():
        ckvn = _rms(zkv[:, :KV_RANK], kvag_ref[...]).astype(BF16)
        v_ref[...] = (_dot(ckvn, wuv_ref[...]) + vones_ref[...]).astype(BF16)
        knt = lax.dot_general(wukt_ref[...], ckvn, (((1,), (1,)), ((), ())), preferred_element_type=F32)
        x = zkv[:, KV_RANK:].T[0:MLA_ROPE]
        xn = x * lax.rsqrt(jnp.mean(x * x, axis=0, keepdims=True) + EPS) * kpeg_ref[...]
        x1, x2 = xn[:MLA_ROPE // 2], xn[MLA_ROPE // 2:]
        cos_t = rc.T[MLA_NOPE:MLA_NOPE + half]
        sin_t = rs2.T[MLA_NOPE + half:MLA_QK]
        kpe = jnp.concatenate([x1 * cos_t - x2 * sin_t, x2 * cos_t + x1 * sin_t], axis=0)
        pad = jnp.zeros((HEAD_PAD - MLA_QK, t), F32)
        kng = kng_ref[...]
        for hh in range(MLA_HEADS):
            blk = knt[hh * MLA_NOPE:(hh + 1) * MLA_NOPE]
            kn = blk * lax.rsqrt(jnp.mean(blk * blk, axis=0, keepdims=True) + EPS) * kng
            kt_ref[0, hh * HEAD_PAD:(hh + 1) * HEAD_PAD, :] = jnp.concatenate([kn, kpe, pad], axis=0).astype(BF16)

    branches = {"sgu": sgu_branch, "conv": conv_branch, "q": query_branch, "kv": key_value_branch}
    for name in MIXER_ORDER:
        branches[name]()


def _full(shape):
    nd = len(shape)
    return pl.BlockSpec(shape, lambda *_: (0,) * nd)


def _mixer_pre(h, cos, sin, lw, batch, seq):
    n = h.shape[0]
    t = MIX_TILE
    tps = seq // t
    row = lambda w: pl.BlockSpec((t, w), lambda i: (i, 0))
    weights = [lw[k] for k in (
        "mix_g", "w_sgu", "w_conv", "w_cq", "w_ckv",
        "sgu_ln_g", "sgu_ln_b", "sgu_wcat", "sgu_bias", "g64", "bg_a",
        "conv_w", "conv_b", "conv_ln_g", "conv_ln_b", "pw_w", "pw_b", "bg_b",
        "qa_g", "w_uq", "gqk", "q_gain",
        "kva_g", "w_ukt", "w_uv", "v_ones", "kn_g", "kpe_g")]
    in_specs = [row(D_MODEL), row(MLA_ROPE // 2), row(MLA_ROPE // 2)] + [_full(w.shape) for w in weights]
    out_shape = (
        jax.ShapeDtypeStruct((n, SGU_WIDTH), F32),
        jax.ShapeDtypeStruct((n, CONV_WIDTH), F32),
        jax.ShapeDtypeStruct((n, MLA_HEADS * HEAD_PAD), BF16),
        jax.ShapeDtypeStruct((batch, MLA_HEADS * HEAD_PAD, seq), BF16),
        jax.ShapeDtypeStruct((n, MLA_HEADS * HEAD_PAD), BF16),
    )
    out_specs = (
        row(SGU_WIDTH), row(CONV_WIDTH), row(MLA_HEADS * HEAD_PAD),
        pl.BlockSpec((1, MLA_HEADS * HEAD_PAD, t), lambda i: (i // tps, 0, i % tps)),
        row(MLA_HEADS * HEAD_PAD),
    )
    return pl.pallas_call(
        functools.partial(_mixer_pre_kernel, tiles_per_seq=tps),
        grid=(n // t,),
        in_specs=in_specs,
        out_specs=out_specs,
        out_shape=out_shape,
        scratch_shapes=[pltpu.VMEM((t + CONV_HALO, CONV_WIDTH), F32),
                        pltpu.VMEM((SUBLANES - 1, CONV_SHIFTED_ROWS, CONV_WIDTH), F32)],
        compiler_params=pltpu.CompilerParams(dimension_semantics=("arbitrary",), vmem_limit_bytes=VMEM_LIMIT),
        name="mixer_pre",
    )(h, cos, sin, *weights)


def _attention_kernel(q_ref, kt_ref, v_ref, o_ref, *, seq):
    tq = ATT_TQ
    exp2_scale = MLA_QK ** -0.5 * math.log2(math.e)
    row = lax.broadcasted_iota(jnp.int32, (tq, tq), 0)
    col = lax.broadcasted_iota(jnp.int32, (tq, tq), 1)
    lane = lax.broadcasted_iota(jnp.int32, (tq, HEAD_PAD), 1)
    def scores(qi, hh):
        nk = (qi + 1) * tq
        q = q_ref[qi * tq:(qi + 1) * tq, hh * HEAD_PAD:(hh + 1) * HEAD_PAD]
        s = _dot(q, kt_ref[0, hh * HEAD_PAD:(hh + 1) * HEAD_PAD, 0:nk])
        diag = jnp.where(col <= row, s[:, nk - tq:], NEG_BIG)
        return diag if qi == 0 else jnp.concatenate([s[:, :nk - tq], diag], axis=1)

    n_blocks = seq // tq
    ahead = [scores(0, hh) for hh in range(2)]
    for qi in range(n_blocks):
        s_pair, ahead, p_pair = ahead, [], []
        for hh in range(2):
            if qi + 1 < n_blocks:
                ahead.append(scores(qi + 1, hh))
            s = s_pair[hh]
            p_pair.append(jnp.exp2(((s - jnp.max(s, axis=-1, keepdims=True)) * exp2_scale).astype(BF16)))
        acc = _dot(jnp.concatenate(p_pair, axis=0), v_ref[0:(qi + 1) * tq, :])
        a0, a1 = acc[:tq, :HEAD_PAD], acc[tq:, HEAD_PAD:]
        o_ref[qi * tq:(qi + 1) * tq, :] = jnp.where(
            lane < MLA_V, a0 / pltpu.roll(a0, MLA_V, 1), a1 / pltpu.roll(a1, MLA_V, 1))


def _attention(q, kt, v, batch, seq):
    n = q.shape[0]
    pairs = MLA_HEADS // 2
    return pl.pallas_call(
        functools.partial(_attention_kernel, seq=seq),
        grid=(batch, pairs),
        in_specs=[
            pl.BlockSpec((seq, 2 * HEAD_PAD), lambda b, p: (b, p)),
            pl.BlockSpec((1, 2 * HEAD_PAD, seq), lambda b, p: (b, p, 0)),
            pl.BlockSpec((seq, 2 * HEAD_PAD), lambda b, p: (b, p)),
        ],
        out_specs=pl.BlockSpec((seq, 2 * MLA_V), lambda b, p: (b, p)),
        out_shape=jax.ShapeDtypeStruct((n, MLA_HEADS * MLA_V), F32),
        compiler_params=pltpu.CompilerParams(
            dimension_semantics=("arbitrary", "arbitrary"), vmem_limit_bytes=VMEM_LIMIT),
        name="attention",
    )(q, kt, v)


def _outproj_router_kernel(h_ref, ya_ref, yb_ref, yc_ref, bgc_ref, woa_ref, wob_ref, woc_ref,
                           ffng_ref, wr_ref, br_ref,
                           h1_ref, m_ref, eidx_ref, wts_ref):
    t = MIX_TILE
    ycn = _rms(yc_ref[...], bgc_ref[...])
    proj = (_dot(ya_ref[...].astype(BF16), woa_ref[...]) + _dot(yb_ref[...].astype(BF16), wob_ref[...])
            + _dot(ycn.astype(BF16), woc_ref[...]))
    h1 = h_ref[...] + proj
    h1_ref[...] = h1
    m = _rms(h1, ffng_ref[...])
    _store_row_tiled(m_ref, m)

    mh = m.astype(BF16)
    ml = (m - mh.astype(F32)).astype(BF16)
    wr = wr_ref[...]
    both = _dot(mh, wr)
    logits = both[:, :LANES] + both[:, LANES:] + _dot(ml, wr[:, :LANES]) + br_ref[...]
    lt = logits.T
    rowi = lax.broadcasted_iota(jnp.int32, (EXPERTS_PER_GROUP, t), 0)
    g8 = lt[0:8]
    gmax = jnp.max(g8, axis=0, keepdims=True)
    gsum = jnp.sum(jnp.exp(g8 - gmax), axis=0, keepdims=True)
    gidx = jnp.min(jnp.where(g8 == gmax, rowi, 8), axis=0, keepdims=True)
    g_w = 1.0 / gsum
    esel = jnp.zeros((EXPERTS_PER_GROUP, t), F32)
    for g in range(N_GROUPS):
        esel = jnp.where(gidx == g, lt[8 + g * EXPERTS_PER_GROUP:8 + (g + 1) * EXPERTS_PER_GROUP], esel)
    ep = jnp.exp(esel - jnp.max(esel, axis=0, keepdims=True))
    eprob = ep / jnp.sum(ep, axis=0, keepdims=True)
    v1 = jnp.max(eprob, axis=0, keepdims=True)
    i1 = jnp.min(jnp.where(eprob == v1, rowi, 8), axis=0, keepdims=True)
    rest = jnp.where(rowi == i1, -1.0, eprob)
    v2 = jnp.max(rest, axis=0, keepdims=True)
    i2 = jnp.min(jnp.where(rest == v2, rowi, 8), axis=0, keepdims=True)
    den = v1 + v2
    e1 = gidx * EXPERTS_PER_GROUP + i1
    e2 = gidx * EXPERTS_PER_GROUP + i2
    eidx_ref[...] = jnp.where(rowi == 0, e1, jnp.where(rowi == 1, e2, 0))
    wts_ref[...] = jnp.where(rowi == 0, g_w * (v1 / den), jnp.where(rowi == 1, g_w * (v2 / den), 0.0))


def _outproj_router(h, ya, yb, yc, lw):
    n = h.shape[0]
    t = MIX_TILE
    row = lambda w: pl.BlockSpec((t, w), lambda i: (i, 0))
    weights = [lw[k] for k in ("bg_c", "w_o_a", "w_o_b", "w_o_c", "ffn_g", "wr_split", "br")]
    colspec = pl.BlockSpec((8, t), lambda i: (0, i))
    return pl.pallas_call(
        _outproj_router_kernel,
        grid=(n // t,),
        in_specs=[row(D_MODEL), row(SGU_WIDTH), row(CONV_WIDTH), row(MLA_HEADS * MLA_V)]
        + [_full(w.shape) for w in weights],
        out_specs=(row(D_MODEL), pl.BlockSpec((t * ROW_TILES, LANES), lambda i: (i, 0)), colspec, colspec),
        out_shape=(
            jax.ShapeDtypeStruct((n, D_MODEL), F32),
            jax.ShapeDtypeStruct((n * ROW_TILES, LANES), F32),
            jax.ShapeDtypeStruct((8, n), jnp.int32),
            jax.ShapeDtypeStruct((8, n), F32),
        ),
        compiler_params=pltpu.CompilerParams(dimension_semantics=("arbitrary",), vmem_limit_bytes=VMEM_LIMIT),
        name="outproj_router",
    )(h, ya, yb, yc, *weights)


def _moe_rank_kernel(e_ref, upper_ref, ones_ref, ltri_ref, dest_ref, cnt_ref, run_ref, base_ref):
    phase = pl.program_id(0)
    step = pl.program_id(1)
    c = RANK_CHUNK
    expert = lax.broadcasted_iota(jnp.int32, (N_EXPERTS, c), 0)

    @pl.when(jnp.logical_and(phase == 0, step == 0))
    def _():
        run_ref[...] = jnp.zeros_like(run_ref)
        base_ref[...] = jnp.zeros_like(base_ref)

    @pl.when(jnp.logical_and(phase == 1, step == 0))
    def _():
        blocks = jnp.floor((run_ref[...] + (EXPERT_ROWS - 1)) * (1.0 / EXPERT_ROWS))
        base_ref[...] = _dot(ltri_ref[...], blocks.astype(BF16)) * EXPERT_ROWS
        run_ref[...] = jnp.zeros_like(run_ref)

    for sub in range(RANK_STEP // c):
        cols = slice(sub * c, (sub + 1) * c)
        onehot = jnp.where(expert == e_ref[:, cols], 1.0, 0.0)
        oh16 = onehot.astype(BF16)
        before = _dot(oh16, upper_ref[...])
        pos = before + run_ref[:, 0:1] + base_ref[:, 0:1]
        dest_ref[:, cols] = jnp.sum(onehot * pos, axis=0, keepdims=True).astype(jnp.int32)
        run_ref[...] = run_ref[...] + _dot(oh16, ones_ref[...])
    cnt_ref[...] = run_ref[...]


def _moe_rank(e_flat, consts):
    total = e_flat.shape[1]
    c = RANK_CHUNK
    st = RANK_STEP
    return pl.pallas_call(
        _moe_rank_kernel,
        grid=(2, total // st),
        in_specs=[pl.BlockSpec((1, st), lambda p, s: (0, s)),
                  _full((c, c)), _full((c, LANES)), _full((N_EXPERTS, N_EXPERTS))],
        out_specs=(pl.BlockSpec((1, st), lambda p, s: (0, s * p)),
                   pl.BlockSpec((N_EXPERTS, LANES), lambda p, s: (0, 0))),
        out_shape=(jax.ShapeDtypeStruct((1, total), jnp.int32),
                   jax.ShapeDtypeStruct((N_EXPERTS, LANES), F32)),
        scratch_shapes=[pltpu.VMEM((N_EXPERTS, LANES), F32), pltpu.VMEM((N_EXPERTS, LANES), F32)],
        compiler_params=pltpu.CompilerParams(dimension_semantics=("arbitrary", "arbitrary")),
        name="moe_rank",
    )(e_flat, consts["upper"], consts["ones"], consts["ltri"])


def _store_row_tiled(ref, x, offset=0):
    rows = x.shape[0]
    for s in range(ROW_TILES):
        ref[pl.ds(offset + s, rows, stride=ROW_TILES), :] = x[:, s * LANES:(s + 1) * LANES]


def _load_row_tiled(ref, rows, offset=0):
    return jnp.concatenate(
        [ref[pl.ds(offset + s, rows, stride=ROW_TILES), :] for s in range(ROW_TILES)], axis=1)


def _row_copy(src, src_row, dst, dst_row, sem):
    return pltpu.make_async_copy(
        src.at[pl.ds(pl.multiple_of(src_row * ROW_TILES, ROW_TILES), ROW_TILES)],
        dst.at[pl.ds(pl.multiple_of(dst_row * ROW_TILES, ROW_TILES), ROW_TILES)], sem)


def _rows_wait(src, dst, dst_row, rows, sem):
    pltpu.make_async_copy(
        src.at[pl.ds(0, rows * ROW_TILES)],
        dst.at[pl.ds(pl.multiple_of(dst_row * ROW_TILES, ROW_TILES), rows * ROW_TILES)], sem).wait()


def _moe_dispatch_kernel(dest_ref, cnt_ref, region_ref, m_ref, xs_ref, zbuf_ref, sem, zsem, *, n_tok):
    t = DISPATCH_TILE
    r = EXPERT_ROWS
    i = pl.program_id(0)

    @pl.when(i == 0)
    def _():
        zbuf_ref[...] = jnp.zeros_like(zbuf_ref)

        def zero_copy(e):
            last = pl.multiple_of((region_ref[e + 1] - r) * ROW_TILES, r * ROW_TILES)
            return pltpu.make_async_copy(zbuf_ref, xs_ref.at[pl.ds(last, r * ROW_TILES)], zsem)

        def start(e, carry):
            @pl.when(cnt_ref[e] > 0)
            def _():
                zero_copy(e).start()
            return carry

        def wait(e, carry):
            @pl.when(cnt_ref[e] > 0)
            def _():
                zero_copy(e).wait()
            return carry

        def tail_copy(blk):
            return pltpu.make_async_copy(
                zbuf_ref, xs_ref.at[pl.ds(pl.multiple_of(blk * (r * ROW_TILES), r * ROW_TILES), r * ROW_TILES)], zsem)

        def tail_start(blk, carry):
            tail_copy(blk).start()
            return carry

        def tail_wait(blk, carry):
            tail_copy(blk).wait()
            return carry

        first_unused = region_ref[N_EXPERTS] // r
        n_blocks = xs_ref.shape[0] // (r * ROW_TILES)
        lax.fori_loop(0, N_EXPERTS, start, 0)
        lax.fori_loop(first_unused, n_blocks, tail_start, 0)
        lax.fori_loop(0, N_EXPERTS, wait, 0)
        lax.fori_loop(first_unused, n_blocks, tail_wait, 0)

    per_iter = DMA_UNROLL // TOP_K

    def issue(g, carry):
        for u in range(per_iter):
            row = g * per_iter + u
            for k in range(TOP_K):
                _row_copy(m_ref, row, xs_ref, dest_ref[k * n_tok + i * t + row], sem).start(priority=k % 2)
        return carry

    lax.fori_loop(0, t // per_iter, issue, 0)
    for k in range(TOP_K):
        pltpu.make_async_copy(m_ref, xs_ref.at[pl.ds(0, t * ROW_TILES)], sem).wait()


def _moe_dispatch(dest, cnt, region, m_rt, rows):
    n = m_rt.shape[0] // ROW_TILES
    t = DISPATCH_TILE
    return pl.pallas_call(
        functools.partial(_moe_dispatch_kernel, n_tok=n),
        grid_spec=pltpu.PrefetchScalarGridSpec(
            num_scalar_prefetch=3,
            grid=(n // t,),
            in_specs=[pl.BlockSpec((t * ROW_TILES, LANES), lambda i, d, c, rg: (i, 0))],
            out_specs=pl.BlockSpec(memory_space=pl.ANY),
            scratch_shapes=[pltpu.VMEM((EXPERT_ROWS * ROW_TILES, LANES), F32),
                            pltpu.SemaphoreType.DMA(()), pltpu.SemaphoreType.DMA(())],
        ),
        out_shape=jax.ShapeDtypeStruct((rows * ROW_TILES, LANES), F32),
        compiler_params=pltpu.CompilerParams(dimension_semantics=("arbitrary",)),
        name="moe_dispatch",
    )(dest, cnt, region, m_rt)


def _moe_experts_kernel(blk_e_ref, nact_ref, x_ref, wg_ref, wu_ref, wd_ref, y_ref, wgu_ref, wdn_ref):
    r = EXPERT_ROWS
    b = pl.program_id(0)
    live = b < nact_ref[0]

    @pl.when(live)
    def _():
        @pl.when(jnp.logical_or(b == 0, blk_e_ref[b] != blk_e_ref[jnp.maximum(b - 1, 0)]))
        def _():
            wgu_ref[:, :D_EXPERT] = wg_ref[0, 0].astype(BF16)
            wgu_ref[:, D_EXPERT:] = wu_ref[0, 0].astype(BF16)
            wdn_ref[...] = wd_ref[0, 0].astype(BF16)

        x = _load_row_tiled(x_ref, r).astype(BF16)
        gu = _dot(x, wgu_ref[...])
        hb = jax.nn.silu(gu[:, :D_EXPERT]) * gu[:, D_EXPERT:]
        _store_row_tiled(y_ref, _dot(hb.astype(BF16), wdn_ref[...]))

    @pl.when(jnp.logical_not(live))
    def _():
        y_ref[...] = jnp.zeros_like(y_ref)


def _moe_experts(blk_e, nact, xs_rt, layer, w_gate, w_up, w_down):
    rows = xs_rt.shape[0] // ROW_TILES
    r = EXPERT_ROWS
    wspec = lambda k, n: pl.BlockSpec((1, 1, k, n), lambda b, be, na: (layer, be[b], 0, 0))
    return pl.pallas_call(
        _moe_experts_kernel,
        grid_spec=pltpu.PrefetchScalarGridSpec(
            num_scalar_prefetch=2,
            grid=(rows // r,),
            in_specs=[pl.BlockSpec((r * ROW_TILES, LANES), lambda b, be, na: (jnp.minimum(b, na[0] - 1), 0)),
                      wspec(D_MODEL, D_EXPERT), wspec(D_MODEL, D_EXPERT), wspec(D_EXPERT, D_MODEL)],
            out_specs=pl.BlockSpec((r * ROW_TILES, LANES), lambda b, be, na: (b, 0)),
            scratch_shapes=[pltpu.VMEM((D_MODEL, 2 * D_EXPERT), BF16),
                            pltpu.VMEM((D_EXPERT, D_MODEL), BF16)],
        ),
        out_shape=jax.ShapeDtypeStruct((rows * ROW_TILES, LANES), F32),
        compiler_params=pltpu.CompilerParams(dimension_semantics=("arbitrary",), vmem_limit_bytes=VMEM_LIMIT),
        name="moe_experts",
    )(blk_e, nact, xs_rt, w_gate, w_up, w_down)


def _combine_ple_kernel(dest_ref, h1_ref, wt_ref, p_ref, y_ref, pleg_ref, gatew_ref, projw_ref, postg_ref,
                        out_ref, ybuf_ref, sem, *, n_tok):
    t = COMBINE_TILE
    i = pl.program_id(0)

    def gather(step, slot):
        def issue(g, carry):
            for u in range(DMA_UNROLL // TOP_K):
                r = g * (DMA_UNROLL // TOP_K) + u
                for k in range(TOP_K):
                    _row_copy(y_ref, dest_ref[k * n_tok + step * t + r], ybuf_ref, (slot * TOP_K + k) * t + r,
                              sem.at[slot]).start(priority=k % 2)
            return carry
        lax.fori_loop(0, t // (DMA_UNROLL // TOP_K), issue, 0)

    @pl.when(i == 0)
    def _():
        gather(0, 0)

    @pl.when(i + 1 < pl.num_programs(0))
    def _():
        gather(i + 1, (i + 1) % 2)

    e = _rms(_dot(p_ref[...].astype(BF16), projw_ref[...]), postg_ref[...])
    slot = i % 2
    _rows_wait(y_ref, ybuf_ref, slot * TOP_K * t, TOP_K * t, sem.at[slot])
    wt = wt_ref[...].T
    y0 = _load_row_tiled(ybuf_ref, t, (slot * TOP_K) * (t * ROW_TILES))
    y1 = _load_row_tiled(ybuf_ref, t, (slot * TOP_K + 1) * (t * ROW_TILES))
    h2 = h1_ref[...] + (wt[:, 0:1] * y0 + wt[:, 1:2] * y1)
    gate = jax.nn.sigmoid(_dot(_rms(h2, pleg_ref[...]).astype(BF16), gatew_ref[...]))
    out_ref[...] = h2 + gate * e


def _combine_ple(dest, h1, wt, p, y_rt, lw):
    n = h1.shape[0]
    t = COMBINE_TILE
    row = lambda w: pl.BlockSpec((t, w), lambda i, d: (i, 0))
    layer_rows = lw["layer"] * (n // t)
    weights = [lw[k] for k in ("ple_g", "gate_w", "proj_w", "post_g")]
    return pl.pallas_call(
        functools.partial(_combine_ple_kernel, n_tok=n),
        grid_spec=pltpu.PrefetchScalarGridSpec(
            num_scalar_prefetch=1,
            grid=(n // t,),
            in_specs=[row(D_MODEL), pl.BlockSpec((SUBLANES, t), lambda i, d: (0, i)), pl.BlockSpec((t, p.shape[1]), lambda i, d: (layer_rows + i, 0)),
                      pl.BlockSpec(memory_space=pl.ANY)]
            + [pl.BlockSpec(w.shape, lambda i, d, nd=w.ndim: (0,) * nd) for w in weights],
            out_specs=row(D_MODEL),
            scratch_shapes=[pltpu.VMEM((2 * TOP_K * t * ROW_TILES, LANES), F32), pltpu.SemaphoreType.DMA((2,))],
        ),
        out_shape=jax.ShapeDtypeStruct((n, D_MODEL), F32),
        compiler_params=pltpu.CompilerParams(dimension_semantics=("arbitrary",), vmem_limit_bytes=VMEM_LIMIT),
        name="combine_ple",
    )(dest, h1, wt, p, y_rt, *weights)


def _segment_matrix(seg_ids):
    seg_ids = jnp.asarray(seg_ids)
    same = (seg_ids[:, None] == seg_ids[None, :]).astype(F32)
    return (same / jnp.sum(same, axis=1, keepdims=True)).astype(BF16)


def _constants():
    lane = jnp.arange(256)
    qk_seg = (lane // HEAD_PAD) * 3 + jnp.where(lane % HEAD_PAD < MLA_NOPE, 0, jnp.where(lane % HEAD_PAD < MLA_QK, 1, 2))
    i = jnp.arange(RANK_CHUNK)
    e = jnp.arange(N_EXPERTS)
    return {
        "g64": _segment_matrix(lane // HEAD_DIM),
        "gqk": _segment_matrix(qk_seg),
        "upper": (i[:, None] < i[None, :]).astype(BF16),
        "ones": jnp.ones((RANK_CHUNK, LANES), BF16),
        "ltri": (e[None, :] < e[:, None]).astype(BF16),
    }


def _layer_weights(i, consts, p):
    row = lambda v: v.reshape(1, -1).astype(F32)
    w_in = p["w_in"][i]
    o_sgu, o_conv, o_q, o_kv = 2 * SGU_WIDTH, 2 * SGU_WIDTH + 2 * CONV_WIDTH, 0, 0
    o_q = o_conv + Q_RANK
    o_kv = o_q + KV_RANK
    w_ckv = jnp.concatenate([w_in[:, o_q:], jnp.zeros((D_MODEL, 256 - KV_RANK - MLA_ROPE), F32)], axis=1)

    causal = jnp.tril(jnp.ones((CHUNK, CHUNK), F32))
    wcat = jnp.transpose(p["sgu_w"][i] * causal, (1, 0, 2)).reshape(CHUNK, SGU_HEADS * CHUNK)
    sbias = jnp.broadcast_to(jnp.transpose(p["sgu_b"][i])[:, :, None],
                             (CHUNK, SGU_HEADS, HEAD_DIM)).reshape(CHUNK, SGU_WIDTH)

    w_uq = p["w_uq"][i].reshape(Q_RANK, MLA_HEADS, MLA_QK)
    w_uq = jnp.pad(w_uq, ((0, 0), (0, 0), (0, HEAD_PAD - MLA_QK))).reshape(Q_RANK, MLA_HEADS * HEAD_PAD)
    qn_g = p["q_norm_g"][i]
    q_gain = jnp.tile(jnp.concatenate([qn_g, jnp.zeros((HEAD_PAD - MLA_QK,), F32)]), MLA_HEADS)
    w_ukv = p["w_ukv"][i].reshape(KV_RANK, MLA_HEADS, MLA_NOPE + MLA_V)
    w_ukt = jnp.transpose(w_ukv[:, :, :MLA_NOPE], (1, 2, 0)).reshape(MLA_HEADS * MLA_NOPE, KV_RANK)
    w_v = w_ukv[:, :, MLA_NOPE:].reshape(KV_RANK, MLA_HEADS // 2, 2, MLA_V)
    zeros_v = jnp.zeros((KV_RANK, MLA_HEADS // 2, MLA_V), F32)
    w_uv = jnp.stack([w_v[:, :, 0], zeros_v, zeros_v, w_v[:, :, 1]], axis=2).reshape(KV_RANK, MLA_HEADS * HEAD_PAD)
    v_ones = jnp.tile(jnp.repeat(jnp.array([0.0, 1.0, 1.0, 0.0], F32), MLA_V), MLA_HEADS // 2)
    kn_g = p["k_norm_g"][i]

    bg = p["branch_norm_g"][i]
    w_o = p["w_o"][i]
    wr = jnp.concatenate([p["router_group_w"][i], jnp.zeros((D_MODEL, 8 - N_GROUPS), F32),
                          p["router_expert_w"][i], jnp.zeros((D_MODEL, LANES - 8 - N_EXPERTS), F32)], axis=1)
    wr_hi = wr.astype(BF16)
    br = jnp.concatenate([p["router_group_b"][i], jnp.full((8 - N_GROUPS,), NEG_BIG, F32),
                          p["router_expert_b"][i], jnp.zeros((LANES - 8 - N_EXPERTS,), F32)])
    return {
        "mix_g": row(p["mix_norm_g"][i]),
        "w_sgu": w_in[:, :o_sgu].astype(BF16),
        "w_conv": w_in[:, o_sgu:o_conv].astype(BF16),
        "w_cq": w_in[:, o_conv:o_q].astype(BF16),
        "w_ckv": w_ckv.astype(BF16),
        "sgu_ln_g": row(p["sgu_ln_g"][i]), "sgu_ln_b": row(p["sgu_ln_b"][i]),
        "sgu_wcat": wcat.astype(BF16), "sgu_bias": sbias, "g64": consts["g64"], "bg_a": row(bg[:SGU_WIDTH]),
        "conv_w": jnp.pad(p["conv_w"][i], ((0, 1), (0, 0))), "conv_b": row(p["conv_b"][i]),
        "conv_ln_g": row(p["conv_ln_g"][i]), "conv_ln_b": row(p["conv_ln_b"][i]),
        "pw_w": p["conv_pw_w"][i].astype(BF16), "pw_b": row(p["conv_pw_b"][i]),
        "bg_b": row(bg[SGU_WIDTH:SGU_WIDTH + CONV_WIDTH]),
        "qa_g": row(p["q_a_norm_g"][i]), "w_uq": w_uq.astype(BF16), "gqk": consts["gqk"],
        "q_gain": row(q_gain),
        "kva_g": row(p["kv_a_norm_g"][i]), "w_ukt": w_ukt.astype(BF16), "w_uv": w_uv.astype(BF16),
        "v_ones": row(v_ones),
        "kn_g": jnp.broadcast_to(kn_g[:MLA_NOPE, None], (MLA_NOPE, MIX_TILE)),
        "kpe_g": jnp.broadcast_to(kn_g[MLA_NOPE:, None], (MLA_ROPE, MIX_TILE)),
        "bg_c": row(bg[SGU_WIDTH + CONV_WIDTH:]),
        "w_o_a": w_o[:SGU_WIDTH].astype(BF16),
        "w_o_b": w_o[SGU_WIDTH:SGU_WIDTH + CONV_WIDTH].astype(BF16),
        "w_o_c": w_o[SGU_WIDTH + CONV_WIDTH:].astype(BF16),
        "ffn_g": row(p["ffn_norm_g"][i]),
        "wr_split": jnp.concatenate([wr_hi, (wr - wr_hi.astype(F32)).astype(BF16)], axis=1), "br": row(br),
        "layer": i, "w_gate": p["moe_w_gate"], "w_up": p["moe_w_up"], "w_down": p["moe_w_down"],
        "ple_g": row(p["ple_norm_g"][i]), "gate_w": p["ple_gate_w"][i].astype(BF16),
        "proj_w": p["ple_proj_w"][i].astype(BF16), "post_g": row(p["ple_post_norm_g"][i]),
    }


def _moe(h1, m, eidx, wts, pl_i, lw, consts):
    n = h1.shape[0]
    total = TOP_K * n
    rows = total + N_EXPERTS * EXPERT_ROWS
    dest2d, counts = _moe_rank(eidx[:TOP_K].reshape(1, total), consts)
    dest = dest2d.reshape(total)
    cnt = counts[:, 0].astype(jnp.int32)
    padded = (cnt + EXPERT_ROWS - 1) // EXPERT_ROWS * EXPERT_ROWS
    pend = jnp.cumsum(padded)
    nblk = rows // EXPERT_ROWS
    blk_start = jnp.arange(nblk, dtype=jnp.int32) * EXPERT_ROWS
    blk_e = jnp.minimum(jnp.sum((pend[None, :] <= blk_start[:, None]).astype(jnp.int32), axis=1), N_EXPERTS - 1)
    nact = (pend[-1:] // EXPERT_ROWS).astype(jnp.int32)
    region = jnp.concatenate([jnp.zeros((1,), jnp.int32), pend.astype(jnp.int32)])
    xs = _moe_dispatch(dest, cnt, region, m, rows)
    y = _moe_experts(blk_e, nact, xs, lw["layer"], lw["w_gate"], lw["w_up"], lw["w_down"])
    return _combine_ple(dest, h1, wts, pl_i, y, lw)


def kernel(x, p, positions, mix_norm_g, w_in, sgu_ln_g, sgu_ln_b, sgu_w, sgu_b, conv_w, conv_b, conv_ln_g, conv_ln_b, conv_pw_w, conv_pw_b, q_a_norm_g, w_uq, kv_a_norm_g, w_ukv, q_norm_g, k_norm_g, branch_norm_g, w_o, ffn_norm_g, router_group_w, router_group_b, router_expert_w, router_expert_b, moe_w_gate, moe_w_up, moe_w_down, ple_norm_g, ple_gate_w, ple_proj_w, ple_post_norm_g):
    params = dict(
        mix_norm_g=mix_norm_g, w_in=w_in, sgu_ln_g=sgu_ln_g, sgu_ln_b=sgu_ln_b, sgu_w=sgu_w, sgu_b=sgu_b,
        conv_w=conv_w, conv_b=conv_b, conv_ln_g=conv_ln_g, conv_ln_b=conv_ln_b, conv_pw_w=conv_pw_w,
        conv_pw_b=conv_pw_b, q_a_norm_g=q_a_norm_g, w_uq=w_uq, kv_a_norm_g=kv_a_norm_g, w_ukv=w_ukv,
        q_norm_g=q_norm_g, k_norm_g=k_norm_g, branch_norm_g=branch_norm_g, w_o=w_o, ffn_norm_g=ffn_norm_g,
        router_group_w=router_group_w, router_group_b=router_group_b, router_expert_w=router_expert_w,
        router_expert_b=router_expert_b, moe_w_gate=moe_w_gate, moe_w_up=moe_w_up, moe_w_down=moe_w_down,
        ple_norm_g=ple_norm_g, ple_gate_w=ple_gate_w, ple_proj_w=ple_proj_w, ple_post_norm_g=ple_post_norm_g)
    batch, seq, d = x.shape
    n = batch * seq
    depth = w_in.shape[0]
    consts = _constants()
    cos, sin = _rope_tables(positions)
    h = x.reshape(n, d)
    p_rows = p.reshape(depth * n, p.shape[-1])
    for i in range(depth):
        lw = _layer_weights(i, consts, params)
        ya, yb, q, kt, v = _mixer_pre(h, cos, sin, lw, batch, seq)
        yc = _attention(q, kt, v, batch, seq)
        h1, m, eidx, wts = _outproj_router(h, ya, yb, yc, lw)
        h = _moe(h1, m, eidx, wts, p_rows, lw, consts)
    return h.reshape(batch, seq, d)
```

```python
import functools
import math

import jax
import jax.numpy as jnp
from jax import lax
from jax.experimental import pallas as pl
from jax.experimental.pallas import tpu as pltpu

F32 = jnp.float32
BF16 = jnp.bfloat16

D_MODEL = 1024
HEAD_DIM = 64
SGU_HEADS = 4
SGU_WIDTH = 256
CHUNK = 128
CONV_WIDTH = 256
CONV_KERNEL = 31
MLA_HEADS = 8
MLA_NOPE = 64
MLA_ROPE = 32
MLA_QK = MLA_NOPE + MLA_ROPE
MLA_V = 64
Q_RANK = 256
KV_RANK = 128
ROPE_THETA = 10000.0
N_GROUPS = 4
EXPERTS_PER_GROUP = 8
N_EXPERTS = 32
D_EXPERT = 256
TOP_K = 2
EPS = 1e-6

LANES = 128
SUBLANES = 8
HEAD_PAD = 128
MIX_TILE = 512
CONV_HALO = 32
CONV_SHIFTED_ROWS = MIX_TILE + CONV_HALO - SUBLANES
MIXER_ORDER = ("q", "kv", "sgu", "conv")
ATT_TQ = 256
ATT_LOOKAHEAD = 1
RANK_CHUNK = 512
RANK_STEP = 2048
EXPERT_ROWS = 512
DISPATCH_TILE = 256
COMBINE_TILE = 256
ROW_TILES = D_MODEL // LANES
DMA_UNROLL = 8
VMEM_LIMIT = 48 * 1024 * 1024
NEG_BIG = -1e30


def _dot(a, b):
    return jnp.dot(a, b, preferred_element_type=F32)


def _rms(x, g):
    ms = jnp.mean(x * x, axis=-1, keepdims=True)
    return x * lax.rsqrt(ms + EPS) * g


def _seg_mean(x, g_ref, split=True):
    g = g_ref[...]
    outs = []
    for c in range(x.shape[1] // 256):
        xb = x[:, c * 256:(c + 1) * 256]
        hi = xb.astype(BF16)
        acc = _dot(hi, g)
        if split:
            acc = acc + _dot((xb - hi.astype(F32)).astype(BF16), g)
        outs.append(acc)
    return outs[0] if len(outs) == 1 else jnp.concatenate(outs, axis=-1)


def _group_layernorm(x, g_ref, gain, bias):
    d = x - _seg_mean(x, g_ref)
    var = _seg_mean(d * d, g_ref)
    return d * lax.rsqrt(var + EPS) * gain + bias


def _rope_table_kernel(pos_ref, invf_ref, cos_ref, sin_ref):
    ang = pos_ref[...].astype(F32) * invf_ref[...]
    cos_ref[...] = jnp.cos(ang)
    sin_ref[...] = jnp.sin(ang)


def _rope_tables(positions):
    b, s = positions.shape
    n = b * s
    half = MLA_ROPE // 2
    inv_freq = ROPE_THETA ** (-jnp.arange(0, MLA_ROPE, 2, dtype=F32) / MLA_ROPE)
    pos_rep = jnp.broadcast_to(positions.reshape(n, 1), (n, half)).reshape(n * half // LANES, LANES)
    invf_rep = jnp.tile(inv_freq, LANES // half).reshape(1, LANES)
    shape = jax.ShapeDtypeStruct(pos_rep.shape, F32)
    cos, sin = pl.pallas_call(_rope_table_kernel, out_shape=(shape, shape), name="rope_tables")(pos_rep, invf_rep)
    return cos.reshape(n, half), sin.reshape(n, half)


def _mixer_pre_kernel(
        h_ref, cos_ref, sin_ref,
        mixg_ref, wsgu_ref, wconv_ref, wcq_ref, wckv_ref,
        slng_ref, slnb_ref, wcat_ref, sbias_ref, g64_ref, bga_ref,
        cw_ref, cb_ref, clng_ref, clnb_ref, pww_ref, pwb_ref, bgb_ref,
        qag_ref, wuq_ref, gqk_ref, qgain_ref,
        kvag_ref, wukt_ref, wuv_ref, vones_ref, kng_ref, kpeg_ref,
        ya_ref, yb_ref, q_ref, kt_ref, v_ref,
        ybuf_ref, ysh_ref, *, tiles_per_seq):
    t = MIX_TILE
    a = _rms(h_ref[...], mixg_ref[...]).astype(BF16)
    zs = _dot(a, wsgu_ref[...])
    zc = _dot(a, wconv_ref[...])
    zq = _dot(a, wcq_ref[...])
    zkv = _dot(a, wckv_ref[...])

    half = MLA_ROPE // 2
    cos, sin = cos_ref[...], sin_ref[...]
    zn = jnp.zeros((t, MLA_NOPE), F32)
    zh = jnp.zeros((t, half), F32)
    zp = jnp.zeros((t, HEAD_PAD - MLA_QK), F32)
    rc = jnp.concatenate([jnp.ones((t, MLA_NOPE), F32), cos, cos, zp], axis=1)
    rs1 = jnp.concatenate([zn, -sin, zh, zp], axis=1)
    rs2 = jnp.concatenate([zn, zh, sin, zp], axis=1)

    def sgu_branch():
        zg = jax.nn.gelu(zs)
        u = zg[:, :SGU_WIDTH]
        vn = _group_layernorm(zg[:, SGU_WIDTH:], g64_ref, slng_ref[...], slnb_ref[...])
        lane_head = lax.broadcasted_iota(jnp.int32, (CHUNK, SGU_WIDTH), 1) // HEAD_DIM
        wcat = wcat_ref[...]
        sbias = sbias_ref[...]
        parts = []
        for c in range(t // CHUNK):
            vc = vn[c * CHUNK:(c + 1) * CHUNK]
            stacked = jnp.concatenate(
                [jnp.where(lane_head == hh, vc, 0.0) for hh in range(SGU_HEADS)], axis=0).astype(BF16)
            s = _dot(wcat, stacked) + sbias
            parts.append(u[c * CHUNK:(c + 1) * CHUNK] * s)
        ya_ref[...] = _rms(jnp.concatenate(parts, axis=0), bga_ref[...])

    def conv_branch():
        yg = zc[:, :CONV_WIDTH] * jax.nn.sigmoid(zc[:, CONV_WIDTH:])
        first = (pl.program_id(0) % tiles_per_seq) == 0

        @pl.when(first)
        def _():
            ybuf_ref[0:CONV_HALO, :] = jnp.zeros((CONV_HALO, CONV_WIDTH), F32)

        @pl.when(jnp.logical_not(first))
        def _():
            ybuf_ref[0:CONV_HALO, :] = ybuf_ref[t:t + CONV_HALO, :]

        ybuf_ref[CONV_HALO:CONV_HALO + t, :] = yg
        for sh in range(1, SUBLANES):
            ysh_ref[sh - 1] = ybuf_ref[pl.ds(sh, CONV_SHIFTED_ROWS), :]
        rows = 64
        first_tap_row = CONV_HALO - (CONV_KERNEL - 1)
        conv_parts = []
        for r in range(t // rows):
            acc = jnp.broadcast_to(cb_ref[...], (rows, CONV_WIDTH))
            for tap in range(CONV_KERNEL):
                sh = (first_tap_row + tap) % SUBLANES
                start = r * rows + first_tap_row + tap - sh
                src = ybuf_ref[pl.ds(start, rows), :] if sh == 0 else ysh_ref[sh - 1, pl.ds(start, rows), :]
                acc = acc + cw_ref[tap:tap + 1, :] * src
            conv_parts.append(acc)
        cv = jnp.concatenate(conv_parts, axis=0)
        cn = _group_layernorm(cv, g64_ref, clng_ref[...], clnb_ref[...])
        yb = _dot(jax.nn.silu(cn).astype(BF16), pww_ref[...]) + pwb_ref[...]
        yb_ref[...] = _rms(yb, bgb_ref[...])

    def query_branch():
        cqn = _rms(zq, qag_ref[...]).astype(BF16)
        qf = _dot(cqn, wuq_ref[...])
        qn = qf * lax.rsqrt(_seg_mean(qf * qf, gqk_ref, split=False) + EPS) * qgain_ref[...]
        for hh in range(MLA_HEADS):
            blk = qn[:, hh * HEAD_PAD:(hh + 1) * HEAD_PAD]
            rot = blk * rc + pltpu.roll(blk, HEAD_PAD - half, 1) * rs1 + pltpu.roll(blk, half, 1) * rs2
            q_ref[:, hh * HEAD_PAD:(hh + 1) * HEAD_PAD] = rot.astype(BF16)

    def key_value_branch():
        ckvn = _rms(zkv[:, :KV_RANK], kvag_ref[...]).astype(BF16)
        v_ref[...] = (_dot(ckvn, wuv_ref[...]) + vones_ref[...]).astype(BF16)
        knt = lax.dot_general(wukt_ref[...], ckvn, (((1,), (1,)), ((), ())), preferred_element_type=F32)
        x = zkv[:, KV_RANK:].T[0:MLA_ROPE]
        xn = x * lax.rsqrt(jnp.mean(x * x, axis=0, keepdims=True) + EPS) * kpeg_ref[...]
        x1, x2 = xn[:MLA_ROPE // 2], xn[MLA_ROPE // 2:]
        cos_t = rc.T[MLA_NOPE:MLA_NOPE + half]
        sin_t = rs2.T[MLA_NOPE + half:MLA_QK]
        kpe = jnp.concatenate([x1 * cos_t - x2 * sin_t, x2 * cos_t + x1 * sin_t], axis=0)
        pad = jnp.zeros((HEAD_PAD - MLA_QK, t), F32)
        kng = kng_ref[...]
        for hh in range(MLA_HEADS):
            blk = knt[hh * MLA_NOPE:(hh + 1) * MLA_NOPE]
            kn = blk * lax.rsqrt(jnp.mean(blk * blk, axis=0, keepdims=True) + EPS) * kng
            kt_ref[0, hh * HEAD_PAD:(hh + 1) * HEAD_PAD, :] = jnp.concatenate([kn, kpe, pad], axis=0).astype(BF16)

    branches = {"sgu": sgu_branch, "conv": conv_branch, "q": query_branch, "kv": key_value_branch}
    for name in MIXER_ORDER:
        branches[name]()


def _full(shape):
    nd = len(shape)
    return pl.BlockSpec(shape, lambda *_: (0,) * nd)


def _mixer_pre(h, cos, sin, lw, batch, seq):
    n = h.shape[0]
    t = MIX_TILE
    tps = seq // t
    row = lambda w: pl.BlockSpec((t, w), lambda i: (i, 0))
    weights = [lw[k] for k in (
        "mix_g", "w_sgu", "w_conv", "w_cq", "w_ckv",
        "sgu_ln_g", "sgu_ln_b", "sgu_wcat", "sgu_bias", "g64", "bg_a",
        "conv_w", "conv_b", "conv_ln_g", "conv_ln_b", "pw_w", "pw_b", "bg_b",
        "qa_g", "w_uq", "gqk", "q_gain",
        "kva_g", "w_ukt", "w_uv", "v_ones", "kn_g", "kpe_g")]
    in_specs = [row(D_MODEL), row(MLA_ROPE // 2), row(MLA_ROPE // 2)] + [_full(w.shape) for w in weights]
    out_shape = (
        jax.ShapeDtypeStruct((n, SGU_WIDTH), F32),
        jax.ShapeDtypeStruct((n, CONV_WIDTH), F32),
        jax.ShapeDtypeStruct((n, MLA_HEADS * HEAD_PAD), BF16),
        jax.ShapeDtypeStruct((batch, MLA_HEADS * HEAD_PAD, seq), BF16),
        jax.ShapeDtypeStruct((n, MLA_HEADS * HEAD_PAD), BF16),
    )
    out_specs = (
        row(SGU_WIDTH), row(CONV_WIDTH), row(MLA_HEADS * HEAD_PAD),
        pl.BlockSpec((1, MLA_HEADS * HEAD_PAD, t), lambda i: (i // tps, 0, i % tps)),
        row(MLA_HEADS * HEAD_PAD),
    )
    return pl.pallas_call(
        functools.partial(_mixer_pre_kernel, tiles_per_seq=tps),
        grid=(n // t,),
        in_specs=in_specs,
        out_specs=out_specs,
        out_shape=out_shape,
        scratch_shapes=[pltpu.VMEM((t + CONV_HALO, CONV_WIDTH), F32),
                        pltpu.VMEM((SUBLANES - 1, CONV_SHIFTED_ROWS, CONV_WIDTH), F32)],
        compiler_params=pltpu.CompilerParams(dimension_semantics=("arbitrary",), vmem_limit_bytes=VMEM_LIMIT),
        name="mixer_pre",
    )(h, cos, sin, *weights)


def _attention_kernel(q_ref, kt_ref, v_ref, o_ref, *, seq):
    tq = ATT_TQ
    exp2_scale = MLA_QK ** -0.5 * math.log2(math.e)
    row = lax.broadcasted_iota(jnp.int32, (tq, tq), 0)
    col = lax.broadcasted_iota(jnp.int32, (tq, tq), 1)
    lane = lax.broadcasted_iota(jnp.int32, (tq, HEAD_PAD), 1)
    def scores(qi, hh):
        nk = (qi + 1) * tq
        q = q_ref[qi * tq:(qi + 1) * tq, hh * HEAD_PAD:(hh + 1) * HEAD_PAD]
        s = _dot(q, kt_ref[0, hh * HEAD_PAD:(hh + 1) * HEAD_PAD, 0:nk])
        diag = jnp.where(col <= row, s[:, nk - tq:], NEG_BIG)
        return diag if qi == 0 else jnp.concatenate([s[:, :nk - tq], diag], axis=1)

    n_blocks = seq // tq
    ahead = [scores(0, hh) for hh in range(2)]
    for qi in range(n_blocks):
        s_pair, ahead, p_pair = ahead, [], []
        for hh in range(2):
            if qi + 1 < n_blocks:
                ahead.append(scores(qi + 1, hh))
            s = s_pair[hh]
            p_pair.append(jnp.exp2(((s - jnp.max(s, axis=-1, keepdims=True)) * exp2_scale).astype(BF16)))
        acc = _dot(jnp.concatenate(p_pair, axis=0), v_ref[0:(qi + 1) * tq, :])
        a0, a1 = acc[:tq, :HEAD_PAD], acc[tq:, HEAD_PAD:]
        o_ref[qi * tq:(qi + 1) * tq, :] = jnp.where(
            lane < MLA_V, a0 / pltpu.roll(a0, MLA_V, 1), a1 / pltpu.roll(a1, MLA_V, 1))


def _attention(q, kt, v, batch, seq):
    n = q.shape[0]
    pairs = MLA_HEADS // 2
    return pl.pallas_call(
        functools.partial(_attention_kernel, seq=seq),
        grid=(batch, pairs),
        in_specs=[
            pl.BlockSpec((seq, 2 * HEAD_PAD), lambda b, p: (b, p)),
            pl.BlockSpec((1, 2 * HEAD_PAD, seq), lambda b, p: (b, p, 0)),
            pl.BlockSpec((seq, 2 * HEAD_PAD), lambda b, p: (b, p)),
        ],
        out_specs=pl.BlockSpec((seq, 2 * MLA_V), lambda b, p: (b, p)),
        out_shape=jax.ShapeDtypeStruct((n, MLA_HEADS * MLA_V), F32),
        compiler_params=pltpu.CompilerParams(
            dimension_semantics=("arbitrary", "arbitrary"), vmem_limit_bytes=VMEM_LIMIT),
        name="attention",
    )(q, kt, v)


def _outproj_router_kernel(h_ref, ya_ref, yb_ref, yc_ref, bgc_ref, woa_ref, wob_ref, woc_ref,
                           ffng_ref, wr_ref, br_ref,
                           h1_ref, m_ref, eidx_ref, wts_ref):
    t = MIX_TILE
    ycn = _rms(yc_ref[...], bgc_ref[...])
    proj = (_dot(ya_ref[...].astype(BF16), woa_ref[...]) + _dot(yb_ref[...].astype(BF16), wob_ref[...])
            + _dot(ycn.astype(BF16), woc_ref[...]))
    h1 = h_ref[...] + proj
    h1_ref[...] = h1
    m = _rms(h1, ffng_ref[...])
    _store_row_tiled(m_ref, m)

    mh = m.astype(BF16)
    ml = (m - mh.astype(F32)).astype(BF16)
    wr = wr_ref[...]
    both = _dot(mh, wr)
    logits = both[:, :LANES] + both[:, LANES:] + _dot(ml, wr[:, :LANES]) + br_ref[...]
    lt = logits.T
    rowi = lax.broadcasted_iota(jnp.int32, (EXPERTS_PER_GROUP, t), 0)
    g8 = lt[0:8]
    gmax = jnp.max(g8, axis=0, keepdims=True)
    gsum = jnp.sum(jnp.exp(g8 - gmax), axis=0, keepdims=True)
    gidx = jnp.min(jnp.where(g8 == gmax, rowi, 8), axis=0, keepdims=True)
    g_w = 1.0 / gsum
    esel = jnp.zeros((EXPERTS_PER_GROUP, t), F32)
    for g in range(N_GROUPS):
        esel = jnp.where(gidx == g, lt[8 + g * EXPERTS_PER_GROUP:8 + (g + 1) * EXPERTS_PER_GROUP], esel)
    ep = jnp.exp(esel - jnp.max(esel, axis=0, keepdims=True))
    eprob = ep / jnp.sum(ep, axis=0, keepdims=True)
    v1 = jnp.max(eprob, axis=0, keepdims=True)
    i1 = jnp.min(jnp.where(eprob == v1, rowi, 8), axis=0, keepdims=True)
    rest = jnp.where(rowi == i1, -1.0, eprob)
    v2 = jnp.max(rest, axis=0, keepdims=True)
    i2 = jnp.min(jnp.where(rest == v2, rowi, 8), axis=0, keepdims=True)
    den = v1 + v2
    e1 = gidx * EXPERTS_PER_GROUP + i1
    e2 = gidx * EXPERTS_PER_GROUP + i2
    eidx_ref[...] = jnp.where(rowi == 0, e1, jnp.where(rowi == 1, e2, 0))
    wts_ref[...] = jnp.where(rowi == 0, g_w * (v1 / den), jnp.where(rowi == 1, g_w * (v2 / den), 0.0))


def _outproj_router(h, ya, yb, yc, lw):
    n = h.shape[0]
    t = MIX_TILE
    row = lambda w: pl.BlockSpec((t, w), lambda i: (i, 0))
    weights = [lw[k] for k in ("bg_c", "w_o_a", "w_o_b", "w_o_c", "ffn_g", "wr_split", "br")]
    colspec = pl.BlockSpec((8, t), lambda i: (0, i))
    return pl.pallas_call(
        _outproj_router_kernel,
        grid=(n // t,),
        in_specs=[row(D_MODEL), row(SGU_WIDTH), row(CONV_WIDTH), row(MLA_HEADS * MLA_V)]
        + [_full(w.shape) for w in weights],
        out_specs=(row(D_MODEL), pl.BlockSpec((t * ROW_TILES, LANES), lambda i: (i, 0)), colspec, colspec),
        out_shape=(
            jax.ShapeDtypeStruct((n, D_MODEL), F32),
            jax.ShapeDtypeStruct((n * ROW_TILES, LANES), F32),
            jax.ShapeDtypeStruct((8, n), jnp.int32),
            jax.ShapeDtypeStruct((8, n), F32),
        ),
        compiler_params=pltpu.CompilerParams(dimension_semantics=("arbitrary",), vmem_limit_bytes=VMEM_LIMIT),
        name="outproj_router",
    )(h, ya, yb, yc, *weights)


def _moe_rank_kernel(e_ref, upper_ref, ones_ref, ltri_ref, dest_ref, cnt_ref, run_ref, base_ref):
    phase = pl.program_id(0)
    step = pl.program_id(1)
    c = RANK_CHUNK
    expert = lax.broadcasted_iota(jnp.int32, (N_EXPERTS, c), 0)

    @pl.when(jnp.logical_and(phase == 0, step == 0))
    def _():
        run_ref[...] = jnp.zeros_like(run_ref)
        base_ref[...] = jnp.zeros_like(base_ref)

    @pl.when(jnp.logical_and(phase == 1, step == 0))
    def _():
        blocks = jnp.floor((run_ref[...] + (EXPERT_ROWS - 1)) * (1.0 / EXPERT_ROWS))
        base_ref[...] = _dot(ltri_ref[...], blocks.astype(BF16)) * EXPERT_ROWS
        run_ref[...] = jnp.zeros_like(run_ref)

    for sub in range(RANK_STEP // c):
        cols = slice(sub * c, (sub + 1) * c)
        onehot = jnp.where(expert == e_ref[:, cols], 1.0, 0.0)
        oh16 = onehot.astype(BF16)
        before = _dot(oh16, upper_ref[...])
        pos = before + run_ref[:, 0:1] + base_ref[:, 0:1]
        dest_ref[:, cols] = jnp.sum(onehot * pos, axis=0, keepdims=True).astype(jnp.int32)
        run_ref[...] = run_ref[...] + _dot(oh16, ones_ref[...])
    cnt_ref[...] = run_ref[...]


def _moe_rank(e_flat, consts):
    total = e_flat.shape[1]
    c = RANK_CHUNK
    st = RANK_STEP
    return pl.pallas_call(
        _moe_rank_kernel,
        grid=(2, total // st),
        in_specs=[pl.BlockSpec((1, st), lambda p, s: (0, s)),
                  _full((c, c)), _full((c, LANES)), _full((N_EXPERTS, N_EXPERTS))],
        out_specs=(pl.BlockSpec((1, st), lambda p, s: (0, s * p)),
                   pl.BlockSpec((N_EXPERTS, LANES), lambda p, s: (0, 0))),
        out_shape=(jax.ShapeDtypeStruct((1, total), jnp.int32),
                   jax.ShapeDtypeStruct((N_EXPERTS, LANES), F32)),
        scratch_shapes=[pltpu.VMEM((N_EXPERTS, LANES), F32), pltpu.VMEM((N_EXPERTS, LANES), F32)],
        compiler_params=pltpu.CompilerParams(dimension_semantics=("arbitrary", "arbitrary")),
        name="moe_rank",
    )(e_flat, consts["upper"], consts["ones"], consts["ltri"])


def _store_row_tiled(ref, x, offset=0):
    rows = x.shape[0]
    for s in range(ROW_TILES):
        ref[pl.ds(offset + s, rows, stride=ROW_TILES), :] = x[:, s * LANES:(s + 1) * LANES]


def _load_row_tiled(ref, rows, offset=0):
    return jnp.concatenate(
        [ref[pl.ds(offset + s, rows, stride=ROW_TILES), :] for s in range(ROW_TILES)], axis=1)


def _row_copy(src, src_row, dst, dst_row, sem):
    return pltpu.make_async_copy(
        src.at[pl.ds(pl.multiple_of(src_row * ROW_TILES, ROW_TILES), ROW_TILES)],
        dst.at[pl.ds(pl.multiple_of(dst_row * ROW_TILES, ROW_TILES), ROW_TILES)], sem)


def _rows_wait(src, dst, dst_row, rows, sem):
    pltpu.make_async_copy(
        src.at[pl.ds(0, rows * ROW_TILES)],
        dst.at[pl.ds(pl.multiple_of(dst_row * ROW_TILES, ROW_TILES), rows * ROW_TILES)], sem).wait()


def _moe_dispatch_kernel(dest_ref, cnt_ref, region_ref, m_ref, xs_ref, inv_ref, zbuf_ref, sem, zsem, *, n_tok):
    t = DISPATCH_TILE
    r = EXPERT_ROWS
    i = pl.program_id(0)

    @pl.when(i == 0)
    def _():
        def clear(s, carry):
            inv_ref[s] = 0
            return carry

        def clear_padding(e, carry):
            lax.fori_loop(region_ref[e] + cnt_ref[e], region_ref[e + 1], clear, 0)
            return carry

        lax.fori_loop(0, N_EXPERTS, clear_padding, 0)
        lax.fori_loop(region_ref[N_EXPERTS], inv_ref.shape[0], clear, 0)

    @pl.when(i == 0)
    def _():
        zbuf_ref[...] = jnp.zeros_like(zbuf_ref)

        def zero_copy(e):
            last = pl.multiple_of((region_ref[e + 1] - r) * ROW_TILES, r * ROW_TILES)
            return pltpu.make_async_copy(zbuf_ref, xs_ref.at[pl.ds(last, r * ROW_TILES)], zsem)

        def start(e, carry):
            @pl.when(cnt_ref[e] > 0)
            def _():
                zero_copy(e).start()
            return carry

        def wait(e, carry):
            @pl.when(cnt_ref[e] > 0)
            def _():
                zero_copy(e).wait()
            return carry

        def tail_copy(blk):
            return pltpu.make_async_copy(
                zbuf_ref, xs_ref.at[pl.ds(pl.multiple_of(blk * (r * ROW_TILES), r * ROW_TILES), r * ROW_TILES)], zsem)

        def tail_start(blk, carry):
            tail_copy(blk).start()
            return carry

        def tail_wait(blk, carry):
            tail_copy(blk).wait()
            return carry

        first_unused = region_ref[N_EXPERTS] // r
        n_blocks = xs_ref.shape[0] // (r * ROW_TILES)
        lax.fori_loop(0, N_EXPERTS, start, 0)
        lax.fori_loop(first_unused, n_blocks, tail_start, 0)
        lax.fori_loop(0, N_EXPERTS, wait, 0)
        lax.fori_loop(first_unused, n_blocks, tail_wait, 0)

    per_iter = DMA_UNROLL // TOP_K

    def issue(g, carry):
        for u in range(per_iter):
            row = g * per_iter + u
            for k in range(TOP_K):
                assignment = k * n_tok + i * t + row
                slot = dest_ref[assignment]
                inv_ref[slot] = assignment
                _row_copy(m_ref, row, xs_ref, slot, sem).start(priority=k % 2)
        return carry

    lax.fori_loop(0, t // per_iter, issue, 0)
    for k in range(TOP_K):
        pltpu.make_async_copy(m_ref, xs_ref.at[pl.ds(0, t * ROW_TILES)], sem).wait()


def _moe_dispatch(dest, cnt, region, m_rt, rows):
    n = m_rt.shape[0] // ROW_TILES
    t = DISPATCH_TILE
    return pl.pallas_call(
        functools.partial(_moe_dispatch_kernel, n_tok=n),
        grid_spec=pltpu.PrefetchScalarGridSpec(
            num_scalar_prefetch=3,
            grid=(n // t,),
            in_specs=[pl.BlockSpec((t * ROW_TILES, LANES), lambda i, d, c, rg: (i, 0))],
            out_specs=(pl.BlockSpec(memory_space=pl.ANY), pl.BlockSpec(memory_space=pltpu.SMEM)),
            scratch_shapes=[pltpu.VMEM((EXPERT_ROWS * ROW_TILES, LANES), F32),
                            pltpu.SemaphoreType.DMA(()), pltpu.SemaphoreType.DMA(())],
        ),
        out_shape=(jax.ShapeDtypeStruct((rows * ROW_TILES, LANES), F32),
                   jax.ShapeDtypeStruct((rows,), jnp.int32)),
        compiler_params=pltpu.CompilerParams(dimension_semantics=("arbitrary",)),
        name="moe_dispatch",
    )(dest, cnt, region, m_rt)


def _moe_experts_kernel(blk_e_ref, nact_ref, nvalid_ref, inv_ref, x_ref, wg_ref, wu_ref, wd_ref, out_ref,
                        wgu_ref, wdn_ref, ybuf0_ref, ybuf1_ref, sem, *, n_assign):
    r = EXPERT_ROWS
    b = pl.program_id(0)
    nact = nact_ref[0]
    ybufs = (ybuf0_ref, ybuf1_ref)

    def scatter(blk, parity):
        valid = jnp.where(blk >= 0, nvalid_ref[jnp.maximum(blk, 0)], 0)
        base = jnp.maximum(blk, 0) * r
        dump = n_assign + parity * r
        for i in range(r):
            row = jnp.where(i < valid, inv_ref[base + i], dump + i)
            _row_copy(ybufs[parity], i, out_ref, row, sem.at[parity]).start(priority=i % 2)

    def scatter_wait(parity):
        _rows_wait(out_ref, ybuf0_ref, 0, r, sem.at[parity])

    @pl.when(b == 0)
    def _():
        ybuf1_ref[...] = jnp.zeros_like(ybuf1_ref)
        fill = pltpu.make_async_copy(
            ybuf1_ref, out_ref.at[pl.ds(n_assign * ROW_TILES, r * ROW_TILES)], sem.at[0])
        fill.start()
        fill.wait()

    @pl.when(jnp.logical_and(b >= 1, b <= nact))
    def _():
        scatter_wait(b % 2)

    def compute(parity):
        @pl.when(jnp.logical_or(b == 0, blk_e_ref[b] != blk_e_ref[jnp.maximum(b - 1, 0)]))
        def _():
            wgu_ref[:, :D_EXPERT] = wg_ref[0, 0].astype(BF16)
            wgu_ref[:, D_EXPERT:] = wu_ref[0, 0].astype(BF16)
            wdn_ref[...] = wd_ref[0, 0].astype(BF16)

        x = _load_row_tiled(x_ref, r).astype(BF16)
        gu = _dot(x, wgu_ref[...])
        hb = jax.nn.silu(gu[:, :D_EXPERT]) * gu[:, D_EXPERT:]
        _store_row_tiled(ybufs[parity], _dot(hb.astype(BF16), wdn_ref[...]))
        scatter(b - 1, 1 - parity)

    for parity in range(2):
        @pl.when(jnp.logical_and(b < nact, b % 2 == parity))
        def _(parity=parity):
            compute(parity)

        @pl.when(jnp.logical_and(b == nact, (b - 1) % 2 == parity))
        def _(parity=parity):
            scatter(b - 1, parity)
            scatter_wait(parity)


def _moe_experts(blk_e, nact, nvalid, inv, xs_rt, layer, w_gate, w_up, w_down, n_assign):
    rows = xs_rt.shape[0] // ROW_TILES
    r = EXPERT_ROWS
    wspec = lambda k, n: pl.BlockSpec((1, 1, k, n), lambda b, be, na, nv, iv: (layer, be[b], 0, 0))
    ybuf = pltpu.VMEM((r * ROW_TILES, LANES), F32)
    return pl.pallas_call(
        functools.partial(_moe_experts_kernel, n_assign=n_assign),
        grid_spec=pltpu.PrefetchScalarGridSpec(
            num_scalar_prefetch=4,
            grid=(rows // r,),
            in_specs=[pl.BlockSpec((r * ROW_TILES, LANES), lambda b, be, na, nv, iv: (jnp.minimum(b, na[0] - 1), 0)),
                      wspec(D_MODEL, D_EXPERT), wspec(D_MODEL, D_EXPERT), wspec(D_EXPERT, D_MODEL)],
            out_specs=pl.BlockSpec(memory_space=pl.ANY),
            scratch_shapes=[pltpu.VMEM((D_MODEL, 2 * D_EXPERT), BF16),
                            pltpu.VMEM((D_EXPERT, D_MODEL), BF16),
                            ybuf, ybuf, pltpu.SemaphoreType.DMA((2,))],
        ),
        out_shape=jax.ShapeDtypeStruct(((n_assign + 2 * r) * ROW_TILES, LANES), F32),
        compiler_params=pltpu.CompilerParams(dimension_semantics=("arbitrary",), vmem_limit_bytes=VMEM_LIMIT),
        name="moe_experts",
    )(blk_e, nact, nvalid, inv, xs_rt, w_gate, w_up, w_down)


def _combine_ple_kernel(h1_ref, wt_ref, p_ref, y0_ref, y1_ref, pleg_ref, gatew_ref, projw_ref, postg_ref, out_ref):
    t = COMBINE_TILE
    e = _rms(_dot(p_ref[...].astype(BF16), projw_ref[...]), postg_ref[...])
    wt = wt_ref[...].T
    h2 = h1_ref[...] + (wt[:, 0:1] * _load_row_tiled(y0_ref, t) + wt[:, 1:2] * _load_row_tiled(y1_ref, t))
    gate = jax.nn.sigmoid(_dot(_rms(h2, pleg_ref[...]).astype(BF16), gatew_ref[...]))
    out_ref[...] = h2 + gate * e


def _combine_ple(h1, wt, p, y_rt, lw):
    n = h1.shape[0]
    t = COMBINE_TILE
    steps = n // t
    row = lambda w: pl.BlockSpec((t, w), lambda i: (i, 0))
    layer_rows = lw["layer"] * steps
    weights = [lw[k] for k in ("ple_g", "gate_w", "proj_w", "post_g")]
    return pl.pallas_call(
        _combine_ple_kernel,
        grid=(steps,),
        in_specs=[row(D_MODEL), pl.BlockSpec((SUBLANES, t), lambda i: (0, i)),
                  pl.BlockSpec((t, p.shape[1]), lambda i: (layer_rows + i, 0)),
                  pl.BlockSpec((t * ROW_TILES, LANES), lambda i: (i, 0)),
                  pl.BlockSpec((t * ROW_TILES, LANES), lambda i: (steps + i, 0))]
        + [_full(w.shape) for w in weights],
        out_specs=row(D_MODEL),
        out_shape=jax.ShapeDtypeStruct((n, D_MODEL), F32),
        compiler_params=pltpu.CompilerParams(dimension_semantics=("arbitrary",), vmem_limit_bytes=VMEM_LIMIT),
        name="combine_ple",
    )(h1, wt, p, y_rt, y_rt, *weights)


def _segment_matrix(seg_ids):
    seg_ids = jnp.asarray(seg_ids)
    same = (seg_ids[:, None] == seg_ids[None, :]).astype(F32)
    return (same / jnp.sum(same, axis=1, keepdims=True)).astype(BF16)


def _constants():
    lane = jnp.arange(256)
    qk_seg = (lane // HEAD_PAD) * 3 + jnp.where(lane % HEAD_PAD < MLA_NOPE, 0, jnp.where(lane % HEAD_PAD < MLA_QK, 1, 2))
    i = jnp.arange(RANK_CHUNK)
    e = jnp.arange(N_EXPERTS)
    return {
        "g64": _segment_matrix(lane // HEAD_DIM),
        "gqk": _segment_matrix(qk_seg),
        "upper": (i[:, None] < i[None, :]).astype(BF16),
        "ones": jnp.ones((RANK_CHUNK, LANES), BF16),
        "ltri": (e[None, :] < e[:, None]).astype(BF16),
    }


def _layer_weights(i, consts, p):
    row = lambda v: v.reshape(1, -1).astype(F32)
    w_in = p["w_in"][i]
    o_sgu, o_conv, o_q, o_kv = 2 * SGU_WIDTH, 2 * SGU_WIDTH + 2 * CONV_WIDTH, 0, 0
    o_q = o_conv + Q_RANK
    o_kv = o_q + KV_RANK
    w_ckv = jnp.concatenate([w_in[:, o_q:], jnp.zeros((D_MODEL, 256 - KV_RANK - MLA_ROPE), F32)], axis=1)

    causal = jnp.tril(jnp.ones((CHUNK, CHUNK), F32))
    wcat = jnp.transpose(p["sgu_w"][i] * causal, (1, 0, 2)).reshape(CHUNK, SGU_HEADS * CHUNK)
    sbias = jnp.broadcast_to(jnp.transpose(p["sgu_b"][i])[:, :, None],
                             (CHUNK, SGU_HEADS, HEAD_DIM)).reshape(CHUNK, SGU_WIDTH)

    w_uq = p["w_uq"][i].reshape(Q_RANK, MLA_HEADS, MLA_QK)
    w_uq = jnp.pad(w_uq, ((0, 0), (0, 0), (0, HEAD_PAD - MLA_QK))).reshape(Q_RANK, MLA_HEADS * HEAD_PAD)
    qn_g = p["q_norm_g"][i]
    q_gain = jnp.tile(jnp.concatenate([qn_g, jnp.zeros((HEAD_PAD - MLA_QK,), F32)]), MLA_HEADS)
    w_ukv = p["w_ukv"][i].reshape(KV_RANK, MLA_HEADS, MLA_NOPE + MLA_V)
    w_ukt = jnp.transpose(w_ukv[:, :, :MLA_NOPE], (1, 2, 0)).reshape(MLA_HEADS * MLA_NOPE, KV_RANK)
    w_v = w_ukv[:, :, MLA_NOPE:].reshape(KV_RANK, MLA_HEADS // 2, 2, MLA_V)
    zeros_v = jnp.zeros((KV_RANK, MLA_HEADS // 2, MLA_V), F32)
    w_uv = jnp.stack([w_v[:, :, 0], zeros_v, zeros_v, w_v[:, :, 1]], axis=2).reshape(KV_RANK, MLA_HEADS * HEAD_PAD)
    v_ones = jnp.tile(jnp.repeat(jnp.array([0.0, 1.0, 1.0, 0.0], F32), MLA_V), MLA_HEADS // 2)
    kn_g = p["k_norm_g"][i]

    bg = p["branch_norm_g"][i]
    w_o = p["w_o"][i]
    wr = jnp.concatenate([p["router_group_w"][i], jnp.zeros((D_MODEL, 8 - N_GROUPS), F32),
                          p["router_expert_w"][i], jnp.zeros((D_MODEL, LANES - 8 - N_EXPERTS), F32)], axis=1)
    wr_hi = wr.astype(BF16)
    br = jnp.concatenate([p["router_group_b"][i], jnp.full((8 - N_GROUPS,), NEG_BIG, F32),
                          p["router_expert_b"][i], jnp.zeros((LANES - 8 - N_EXPERTS,), F32)])
    return {
        "mix_g": row(p["mix_norm_g"][i]),
        "w_sgu": w_in[:, :o_sgu].astype(BF16),
        "w_conv": w_in[:, o_sgu:o_conv].astype(BF16),
        "w_cq": w_in[:, o_conv:o_q].astype(BF16),
        "w_ckv": w_ckv.astype(BF16),
        "sgu_ln_g": row(p["sgu_ln_g"][i]), "sgu_ln_b": row(p["sgu_ln_b"][i]),
        "sgu_wcat": wcat.astype(BF16), "sgu_bias": sbias, "g64": consts["g64"], "bg_a": row(bg[:SGU_WIDTH]),
        "conv_w": jnp.pad(p["conv_w"][i], ((0, 1), (0, 0))), "conv_b": row(p["conv_b"][i]),
        "conv_ln_g": row(p["conv_ln_g"][i]), "conv_ln_b": row(p["conv_ln_b"][i]),
        "pw_w": p["conv_pw_w"][i].astype(BF16), "pw_b": row(p["conv_pw_b"][i]),
        "bg_b": row(bg[SGU_WIDTH:SGU_WIDTH + CONV_WIDTH]),
        "qa_g": row(p["q_a_norm_g"][i]), "w_uq": w_uq.astype(BF16), "gqk": consts["gqk"],
        "q_gain": row(q_gain),
        "kva_g": row(p["kv_a_norm_g"][i]), "w_ukt": w_ukt.astype(BF16), "w_uv": w_uv.astype(BF16),
        "v_ones": row(v_ones),
        "kn_g": jnp.broadcast_to(kn_g[:MLA_NOPE, None], (MLA_NOPE, MIX_TILE)),
        "kpe_g": jnp.broadcast_to(kn_g[MLA_NOPE:, None], (MLA_ROPE, MIX_TILE)),
        "bg_c": row(bg[SGU_WIDTH + CONV_WIDTH:]),
        "w_o_a": w_o[:SGU_WIDTH].astype(BF16),
        "w_o_b": w_o[SGU_WIDTH:SGU_WIDTH + CONV_WIDTH].astype(BF16),
        "w_o_c": w_o[SGU_WIDTH + CONV_WIDTH:].astype(BF16),
        "ffn_g": row(p["ffn_norm_g"][i]),
        "wr_split": jnp.concatenate([wr_hi, (wr - wr_hi.astype(F32)).astype(BF16)], axis=1), "br": row(br),
        "layer": i, "w_gate": p["moe_w_gate"], "w_up": p["moe_w_up"], "w_down": p["moe_w_down"],
        "ple_g": row(p["ple_norm_g"][i]), "gate_w": p["ple_gate_w"][i].astype(BF16),
        "proj_w": p["ple_proj_w"][i].astype(BF16), "post_g": row(p["ple_post_norm_g"][i]),
    }


def _moe(h1, m, eidx, wts, pl_i, lw, consts):
    n = h1.shape[0]
    total = TOP_K * n
    rows = total + N_EXPERTS * EXPERT_ROWS
    dest2d, counts = _moe_rank(eidx[:TOP_K].reshape(1, total), consts)
    dest = dest2d.reshape(total)
    cnt = counts[:, 0].astype(jnp.int32)
    padded = (cnt + EXPERT_ROWS - 1) // EXPERT_ROWS * EXPERT_ROWS
    pend = jnp.cumsum(padded)
    nblk = rows // EXPERT_ROWS
    blk_start = jnp.arange(nblk, dtype=jnp.int32) * EXPERT_ROWS
    blk_e = jnp.minimum(jnp.sum((pend[None, :] <= blk_start[:, None]).astype(jnp.int32), axis=1), N_EXPERTS - 1)
    nact = (pend[-1:] // EXPERT_ROWS).astype(jnp.int32)
    region = jnp.concatenate([jnp.zeros((1,), jnp.int32), pend.astype(jnp.int32)])
    nvalid = jnp.clip(cnt[blk_e] - (blk_start - region[blk_e]), 0, EXPERT_ROWS)
    nvalid = jnp.where(jnp.arange(nblk) < nact[0], nvalid, 0).astype(jnp.int32)
    xs, inv = _moe_dispatch(dest, cnt, region, m, rows)
    y = _moe_experts(blk_e, nact, nvalid, inv, xs, lw["layer"], lw["w_gate"], lw["w_up"], lw["w_down"], total)
    return _combine_ple(h1, wts, pl_i, y, lw)


def kernel(x, p, positions, mix_norm_g, w_in, sgu_ln_g, sgu_ln_b, sgu_w, sgu_b, conv_w, conv_b, conv_ln_g, conv_ln_b, conv_pw_w, conv_pw_b, q_a_norm_g, w_uq, kv_a_norm_g, w_ukv, q_norm_g, k_norm_g, branch_norm_g, w_o, ffn_norm_g, router_group_w, router_group_b, router_expert_w, router_expert_b, moe_w_gate, moe_w_up, moe_w_down, ple_norm_g, ple_gate_w, ple_proj_w, ple_post_norm_g):
    params = dict(
        mix_norm_g=mix_norm_g, w_in=w_in, sgu_ln_g=sgu_ln_g, sgu_ln_b=sgu_ln_b, sgu_w=sgu_w, sgu_b=sgu_b,
        conv_w=conv_w, conv_b=conv_b, conv_ln_g=conv_ln_g, conv_ln_b=conv_ln_b, conv_pw_w=conv_pw_w,
        conv_pw_b=conv_pw_b, q_a_norm_g=q_a_norm_g, w_uq=w_uq, kv_a_norm_g=kv_a_norm_g, w_ukv=w_ukv,
        q_norm_g=q_norm_g, k_norm_g=k_norm_g, branch_norm_g=branch_norm_g, w_o=w_o, ffn_norm_g=ffn_norm_g,
        router_group_w=router_group_w, router_group_b=router_group_b, router_expert_w=router_expert_w,
        router_expert_b=router_expert_b, moe_w_gate=moe_w_gate, moe_w_up=moe_w_up, moe_w_down=moe_w_down,
        ple_norm_g=ple_norm_g, ple_gate_w=ple_gate_w, ple_proj_w=ple_proj_w, ple_post_norm_g=ple_post_norm_g)
    batch, seq, d = x.shape
    n = batch * seq
    depth = w_in.shape[0]
    consts = _constants()
    cos, sin = _rope_tables(positions)
    h = x.reshape(n, d)
    p_rows = p.reshape(depth * n, p.shape[-1])
    for i in range(depth):
        lw = _layer_weights(i, consts, params)
        ya, yb, q, kt, v = _mixer_pre(h, cos, sin, lw, batch, seq)
        yc = _attention(q, kt, v, batch, seq)
        h1, m, eidx, wts = _outproj_router(h, ya, yb, yc, lw)
        h = _moe(h1, m, eidx, wts, p_rows, lw, consts)
    return h.reshape(batch, seq, d)
```

```python
import functools
import math

import jax
import jax.numpy as jnp
from jax import lax
from jax.experimental import pallas as pl
from jax.experimental.pallas import tpu as pltpu

F32 = jnp.float32
BF16 = jnp.bfloat16

D_MODEL = 1024
HEAD_DIM = 64
SGU_HEADS = 4
SGU_WIDTH = 256
CHUNK = 128
CONV_WIDTH = 256
CONV_KERNEL = 31
MLA_HEADS = 8
MLA_NOPE = 64
MLA_ROPE = 32
MLA_QK = MLA_NOPE + MLA_ROPE
MLA_V = 64
Q_RANK = 256
KV_RANK = 128
ROPE_THETA = 10000.0
N_GROUPS = 4
EXPERTS_PER_GROUP = 8
N_EXPERTS = 32
D_EXPERT = 256
TOP_K = 2
EPS = 1e-6

LANES = 128
SUBLANES = 8
HEAD_PAD = 128
MIX_TILE = 512
ROUTER_TILE = 1024
CONV_HALO = 32
CONV_SHIFTED_ROWS = MIX_TILE + CONV_HALO - SUBLANES
MIXER_ORDER = ("q", "kv", "sgu", "conv")
ATT_TQ = 256
ATT_LOOKAHEAD = 1
RANK_CHUNK = 512
RANK_STEP = 2048
EXPERT_ROWS = 512
DISPATCH_TILE = 512
COMBINE_TILE = 512
ROW_TILES = D_MODEL // LANES
DMA_UNROLL = 8
VMEM_LIMIT = 48 * 1024 * 1024
NEG_BIG = -1e30


def _dot(a, b):
    return jnp.dot(a, b, preferred_element_type=F32)


def _rms(x, g):
    ms = jnp.mean(x * x, axis=-1, keepdims=True)
    return x * lax.rsqrt(ms + EPS) * g


def _seg_mean(x, g_ref, split=True):
    g = g_ref[...]
    outs = []
    for c in range(x.shape[1] // 256):
        xb = x[:, c * 256:(c + 1) * 256]
        hi = xb.astype(BF16)
        acc = _dot(hi, g)
        if split:
            acc = acc + _dot((xb - hi.astype(F32)).astype(BF16), g)
        outs.append(acc)
    return outs[0] if len(outs) == 1 else jnp.concatenate(outs, axis=-1)


def _group_layernorm(x, g_ref, gain, bias):
    d = x - _seg_mean(x, g_ref)
    var = _seg_mean(d * d, g_ref)
    return d * lax.rsqrt(var + EPS) * gain + bias


def _rope_table_kernel(pos_ref, invf_ref, cos_ref, sin_ref):
    ang = pos_ref[...].astype(F32) * invf_ref[...]
    cos_ref[...] = jnp.cos(ang)
    sin_ref[...] = jnp.sin(ang)


def _rope_tables(positions):
    b, s = positions.shape
    n = b * s
    half = MLA_ROPE // 2
    inv_freq = ROPE_THETA ** (-jnp.arange(0, MLA_ROPE, 2, dtype=F32) / MLA_ROPE)
    pos_rep = jnp.broadcast_to(positions.reshape(n, 1), (n, half)).reshape(n * half // LANES, LANES)
    invf_rep = jnp.tile(inv_freq, LANES // half).reshape(1, LANES)
    shape = jax.ShapeDtypeStruct(pos_rep.shape, F32)
    cos, sin = pl.pallas_call(_rope_table_kernel, out_shape=(shape, shape), name="rope_tables")(pos_rep, invf_rep)
    return cos.reshape(n, half), sin.reshape(n, half)


def _mixer_pre_kernel(
        h_ref, cos_ref, sin_ref,
        mixg_ref, wsgu_ref, wconv_ref, wcq_ref, wckv_ref,
        slng_ref, slnb_ref, wcat_ref, sbias_ref, g64_ref, bga_ref,
        cw_ref, cb_ref, clng_ref, clnb_ref, pww_ref, pwb_ref, bgb_ref,
        qag_ref, wuq_ref, gqk_ref, qgain_ref,
        kvag_ref, wukt_ref, wuv_ref, vones_ref, kng_ref, kpeg_ref,
        ya_ref, yb_ref, q_ref, kt_ref, v_ref,
        ybuf_ref, ysh_ref, *, tiles_per_seq):
    t = MIX_TILE
    a = _rms(h_ref[...], mixg_ref[...]).astype(BF16)
    zs = _dot(a, wsgu_ref[...])
    zc = _dot(a, wconv_ref[...])
    zq = _dot(a, wcq_ref[...])
    zkv = _dot(a, wckv_ref[...])

    half = MLA_ROPE // 2
    cos, sin = cos_ref[...], sin_ref[...]
    zn = jnp.zeros((t, MLA_NOPE), F32)
    zh = jnp.zeros((t, half), F32)
    zp = jnp.zeros((t, HEAD_PAD - MLA_QK), F32)
    rc = jnp.concatenate([jnp.ones((t, MLA_NOPE), F32), cos, cos, zp], axis=1)
    rs1 = jnp.concatenate([zn, -sin, zh, zp], axis=1)
    rs2 = jnp.concatenate([zn, zh, sin, zp], axis=1)

    def sgu_branch():
        zg = jax.nn.gelu(zs)
        u = zg[:, :SGU_WIDTH]
        vn = _group_layernorm(zg[:, SGU_WIDTH:], g64_ref, slng_ref[...], slnb_ref[...])
        lane_head = lax.broadcasted_iota(jnp.int32, (CHUNK, SGU_WIDTH), 1) // HEAD_DIM
        wcat = wcat_ref[...]
        sbias = sbias_ref[...]
        parts = []
        for c in range(t // CHUNK):
            vc = vn[c * CHUNK:(c + 1) * CHUNK]
            stacked = jnp.concatenate(
                [jnp.where(lane_head == hh, vc, 0.0) for hh in range(SGU_HEADS)], axis=0).astype(BF16)
            s = _dot(wcat, stacked) + sbias
            parts.append(u[c * CHUNK:(c + 1) * CHUNK] * s)
        ya_ref[...] = _rms(jnp.concatenate(parts, axis=0), bga_ref[...]).astype(BF16)

    def conv_branch():
        yg = zc[:, :CONV_WIDTH] * jax.nn.sigmoid(zc[:, CONV_WIDTH:])
        first = (pl.program_id(0) % tiles_per_seq) == 0

        @pl.when(first)
        def _():
            ybuf_ref[0:CONV_HALO, :] = jnp.zeros((CONV_HALO, CONV_WIDTH), F32)

        @pl.when(jnp.logical_not(first))
        def _():
            ybuf_ref[0:CONV_HALO, :] = ybuf_ref[t:t + CONV_HALO, :]

        ybuf_ref[CONV_HALO:CONV_HALO + t, :] = yg
        for sh in range(1, SUBLANES):
            ysh_ref[sh - 1] = ybuf_ref[pl.ds(sh, CONV_SHIFTED_ROWS), :]
        rows = 64
        first_tap_row = CONV_HALO - (CONV_KERNEL - 1)
        conv_parts = []
        for r in range(t // rows):
            acc = jnp.broadcast_to(cb_ref[...], (rows, CONV_WIDTH))
            for tap in range(CONV_KERNEL):
                sh = (first_tap_row + tap) % SUBLANES
                start = r * rows + first_tap_row + tap - sh
                src = ybuf_ref[pl.ds(start, rows), :] if sh == 0 else ysh_ref[sh - 1, pl.ds(start, rows), :]
                acc = acc + cw_ref[tap:tap + 1, :] * src
            conv_parts.append(acc)
        cv = jnp.concatenate(conv_parts, axis=0)
        cn = _group_layernorm(cv, g64_ref, clng_ref[...], clnb_ref[...])
        yb = _dot(jax.nn.silu(cn).astype(BF16), pww_ref[...]) + pwb_ref[...]
        yb_ref[...] = _rms(yb, bgb_ref[...]).astype(BF16)

    def query_branch():
        cqn = _rms(zq, qag_ref[...]).astype(BF16)
        qf = _dot(cqn, wuq_ref[...])
        qn = qf * lax.rsqrt(_seg_mean(qf * qf, gqk_ref, split=False) + EPS) * qgain_ref[...]
        for hh in range(MLA_HEADS):
            blk = qn[:, hh * HEAD_PAD:(hh + 1) * HEAD_PAD]
            rot = blk * rc + pltpu.roll(blk, HEAD_PAD - half, 1) * rs1 + pltpu.roll(blk, half, 1) * rs2
            q_ref[:, hh * HEAD_PAD:(hh + 1) * HEAD_PAD] = rot.astype(BF16)

    def key_value_branch():
        ckvn = _rms(zkv[:, :KV_RANK], kvag_ref[...]).astype(BF16)
        v_ref[...] = (_dot(ckvn, wuv_ref[...]) + vones_ref[...]).astype(BF16)
        knt = lax.dot_general(wukt_ref[...], ckvn, (((1,), (1,)), ((), ())), preferred_element_type=F32)
        x = zkv[:, KV_RANK:].T[0:MLA_ROPE]
        xn = x * lax.rsqrt(jnp.mean(x * x, axis=0, keepdims=True) + EPS) * kpeg_ref[...]
        x1, x2 = xn[:MLA_ROPE // 2], xn[MLA_ROPE // 2:]
        cos_t = rc.T[MLA_NOPE:MLA_NOPE + half]
        sin_t = rs2.T[MLA_NOPE + half:MLA_QK]
        kpe = jnp.concatenate([x1 * cos_t - x2 * sin_t, x2 * cos_t + x1 * sin_t], axis=0)
        pad = jnp.zeros((HEAD_PAD - MLA_QK, t), F32)
        kng = kng_ref[...]
        for hh in range(MLA_HEADS):
            blk = knt[hh * MLA_NOPE:(hh + 1) * MLA_NOPE]
            kn = blk * lax.rsqrt(jnp.mean(blk * blk, axis=0, keepdims=True) + EPS) * kng
            kt_ref[0, hh * HEAD_PAD:(hh + 1) * HEAD_PAD, :] = jnp.concatenate([kn, kpe, pad], axis=0).astype(BF16)

    branches = {"sgu": sgu_branch, "conv": conv_branch, "q": query_branch, "kv": key_value_branch}
    for name in MIXER_ORDER:
        branches[name]()


def _full(shape):
    nd = len(shape)
    return pl.BlockSpec(shape, lambda *_: (0,) * nd)


def _mixer_pre(h, cos, sin, lw, batch, seq):
    n = h.shape[0]
    t = MIX_TILE
    tps = seq // t
    row = lambda w: pl.BlockSpec((t, w), lambda i: (i, 0))
    weights = [lw[k] for k in (
        "mix_g", "w_sgu", "w_conv", "w_cq", "w_ckv",
        "sgu_ln_g", "sgu_ln_b", "sgu_wcat", "sgu_bias", "g64", "bg_a",
        "conv_w", "conv_b", "conv_ln_g", "conv_ln_b", "pw_w", "pw_b", "bg_b",
        "qa_g", "w_uq", "gqk", "q_gain",
        "kva_g", "w_ukt", "w_uv", "v_ones", "kn_g", "kpe_g")]
    in_specs = [row(D_MODEL), row(MLA_ROPE // 2), row(MLA_ROPE // 2)] + [_full(w.shape) for w in weights]
    out_shape = (
        jax.ShapeDtypeStruct((n, SGU_WIDTH), BF16),
        jax.ShapeDtypeStruct((n, CONV_WIDTH), BF16),
        jax.ShapeDtypeStruct((n, MLA_HEADS * HEAD_PAD), BF16),
        jax.ShapeDtypeStruct((batch, MLA_HEADS * HEAD_PAD, seq), BF16),
        jax.ShapeDtypeStruct((n, MLA_HEADS * HEAD_PAD), BF16),
    )
    out_specs = (
        row(SGU_WIDTH), row(CONV_WIDTH), row(MLA_HEADS * HEAD_PAD),
        pl.BlockSpec((1, MLA_HEADS * HEAD_PAD, t), lambda i: (i // tps, 0, i % tps)),
        row(MLA_HEADS * HEAD_PAD),
    )
    return pl.pallas_call(
        functools.partial(_mixer_pre_kernel, tiles_per_seq=tps),
        grid=(n // t,),
        in_specs=in_specs,
        out_specs=out_specs,
        out_shape=out_shape,
        scratch_shapes=[pltpu.VMEM((t + CONV_HALO, CONV_WIDTH), F32),
                        pltpu.VMEM((SUBLANES - 1, CONV_SHIFTED_ROWS, CONV_WIDTH), F32)],
        compiler_params=pltpu.CompilerParams(dimension_semantics=("arbitrary",), vmem_limit_bytes=VMEM_LIMIT),
        name="mixer_pre",
    )(h, cos, sin, *weights)


def _attention_kernel(q_ref, kt_ref, v_ref, o_ref, *, seq):
    tq = ATT_TQ
    exp2_scale = MLA_QK ** -0.5 * math.log2(math.e)
    row = lax.broadcasted_iota(jnp.int32, (tq, tq), 0)
    col = lax.broadcasted_iota(jnp.int32, (tq, tq), 1)
    lane = lax.broadcasted_iota(jnp.int32, (tq, HEAD_PAD), 1)
    def scores(qi, hh):
        nk = (qi + 1) * tq
        q = q_ref[qi * tq:(qi + 1) * tq, hh * HEAD_PAD:(hh + 1) * HEAD_PAD]
        s = _dot(q, kt_ref[0, hh * HEAD_PAD:(hh + 1) * HEAD_PAD, 0:nk])
        diag = jnp.where(col <= row, s[:, nk - tq:], NEG_BIG)
        return diag if qi == 0 else jnp.concatenate([s[:, :nk - tq], diag], axis=1)

    n_blocks = seq // tq
    ahead = [scores(0, hh) for hh in range(2)]
    for qi in range(n_blocks):
        s_pair, ahead, p_pair = ahead, [], []
        for hh in range(2):
            if qi + 1 < n_blocks:
                ahead.append(scores(qi + 1, hh))
            s = s_pair[hh]
            p_pair.append(jnp.exp2(((s - jnp.max(s, axis=-1, keepdims=True)) * exp2_scale).astype(BF16)))
        acc = _dot(jnp.concatenate(p_pair, axis=0), v_ref[0:(qi + 1) * tq, :])
        a0, a1 = acc[:tq, :HEAD_PAD], acc[tq:, HEAD_PAD:]
        o_ref[qi * tq:(qi + 1) * tq, :] = jnp.where(
            lane < MLA_V, a0 / pltpu.roll(a0, MLA_V, 1), a1 / pltpu.roll(a1, MLA_V, 1))


def _attention(q, kt, v, batch, seq):
    n = q.shape[0]
    pairs = MLA_HEADS // 2
    return pl.pallas_call(
        functools.partial(_attention_kernel, seq=seq),
        grid=(batch, pairs),
        in_specs=[
            pl.BlockSpec((seq, 2 * HEAD_PAD), lambda b, p: (b, p)),
            pl.BlockSpec((1, 2 * HEAD_PAD, seq), lambda b, p: (b, p, 0)),
            pl.BlockSpec((seq, 2 * HEAD_PAD), lambda b, p: (b, p)),
        ],
        out_specs=pl.BlockSpec((seq, 2 * MLA_V), lambda b, p: (b, p)),
        out_shape=jax.ShapeDtypeStruct((n, MLA_HEADS * MLA_V), F32),
        compiler_params=pltpu.CompilerParams(
            dimension_semantics=("arbitrary", "arbitrary"), vmem_limit_bytes=VMEM_LIMIT),
        name="attention",
    )(q, kt, v)


def _outproj_router_kernel(h_ref, ya_ref, yb_ref, yc_ref, bgc_ref, woa_ref, wob_ref, woc_ref,
                           ffng_ref, wr_ref, br_ref,
                           h1_ref, m_ref, eidx_ref, wts_ref):
    t = ROUTER_TILE
    ycn = _rms(yc_ref[...], bgc_ref[...])
    proj = (_dot(ya_ref[...], woa_ref[...]) + _dot(yb_ref[...], wob_ref[...])
            + _dot(ycn.astype(BF16), woc_ref[...]))
    h1 = h_ref[...] + proj
    h1_ref[...] = h1
    m = _rms(h1, ffng_ref[...])
    _store_row_tiled(m_ref, m)

    mh = m.astype(BF16)
    ml = (m - mh.astype(F32)).astype(BF16)
    wr = wr_ref[...]
    both = _dot(mh, wr)
    logits = both[:, :LANES] + both[:, LANES:] + _dot(ml, wr[:, :LANES]) + br_ref[...]
    lt = logits.T
    rowi = lax.broadcasted_iota(jnp.int32, (EXPERTS_PER_GROUP, t), 0)
    g8 = lt[0:8]
    gmax = jnp.max(g8, axis=0, keepdims=True)
    gsum = jnp.sum(jnp.exp(g8 - gmax), axis=0, keepdims=True)
    gidx = jnp.min(jnp.where(g8 == gmax, rowi, 8), axis=0, keepdims=True)
    g_w = 1.0 / gsum
    esel = jnp.zeros((EXPERTS_PER_GROUP, t), F32)
    for g in range(N_GROUPS):
        esel = jnp.where(gidx == g, lt[8 + g * EXPERTS_PER_GROUP:8 + (g + 1) * EXPERTS_PER_GROUP], esel)
    ep = jnp.exp(esel - jnp.max(esel, axis=0, keepdims=True))
    eprob = ep / jnp.sum(ep, axis=0, keepdims=True)
    v1 = jnp.max(eprob, axis=0, keepdims=True)
    i1 = jnp.min(jnp.where(eprob == v1, rowi, 8), axis=0, keepdims=True)
    rest = jnp.where(rowi == i1, -1.0, eprob)
    v2 = jnp.max(rest, axis=0, keepdims=True)
    i2 = jnp.min(jnp.where(rest == v2, rowi, 8), axis=0, keepdims=True)
    den = v1 + v2
    e1 = gidx * EXPERTS_PER_GROUP + i1
    e2 = gidx * EXPERTS_PER_GROUP + i2
    eidx_ref[...] = jnp.where(rowi == 0, e1, jnp.where(rowi == 1, e2, 0))
    wts_ref[...] = jnp.where(rowi == 0, g_w * (v1 / den), jnp.where(rowi == 1, g_w * (v2 / den), 0.0))


def _outproj_router(h, ya, yb, yc, lw):
    n = h.shape[0]
    t = ROUTER_TILE
    row = lambda w: pl.BlockSpec((t, w), lambda i: (i, 0))
    weights = [lw[k] for k in ("bg_c", "w_o_a", "w_o_b", "w_o_c", "ffn_g", "wr_split", "br")]
    colspec = pl.BlockSpec((8, t), lambda i: (0, i))
    return pl.pallas_call(
        _outproj_router_kernel,
        grid=(n // t,),
        in_specs=[row(D_MODEL), row(SGU_WIDTH), row(CONV_WIDTH), row(MLA_HEADS * MLA_V)]
        + [_full(w.shape) for w in weights],
        out_specs=(row(D_MODEL), pl.BlockSpec((t * ROW_TILES, LANES), lambda i: (i, 0)), colspec, colspec),
        out_shape=(
            jax.ShapeDtypeStruct((n, D_MODEL), F32),
            jax.ShapeDtypeStruct((n * ROW_TILES, LANES), F32),
            jax.ShapeDtypeStruct((8, n), jnp.int32),
            jax.ShapeDtypeStruct((8, n), F32),
        ),
        compiler_params=pltpu.CompilerParams(dimension_semantics=("arbitrary",), vmem_limit_bytes=VMEM_LIMIT),
        name="outproj_router",
    )(h, ya, yb, yc, *weights)


def _moe_rank_kernel(e_ref, upper_ref, ones_ref, ltri_ref, dest_ref, cnt_ref, run_ref, base_ref):
    phase = pl.program_id(0)
    step = pl.program_id(1)
    c = RANK_CHUNK
    expert = lax.broadcasted_iota(jnp.int32, (N_EXPERTS, c), 0)

    @pl.when(jnp.logical_and(phase == 0, step == 0))
    def _():
        run_ref[...] = jnp.zeros_like(run_ref)
        base_ref[...] = jnp.zeros_like(base_ref)

    @pl.when(jnp.logical_and(phase == 1, step == 0))
    def _():
        blocks = jnp.floor((run_ref[...] + (EXPERT_ROWS - 1)) * (1.0 / EXPERT_ROWS))
        base_ref[...] = _dot(ltri_ref[...], blocks.astype(BF16)) * EXPERT_ROWS
        run_ref[...] = jnp.zeros_like(run_ref)

    def chunk_onehot(sub):
        return jnp.where(expert == e_ref[:, sub * c:(sub + 1) * c], 1.0, 0.0)

    @pl.when(phase == 0)
    def _():
        dest_ref[...] = jnp.zeros_like(dest_ref)
        for sub in range(RANK_STEP // c):
            run_ref[...] = run_ref[...] + _dot(chunk_onehot(sub).astype(BF16), ones_ref[...])

    @pl.when(phase == 1)
    def _():
        for sub in range(RANK_STEP // c):
            onehot = chunk_onehot(sub)
            oh16 = onehot.astype(BF16)
            before = _dot(oh16, upper_ref[...])
            pos = before + run_ref[:, 0:1] + base_ref[:, 0:1]
            dest_ref[:, sub * c:(sub + 1) * c] = jnp.sum(onehot * pos, axis=0, keepdims=True).astype(jnp.int32)
            run_ref[...] = run_ref[...] + _dot(oh16, ones_ref[...])

    cnt_ref[...] = run_ref[...]


def _moe_rank(e_flat, consts):
    total = e_flat.shape[1]
    c = RANK_CHUNK
    st = RANK_STEP
    return pl.pallas_call(
        _moe_rank_kernel,
        grid=(2, total // st),
        in_specs=[pl.BlockSpec((1, st), lambda p, s: (0, s)),
                  _full((c, c)), _full((c, LANES)), _full((N_EXPERTS, N_EXPERTS))],
        out_specs=(pl.BlockSpec((1, st), lambda p, s: (0, s * p)),
                   pl.BlockSpec((N_EXPERTS, LANES), lambda p, s: (0, 0))),
        out_shape=(jax.ShapeDtypeStruct((1, total), jnp.int32),
                   jax.ShapeDtypeStruct((N_EXPERTS, LANES), F32)),
        scratch_shapes=[pltpu.VMEM((N_EXPERTS, LANES), F32), pltpu.VMEM((N_EXPERTS, LANES), F32)],
        compiler_params=pltpu.CompilerParams(dimension_semantics=("arbitrary", "arbitrary")),
        name="moe_rank",
    )(e_flat, consts["upper"], consts["ones"], consts["ltri"])


def _store_row_tiled(ref, x, offset=0):
    rows = x.shape[0]
    for s in range(ROW_TILES):
        ref[pl.ds(offset + s, rows, stride=ROW_TILES), :] = x[:, s * LANES:(s + 1) * LANES]


def _load_row_tiled(ref, rows, offset=0):
    return jnp.concatenate(
        [ref[pl.ds(offset + s, rows, stride=ROW_TILES), :] for s in range(ROW_TILES)], axis=1)


def _row_copy(src, src_row, dst, dst_row, sem):
    return pltpu.make_async_copy(
        src.at[pl.ds(pl.multiple_of(src_row * ROW_TILES, ROW_TILES), ROW_TILES)],
        dst.at[pl.ds(pl.multiple_of(dst_row * ROW_TILES, ROW_TILES), ROW_TILES)], sem)


def _rows_wait(src, dst, dst_row, rows, sem):
    pltpu.make_async_copy(
        src.at[pl.ds(0, rows * ROW_TILES)],
        dst.at[pl.ds(pl.multiple_of(dst_row * ROW_TILES, ROW_TILES), rows * ROW_TILES)], sem).wait()


def _moe_dispatch_kernel(dest_ref, cnt_ref, region_ref, m_ref, xs_ref, zbuf_ref, sem, zsem, *, n_tok):
    t = DISPATCH_TILE
    r = EXPERT_ROWS
    i = pl.program_id(0)

    @pl.when(i == 0)
    def _():
        zbuf_ref[...] = jnp.zeros_like(zbuf_ref)

        def zero_copy(e):
            last = pl.multiple_of((region_ref[e + 1] - r) * ROW_TILES, r * ROW_TILES)
            return pltpu.make_async_copy(zbuf_ref, xs_ref.at[pl.ds(last, r * ROW_TILES)], zsem)

        def start(e, carry):
            @pl.when(cnt_ref[e] > 0)
            def _():
                zero_copy(e).start()
            return carry

        def wait(e, carry):
            @pl.when(cnt_ref[e] > 0)
            def _():
                zero_copy(e).wait()
            return carry

        def tail_copy(blk):
            return pltpu.make_async_copy(
                zbuf_ref, xs_ref.at[pl.ds(pl.multiple_of(blk * (r * ROW_TILES), r * ROW_TILES), r * ROW_TILES)], zsem)

        def tail_start(blk, carry):
            tail_copy(blk).start()
            return carry

        def tail_wait(blk, carry):
            tail_copy(blk).wait()
            return carry

        first_unused = region_ref[N_EXPERTS] // r
        n_blocks = xs_ref.shape[0] // (r * ROW_TILES)
        lax.fori_loop(0, N_EXPERTS, start, 0)
        lax.fori_loop(first_unused, n_blocks, tail_start, 0)
        lax.fori_loop(0, N_EXPERTS, wait, 0)
        lax.fori_loop(first_unused, n_blocks, tail_wait, 0)

    per_iter = DMA_UNROLL // TOP_K

    def issue(g, carry):
        for u in range(per_iter):
            row = g * per_iter + u
            for k in range(TOP_K):
                _row_copy(m_ref, row, xs_ref, dest_ref[k * n_tok + i * t + row], sem).start(priority=k % 2)
        return carry

    lax.fori_loop(0, t // per_iter, issue, 0)
    for k in range(TOP_K):
        pltpu.make_async_copy(m_ref, xs_ref.at[pl.ds(0, t * ROW_TILES)], sem).wait()


def _moe_dispatch(dest, cnt, region, m_rt, rows):
    n = m_rt.shape[0] // ROW_TILES
    t = DISPATCH_TILE
    return pl.pallas_call(
        functools.partial(_moe_dispatch_kernel, n_tok=n),
        grid_spec=pltpu.PrefetchScalarGridSpec(
            num_scalar_prefetch=3,
            grid=(n // t,),
            in_specs=[pl.BlockSpec((t * ROW_TILES, LANES), lambda i, d, c, rg: (i, 0))],
            out_specs=pl.BlockSpec(memory_space=pl.ANY),
            scratch_shapes=[pltpu.VMEM((EXPERT_ROWS * ROW_TILES, LANES), F32),
                            pltpu.SemaphoreType.DMA(()), pltpu.SemaphoreType.DMA(())],
        ),
        out_shape=jax.ShapeDtypeStruct((rows * ROW_TILES, LANES), F32),
        compiler_params=pltpu.CompilerParams(dimension_semantics=("arbitrary",)),
        name="moe_dispatch",
    )(dest, cnt, region, m_rt)


def _moe_experts_kernel(blk_e_ref, nact_ref, x_ref, wg_ref, wu_ref, wd_ref, y_ref, wgu_ref, wdn_ref):
    r = EXPERT_ROWS
    b = pl.program_id(0)
    live = b < nact_ref[0]

    @pl.when(live)
    def _():
        @pl.when(jnp.logical_or(b == 0, blk_e_ref[b] != blk_e_ref[jnp.maximum(b - 1, 0)]))
        def _():
            wgu_ref[:, :D_EXPERT] = wg_ref[0, 0].astype(BF16)
            wgu_ref[:, D_EXPERT:] = wu_ref[0, 0].astype(BF16)
            wdn_ref[...] = wd_ref[0, 0].astype(BF16)

        x = _load_row_tiled(x_ref, r).astype(BF16)
        gu = _dot(x, wgu_ref[...])
        hb = jax.nn.silu(gu[:, :D_EXPERT]) * gu[:, D_EXPERT:]
        _store_row_tiled(y_ref, _dot(hb.astype(BF16), wdn_ref[...]))

    @pl.when(jnp.logical_not(live))
    def _():
        y_ref[...] = jnp.zeros_like(y_ref)


def _moe_experts(blk_e, nact, xs_rt, layer, w_gate, w_up, w_down):
    rows = xs_rt.shape[0] // ROW_TILES
    r = EXPERT_ROWS
    wspec = lambda k, n: pl.BlockSpec((1, 1, k, n), lambda b, be, na: (layer, be[b], 0, 0))
    return pl.pallas_call(
        _moe_experts_kernel,
        grid_spec=pltpu.PrefetchScalarGridSpec(
            num_scalar_prefetch=2,
            grid=(rows // r,),
            in_specs=[pl.BlockSpec((r * ROW_TILES, LANES), lambda b, be, na: (jnp.minimum(b, na[0] - 1), 0)),
                      wspec(D_MODEL, D_EXPERT), wspec(D_MODEL, D_EXPERT), wspec(D_EXPERT, D_MODEL)],
            out_specs=pl.BlockSpec((r * ROW_TILES, LANES), lambda b, be, na: (b, 0)),
            scratch_shapes=[pltpu.VMEM((D_MODEL, 2 * D_EXPERT), BF16),
                            pltpu.VMEM((D_EXPERT, D_MODEL), BF16)],
        ),
        out_shape=jax.ShapeDtypeStruct((rows * ROW_TILES, LANES), F32),
        compiler_params=pltpu.CompilerParams(dimension_semantics=("arbitrary",), vmem_limit_bytes=VMEM_LIMIT),
        name="moe_experts",
    )(blk_e, nact, xs_rt, w_gate, w_up, w_down)


def _combine_ple_kernel(dest_ref, h1_ref, wt_ref, p_ref, y_ref, pleg_ref, gatew_ref, projw_ref, postg_ref,
                        out_ref, ybuf_ref, sem, *, n_tok):
    t = COMBINE_TILE
    i = pl.program_id(0)

    def gather(step, slot):
        def issue(g, carry):
            for u in range(DMA_UNROLL // TOP_K):
                r = g * (DMA_UNROLL // TOP_K) + u
                for k in range(TOP_K):
                    _row_copy(y_ref, dest_ref[k * n_tok + step * t + r], ybuf_ref, (slot * TOP_K + k) * t + r,
                              sem.at[slot]).start(priority=k % 2)
            return carry
        lax.fori_loop(0, t // (DMA_UNROLL // TOP_K), issue, 0)

    @pl.when(i == 0)
    def _():
        gather(0, 0)

    @pl.when(i + 1 < pl.num_programs(0))
    def _():
        gather(i + 1, (i + 1) % 2)

    e = _rms(_dot(p_ref[...].astype(BF16), projw_ref[...]), postg_ref[...])
    slot = i % 2
    _rows_wait(y_ref, ybuf_ref, slot * TOP_K * t, TOP_K * t, sem.at[slot])
    wt = wt_ref[...].T
    y0 = _load_row_tiled(ybuf_ref, t, (slot * TOP_K) * (t * ROW_TILES))
    y1 = _load_row_tiled(ybuf_ref, t, (slot * TOP_K + 1) * (t * ROW_TILES))
    h2 = h1_ref[...] + (wt[:, 0:1] * y0 + wt[:, 1:2] * y1)
    gate = jax.nn.sigmoid(_dot(_rms(h2, pleg_ref[...]).astype(BF16), gatew_ref[...]))
    out_ref[...] = h2 + gate * e


def _combine_ple(dest, h1, wt, p, y_rt, lw):
    n = h1.shape[0]
    t = COMBINE_TILE
    row = lambda w: pl.BlockSpec((t, w), lambda i, d: (i, 0))
    layer_rows = lw["layer"] * (n // t)
    weights = [lw[k] for k in ("ple_g", "gate_w", "proj_w", "post_g")]
    return pl.pallas_call(
        functools.partial(_combine_ple_kernel, n_tok=n),
        grid_spec=pltpu.PrefetchScalarGridSpec(
            num_scalar_prefetch=1,
            grid=(n // t,),
            in_specs=[row(D_MODEL), pl.BlockSpec((SUBLANES, t), lambda i, d: (0, i)), pl.BlockSpec((t, p.shape[1]), lambda i, d: (layer_rows + i, 0)),
                      pl.BlockSpec(memory_space=pl.ANY)]
            + [pl.BlockSpec(w.shape, lambda i, d, nd=w.ndim: (0,) * nd) for w in weights],
            out_specs=row(D_MODEL),
            scratch_shapes=[pltpu.VMEM((2 * TOP_K * t * ROW_TILES, LANES), F32), pltpu.SemaphoreType.DMA((2,))],
        ),
        out_shape=jax.ShapeDtypeStruct((n, D_MODEL), F32),
        compiler_params=pltpu.CompilerParams(dimension_semantics=("arbitrary",), vmem_limit_bytes=VMEM_LIMIT),
        name="combine_ple",
    )(dest, h1, wt, p, y_rt, *weights)


def _segment_matrix(seg_ids):
    seg_ids = jnp.asarray(seg_ids)
    same = (seg_ids[:, None] == seg_ids[None, :]).astype(F32)
    return (same / jnp.sum(same, axis=1, keepdims=True)).astype(BF16)


def _constants():
    lane = jnp.arange(256)
    qk_seg = (lane // HEAD_PAD) * 3 + jnp.where(lane % HEAD_PAD < MLA_NOPE, 0, jnp.where(lane % HEAD_PAD < MLA_QK, 1, 2))
    i = jnp.arange(RANK_CHUNK)
    e = jnp.arange(N_EXPERTS)
    return {
        "g64": _segment_matrix(lane // HEAD_DIM),
        "gqk": _segment_matrix(qk_seg),
        "upper": (i[:, None] < i[None, :]).astype(BF16),
        "ones": jnp.ones((RANK_CHUNK, LANES), BF16),
        "ltri": (e[None, :] < e[:, None]).astype(BF16),
    }


def _layer_weights(i, consts, p):
    row = lambda v: v.reshape(1, -1).astype(F32)
    w_in = p["w_in"][i]
    o_sgu, o_conv, o_q, o_kv = 2 * SGU_WIDTH, 2 * SGU_WIDTH + 2 * CONV_WIDTH, 0, 0
    o_q = o_conv + Q_RANK
    o_kv = o_q + KV_RANK
    w_ckv = jnp.concatenate([w_in[:, o_q:], jnp.zeros((D_MODEL, 256 - KV_RANK - MLA_ROPE), F32)], axis=1)

    causal = jnp.tril(jnp.ones((CHUNK, CHUNK), F32))
    wcat = jnp.transpose(p["sgu_w"][i] * causal, (1, 0, 2)).reshape(CHUNK, SGU_HEADS * CHUNK)
    sbias = jnp.broadcast_to(jnp.transpose(p["sgu_b"][i])[:, :, None],
                             (CHUNK, SGU_HEADS, HEAD_DIM)).reshape(CHUNK, SGU_WIDTH)

    w_uq = p["w_uq"][i].reshape(Q_RANK, MLA_HEADS, MLA_QK)
    w_uq = jnp.pad(w_uq, ((0, 0), (0, 0), (0, HEAD_PAD - MLA_QK))).reshape(Q_RANK, MLA_HEADS * HEAD_PAD)
    qn_g = p["q_norm_g"][i]
    q_gain = jnp.tile(jnp.concatenate([qn_g, jnp.zeros((HEAD_PAD - MLA_QK,), F32)]), MLA_HEADS)
    w_ukv = p["w_ukv"][i].reshape(KV_RANK, MLA_HEADS, MLA_NOPE + MLA_V)
    w_ukt = jnp.transpose(w_ukv[:, :, :MLA_NOPE], (1, 2, 0)).reshape(MLA_HEADS * MLA_NOPE, KV_RANK)
    w_v = w_ukv[:, :, MLA_NOPE:].reshape(KV_RANK, MLA_HEADS // 2, 2, MLA_V)
    zeros_v = jnp.zeros((KV_RANK, MLA_HEADS // 2, MLA_V), F32)
    w_uv = jnp.stack([w_v[:, :, 0], zeros_v, zeros_v, w_v[:, :, 1]], axis=2).reshape(KV_RANK, MLA_HEADS * HEAD_PAD)
    v_ones = jnp.tile(jnp.repeat(jnp.array([0.0, 1.0, 1.0, 0.0], F32), MLA_V), MLA_HEADS // 2)
    kn_g = p["k_norm_g"][i]

    bg = p["branch_norm_g"][i]
    w_o = p["w_o"][i]
    wr = jnp.concatenate([p["router_group_w"][i], jnp.zeros((D_MODEL, 8 - N_GROUPS), F32),
                          p["router_expert_w"][i], jnp.zeros((D_MODEL, LANES - 8 - N_EXPERTS), F32)], axis=1)
    wr_hi = wr.astype(BF16)
    br = jnp.concatenate([p["router_group_b"][i], jnp.full((8 - N_GROUPS,), NEG_BIG, F32),
                          p["router_expert_b"][i], jnp.zeros((LANES - 8 - N_EXPERTS,), F32)])
    return {
        "mix_g": row(p["mix_norm_g"][i]),
        "w_sgu": w_in[:, :o_sgu].astype(BF16),
        "w_conv": w_in[:, o_sgu:o_conv].astype(BF16),
        "w_cq": w_in[:, o_conv:o_q].astype(BF16),
        "w_ckv": w_ckv.astype(BF16),
        "sgu_ln_g": row(p["sgu_ln_g"][i]), "sgu_ln_b": row(p["sgu_ln_b"][i]),
        "sgu_wcat": wcat.astype(BF16), "sgu_bias": sbias, "g64": consts["g64"], "bg_a": row(bg[:SGU_WIDTH]),
        "conv_w": jnp.pad(p["conv_w"][i], ((0, 1), (0, 0))), "conv_b": row(p["conv_b"][i]),
        "conv_ln_g": row(p["conv_ln_g"][i]), "conv_ln_b": row(p["conv_ln_b"][i]),
        "pw_w": p["conv_pw_w"][i].astype(BF16), "pw_b": row(p["conv_pw_b"][i]),
        "bg_b": row(bg[SGU_WIDTH:SGU_WIDTH + CONV_WIDTH]),
        "qa_g": row(p["q_a_norm_g"][i]), "w_uq": w_uq.astype(BF16), "gqk": consts["gqk"],
        "q_gain": row(q_gain),
        "kva_g": row(p["kv_a_norm_g"][i]), "w_ukt": w_ukt.astype(BF16), "w_uv": w_uv.astype(BF16),
        "v_ones": row(v_ones),
        "kn_g": jnp.broadcast_to(kn_g[:MLA_NOPE, None], (MLA_NOPE, MIX_TILE)),
        "kpe_g": jnp.broadcast_to(kn_g[MLA_NOPE:, None], (MLA_ROPE, MIX_TILE)),
        "bg_c": row(bg[SGU_WIDTH + CONV_WIDTH:]),
        "w_o_a": w_o[:SGU_WIDTH].astype(BF16),
        "w_o_b": w_o[SGU_WIDTH:SGU_WIDTH + CONV_WIDTH].astype(BF16),
        "w_o_c": w_o[SGU_WIDTH + CONV_WIDTH:].astype(BF16),
        "ffn_g": row(p["ffn_norm_g"][i]),
        "wr_split": jnp.concatenate([wr_hi, (wr - wr_hi.astype(F32)).astype(BF16)], axis=1), "br": row(br),
        "layer": i, "w_gate": p["moe_w_gate"], "w_up": p["moe_w_up"], "w_down": p["moe_w_down"],
        "ple_g": row(p["ple_norm_g"][i]), "gate_w": p["ple_gate_w"][i].astype(BF16),
        "proj_w": p["ple_proj_w"][i].astype(BF16), "post_g": row(p["ple_post_norm_g"][i]),
    }


def _moe(h1, m, eidx, wts, pl_i, lw, consts):
    n = h1.shape[0]
    total = TOP_K * n
    rows = total + N_EXPERTS * EXPERT_ROWS
    dest2d, counts = _moe_rank(eidx[:TOP_K].reshape(1, total), consts)
    dest = dest2d.reshape(total)
    cnt = counts[:, 0].astype(jnp.int32)
    padded = (cnt + EXPERT_ROWS - 1) // EXPERT_ROWS * EXPERT_ROWS
    pend = jnp.cumsum(padded)
    nblk = rows // EXPERT_ROWS
    blk_start = jnp.arange(nblk, dtype=jnp.int32) * EXPERT_ROWS
    blk_e = jnp.minimum(jnp.sum((pend[None, :] <= blk_start[:, None]).astype(jnp.int32), axis=1), N_EXPERTS - 1)
    nact = (pend[-1:] // EXPERT_ROWS).astype(jnp.int32)
    region = jnp.concatenate([jnp.zeros((1,), jnp.int32), pend.astype(jnp.int32)])
    xs = _moe_dispatch(dest, cnt, region, m, rows)
    y = _moe_experts(blk_e, nact, xs, lw["layer"], lw["w_gate"], lw["w_up"], lw["w_down"])
    return _combine_ple(dest, h1, wts, pl_i, y, lw)


def kernel(x, p, positions, mix_norm_g, w_in, sgu_ln_g, sgu_ln_b, sgu_w, sgu_b, conv_w, conv_b, conv_ln_g, conv_ln_b, conv_pw_w, conv_pw_b, q_a_norm_g, w_uq, kv_a_norm_g, w_ukv, q_norm_g, k_norm_g, branch_norm_g, w_o, ffn_norm_g, router_group_w, router_group_b, router_expert_w, router_expert_b, moe_w_gate, moe_w_up, moe_w_down, ple_norm_g, ple_gate_w, ple_proj_w, ple_post_norm_g):
    params = dict(
        mix_norm_g=mix_norm_g, w_in=w_in, sgu_ln_g=sgu_ln_g, sgu_ln_b=sgu_ln_b, sgu_w=sgu_w, sgu_b=sgu_b,
        conv_w=conv_w, conv_b=conv_b, conv_ln_g=conv_ln_g, conv_ln_b=conv_ln_b, conv_pw_w=conv_pw_w,
        conv_pw_b=conv_pw_b, q_a_norm_g=q_a_norm_g, w_uq=w_uq, kv_a_norm_g=kv_a_norm_g, w_ukv=w_ukv,
        q_norm_g=q_norm_g, k_norm_g=k_norm_g, branch_norm_g=branch_norm_g, w_o=w_o, ffn_norm_g=ffn_norm_g,
        router_group_w=router_group_w, router_group_b=router_group_b, router_expert_w=router_expert_w,
        router_expert_b=router_expert_b, moe_w_gate=moe_w_gate, moe_w_up=moe_w_up, moe_w_down=moe_w_down,
        ple_norm_g=ple_norm_g, ple_gate_w=ple_gate_w, ple_proj_w=ple_proj_w, ple_post_norm_g=ple_post_norm_g)
    batch, seq, d = x.shape
    n = batch * seq
    depth = w_in.shape[0]
    consts = _constants()
    cos, sin = _rope_tables(positions)
    h = x.reshape(n, d)
    p_rows = p.reshape(depth * n, p.shape[-1])
    for i in range(depth):
        lw = _layer_weights(i, consts, params)
        ya, yb, q, kt, v = _mixer_pre(h, cos, sin, lw, batch, seq)
        yc = _attention(q, kt, v, batch, seq)
        h1, m, eidx, wts = _outproj_router(h, ya, yb, yc, lw)
        h = _moe(h1, m, eidx, wts, p_rows, lw, consts)
    return h.reshape(batch, seq, d)
```

```python
import functools
import math

import jax
import jax.numpy as jnp
from jax import lax
from jax.experimental import pallas as pl
from jax.experimental.pallas import tpu as pltpu

F32 = jnp.float32
BF16 = jnp.bfloat16

D_MODEL = 1024
HEAD_DIM = 64
SGU_HEADS = 4
SGU_WIDTH = 256
CHUNK = 128
CONV_WIDTH = 256
CONV_KERNEL = 31
MLA_HEADS = 8
MLA_NOPE = 64
MLA_ROPE = 32
MLA_QK = MLA_NOPE + MLA_ROPE
MLA_V = 64
Q_RANK = 256
KV_RANK = 128
ROPE_THETA = 10000.0
N_GROUPS = 4
EXPERTS_PER_GROUP = 8
N_EXPERTS = 32
D_EXPERT = 256
TOP_K = 2
EPS = 1e-6

LANES = 128
SUBLANES = 8
HEAD_PAD = 128
MIX_TILE = 1024
ROUTER_TILE = 1024
CONV_HALO = 32
CONV_SHIFTED_ROWS = MIX_TILE + CONV_HALO - SUBLANES
MIXER_ORDER = ("q", "kv", "sgu", "conv")
ATT_TQ = 256
ATT_LOOKAHEAD = 1
RANK_CHUNK = 512
RANK_STEP = 2048
EXPERT_ROWS = 512
DISPATCH_TILE = 1024
COMBINE_TILE = 512
ROW_TILES = D_MODEL // LANES
DMA_UNROLL = 8
VMEM_LIMIT = 48 * 1024 * 1024
NEG_BIG = -1e30


def _dot(a, b):
    return jnp.dot(a, b, preferred_element_type=F32)


def _rms(x, g):
    ms = jnp.mean(x * x, axis=-1, keepdims=True)
    return x * lax.rsqrt(ms + EPS) * g


def _seg_mean(x, g_ref, split=True):
    g = g_ref[...]
    outs = []
    for c in range(x.shape[1] // 256):
        xb = x[:, c * 256:(c + 1) * 256]
        hi = xb.astype(BF16)
        acc = _dot(hi, g)
        if split:
            acc = acc + _dot((xb - hi.astype(F32)).astype(BF16), g)
        outs.append(acc)
    return outs[0] if len(outs) == 1 else jnp.concatenate(outs, axis=-1)


def _group_layernorm(x, g_ref, gain, bias):
    d = x - _seg_mean(x, g_ref)
    var = _seg_mean(d * d, g_ref)
    return d * lax.rsqrt(var + EPS) * gain + bias


def _rope_table_kernel(pos_ref, invf_ref, cos_ref, sin_ref):
    ang = pos_ref[...].astype(F32) * invf_ref[...]
    cos_ref[...] = jnp.cos(ang)
    sin_ref[...] = jnp.sin(ang)


def _rope_tables(positions):
    b, s = positions.shape
    n = b * s
    half = MLA_ROPE // 2
    inv_freq = ROPE_THETA ** (-jnp.arange(0, MLA_ROPE, 2, dtype=F32) / MLA_ROPE)
    pos_rep = jnp.broadcast_to(positions.reshape(n, 1), (n, half)).reshape(n * half // LANES, LANES)
    invf_rep = jnp.tile(inv_freq, LANES // half).reshape(1, LANES)
    shape = jax.ShapeDtypeStruct(pos_rep.shape, F32)
    cos, sin = pl.pallas_call(_rope_table_kernel, out_shape=(shape, shape), name="rope_tables")(pos_rep, invf_rep)
    return cos.reshape(n, half), sin.reshape(n, half)


def _mixer_pre_kernel(
        h_ref, cos_ref, sin_ref,
        mixg_ref, wsgu_ref, wconv_ref, wcq_ref, wckv_ref,
        slng_ref, slnb_ref, wcat_ref, sbias_ref, g64_ref, bga_ref,
        cw_ref, cb_ref, clng_ref, clnb_ref, pww_ref, pwb_ref, bgb_ref,
        qag_ref, wuq_ref, gqk_ref, qgain_ref,
        kvag_ref, wukt_ref, wuv_ref, vones_ref, kng_ref, kpeg_ref,
        ya_ref, yb_ref, q_ref, kt_ref, v_ref,
        ybuf_ref, ysh_ref, *, tiles_per_seq):
    t = MIX_TILE
    a = _rms(h_ref[...], mixg_ref[...]).astype(BF16)
    zs = _dot(a, wsgu_ref[...])
    zc = _dot(a, wconv_ref[...])
    zq = _dot(a, wcq_ref[...])
    zkv = _dot(a, wckv_ref[...])

    half = MLA_ROPE // 2
    cos, sin = cos_ref[...], sin_ref[...]
    zn = jnp.zeros((t, MLA_NOPE), F32)
    zh = jnp.zeros((t, half), F32)
    zp = jnp.zeros((t, HEAD_PAD - MLA_QK), F32)
    rc = jnp.concatenate([jnp.ones((t, MLA_NOPE), F32), cos, cos, zp], axis=1)
    rs1 = jnp.concatenate([zn, -sin, zh, zp], axis=1)
    rs2 = jnp.concatenate([zn, zh, sin, zp], axis=1)

    def sgu_branch():
        zg = jax.nn.gelu(zs)
        u = zg[:, :SGU_WIDTH]
        vn = _group_layernorm(zg[:, SGU_WIDTH:], g64_ref, slng_ref[...], slnb_ref[...])
        lane_head = lax.broadcasted_iota(jnp.int32, (CHUNK, SGU_WIDTH), 1) // HEAD_DIM
        wcat = wcat_ref[...]
        sbias = sbias_ref[...]
        parts = []
        for c in range(t // CHUNK):
            vc = vn[c * CHUNK:(c + 1) * CHUNK]
            stacked = jnp.concatenate(
                [jnp.where(lane_head == hh, vc, 0.0) for hh in range(SGU_HEADS)], axis=0).astype(BF16)
            s = _dot(wcat, stacked) + sbias
            parts.append(u[c * CHUNK:(c + 1) * CHUNK] * s)
        ya_ref[...] = _rms(jnp.concatenate(parts, axis=0), bga_ref[...]).astype(BF16)

    def conv_branch():
        yg = zc[:, :CONV_WIDTH] * jax.nn.sigmoid(zc[:, CONV_WIDTH:])
        first = (pl.program_id(0) % tiles_per_seq) == 0

        @pl.when(first)
        def _():
            ybuf_ref[0:CONV_HALO, :] = jnp.zeros((CONV_HALO, CONV_WIDTH), F32)

        @pl.when(jnp.logical_not(first))
        def _():
            ybuf_ref[0:CONV_HALO, :] = ybuf_ref[t:t + CONV_HALO, :]

        ybuf_ref[CONV_HALO:CONV_HALO + t, :] = yg
        for sh in range(1, SUBLANES):
            ysh_ref[sh - 1] = ybuf_ref[pl.ds(sh, CONV_SHIFTED_ROWS), :]
        rows = 64
        first_tap_row = CONV_HALO - (CONV_KERNEL - 1)
        conv_parts = []
        for r in range(t // rows):
            acc = jnp.broadcast_to(cb_ref[...], (rows, CONV_WIDTH))
            for tap in range(CONV_KERNEL):
                sh = (first_tap_row + tap) % SUBLANES
                start = r * rows + first_tap_row + tap - sh
                src = ybuf_ref[pl.ds(start, rows), :] if sh == 0 else ysh_ref[sh - 1, pl.ds(start, rows), :]
                acc = acc + cw_ref[tap:tap + 1, :] * src
            conv_parts.append(acc)
        cv = jnp.concatenate(conv_parts, axis=0)
        cn = _group_layernorm(cv, g64_ref, clng_ref[...], clnb_ref[...])
        yb = _dot(jax.nn.silu(cn).astype(BF16), pww_ref[...]) + pwb_ref[...]
        yb_ref[...] = _rms(yb, bgb_ref[...]).astype(BF16)

    def query_branch():
        cqn = _rms(zq, qag_ref[...]).astype(BF16)
        qf = _dot(cqn, wuq_ref[...])
        qn = qf * lax.rsqrt(_seg_mean(qf * qf, gqk_ref, split=False) + EPS) * qgain_ref[...]
        for hh in range(MLA_HEADS):
            blk = qn[:, hh * HEAD_PAD:(hh + 1) * HEAD_PAD]
            rot = blk * rc + pltpu.roll(blk, HEAD_PAD - half, 1) * rs1 + pltpu.roll(blk, half, 1) * rs2
            q_ref[:, hh * HEAD_PAD:(hh + 1) * HEAD_PAD] = rot.astype(BF16)

    def key_value_branch():
        ckvn = _rms(zkv[:, :KV_RANK], kvag_ref[...]).astype(BF16)
        v_ref[...] = (_dot(ckvn, wuv_ref[...]) + vones_ref[...]).astype(BF16)
        knt = lax.dot_general(wukt_ref[...], ckvn, (((1,), (1,)), ((), ())), preferred_element_type=F32)
        x = zkv[:, KV_RANK:].T[0:MLA_ROPE]
        xn = x * lax.rsqrt(jnp.mean(x * x, axis=0, keepdims=True) + EPS) * kpeg_ref[...]
        x1, x2 = xn[:MLA_ROPE // 2], xn[MLA_ROPE // 2:]
        cos_t = rc.T[MLA_NOPE:MLA_NOPE + half]
        sin_t = rs2.T[MLA_NOPE + half:MLA_QK]
        kpe = jnp.concatenate([x1 * cos_t - x2 * sin_t, x2 * cos_t + x1 * sin_t], axis=0)
        pad = jnp.zeros((HEAD_PAD - MLA_QK, t), F32)
        kng = kng_ref[...]
        for hh in range(MLA_HEADS):
            blk = knt[hh * MLA_NOPE:(hh + 1) * MLA_NOPE]
            kn = blk * lax.rsqrt(jnp.mean(blk * blk, axis=0, keepdims=True) + EPS) * kng
            kt_ref[0, hh * HEAD_PAD:(hh + 1) * HEAD_PAD, :] = jnp.concatenate([kn, kpe, pad], axis=0).astype(BF16)

    branches = {"sgu": sgu_branch, "conv": conv_branch, "q": query_branch, "kv": key_value_branch}
    for name in MIXER_ORDER:
        branches[name]()


def _full(shape):
    nd = len(shape)
    return pl.BlockSpec(shape, lambda *_: (0,) * nd)


def _mixer_pre(h, cos, sin, lw, batch, seq):
    n = h.shape[0]
    t = MIX_TILE
    tps = seq // t
    row = lambda w: pl.BlockSpec((t, w), lambda i: (i, 0))
    weights = [lw[k] for k in (
        "mix_g", "w_sgu", "w_conv", "w_cq", "w_ckv",
        "sgu_ln_g", "sgu_ln_b", "sgu_wcat", "sgu_bias", "g64", "bg_a",
        "conv_w", "conv_b", "conv_ln_g", "conv_ln_b", "pw_w", "pw_b", "bg_b",
        "qa_g", "w_uq", "gqk", "q_gain",
        "kva_g", "w_ukt", "w_uv", "v_ones", "kn_g", "kpe_g")]
    in_specs = [row(D_MODEL), row(MLA_ROPE // 2), row(MLA_ROPE // 2)] + [_full(w.shape) for w in weights]
    out_shape = (
        jax.ShapeDtypeStruct((n, SGU_WIDTH), BF16),
        jax.ShapeDtypeStruct((n, CONV_WIDTH), BF16),
        jax.ShapeDtypeStruct((n, MLA_HEADS * HEAD_PAD), BF16),
        jax.ShapeDtypeStruct((batch, MLA_HEADS * HEAD_PAD, seq), BF16),
        jax.ShapeDtypeStruct((n, MLA_HEADS * HEAD_PAD), BF16),
    )
    out_specs = (
        row(SGU_WIDTH), row(CONV_WIDTH), row(MLA_HEADS * HEAD_PAD),
        pl.BlockSpec((1, MLA_HEADS * HEAD_PAD, t), lambda i: (i // tps, 0, i % tps)),
        row(MLA_HEADS * HEAD_PAD),
    )
    return pl.pallas_call(
        functools.partial(_mixer_pre_kernel, tiles_per_seq=tps),
        grid=(n // t,),
        in_specs=in_specs,
        out_specs=out_specs,
        out_shape=out_shape,
        scratch_shapes=[pltpu.VMEM((t + CONV_HALO, CONV_WIDTH), F32),
                        pltpu.VMEM((SUBLANES - 1, CONV_SHIFTED_ROWS, CONV_WIDTH), F32)],
        compiler_params=pltpu.CompilerParams(dimension_semantics=("arbitrary",), vmem_limit_bytes=VMEM_LIMIT),
        name="mixer_pre",
    )(h, cos, sin, *weights)


def _attention_kernel(q_ref, kt_ref, v_ref, o_ref, *, seq):
    tq = ATT_TQ
    exp2_scale = MLA_QK ** -0.5 * math.log2(math.e)
    row = lax.broadcasted_iota(jnp.int32, (tq, tq), 0)
    col = lax.broadcasted_iota(jnp.int32, (tq, tq), 1)
    lane = lax.broadcasted_iota(jnp.int32, (tq, HEAD_PAD), 1)
    def scores(qi, hh):
        nk = (qi + 1) * tq
        q = q_ref[qi * tq:(qi + 1) * tq, hh * HEAD_PAD:(hh + 1) * HEAD_PAD]
        s = _dot(q, kt_ref[0, hh * HEAD_PAD:(hh + 1) * HEAD_PAD, 0:nk])
        diag = jnp.where(col <= row, s[:, nk - tq:], NEG_BIG)
        return diag if qi == 0 else jnp.concatenate([s[:, :nk - tq], diag], axis=1)

    n_blocks = seq // tq
    ahead = [scores(0, hh) for hh in range(2)]
    for qi in range(n_blocks):
        s_pair, ahead, p_pair = ahead, [], []
        for hh in range(2):
            if qi + 1 < n_blocks:
                ahead.append(scores(qi + 1, hh))
            s = s_pair[hh]
            p_pair.append(jnp.exp2(((s - jnp.max(s, axis=-1, keepdims=True)) * exp2_scale).astype(BF16)))
        acc = _dot(jnp.concatenate(p_pair, axis=0), v_ref[0:(qi + 1) * tq, :])
        a0, a1 = acc[:tq, :HEAD_PAD], acc[tq:, HEAD_PAD:]
        o_ref[qi * tq:(qi + 1) * tq, :] = jnp.where(
            lane < MLA_V, a0 / pltpu.roll(a0, MLA_V, 1), a1 / pltpu.roll(a1, MLA_V, 1))


def _attention(q, kt, v, batch, seq):
    n = q.shape[0]
    pairs = MLA_HEADS // 2
    return pl.pallas_call(
        functools.partial(_attention_kernel, seq=seq),
        grid=(batch, pairs),
        in_specs=[
            pl.BlockSpec((seq, 2 * HEAD_PAD), lambda b, p: (b, p)),
            pl.BlockSpec((1, 2 * HEAD_PAD, seq), lambda b, p: (b, p, 0)),
            pl.BlockSpec((seq, 2 * HEAD_PAD), lambda b, p: (b, p)),
        ],
        out_specs=pl.BlockSpec((seq, 2 * MLA_V), lambda b, p: (b, p)),
        out_shape=jax.ShapeDtypeStruct((n, MLA_HEADS * MLA_V), F32),
        compiler_params=pltpu.CompilerParams(
            dimension_semantics=("arbitrary", "arbitrary"), vmem_limit_bytes=VMEM_LIMIT),
        name="attention",
    )(q, kt, v)


def _outproj_router_kernel(h_ref, ya_ref, yb_ref, yc_ref, bgc_ref, woa_ref, wob_ref, woc_ref,
                           ffng_ref, wr_ref, br_ref,
                           h1_ref, m_ref, eidx_ref, wts_ref):
    t = ROUTER_TILE
    ycn = _rms(yc_ref[...], bgc_ref[...])
    proj = (_dot(ya_ref[...], woa_ref[...]) + _dot(yb_ref[...], wob_ref[...])
            + _dot(ycn.astype(BF16), woc_ref[...]))
    h1 = h_ref[...] + proj
    h1_ref[...] = h1
    m = _rms(h1, ffng_ref[...])
    _store_row_tiled(m_ref, m)

    mh = m.astype(BF16)
    ml = (m - mh.astype(F32)).astype(BF16)
    wr = wr_ref[...]
    both = _dot(mh, wr)
    logits = both[:, :LANES] + both[:, LANES:] + _dot(ml, wr[:, :LANES]) + br_ref[...]
    lt = logits.T
    rowi = lax.broadcasted_iota(jnp.int32, (EXPERTS_PER_GROUP, t), 0)
    g8 = lt[0:8]
    gmax = jnp.max(g8, axis=0, keepdims=True)
    gsum = jnp.sum(jnp.exp(g8 - gmax), axis=0, keepdims=True)
    gidx = jnp.min(jnp.where(g8 == gmax, rowi, 8), axis=0, keepdims=True)
    g_w = 1.0 / gsum
    esel = jnp.zeros((EXPERTS_PER_GROUP, t), F32)
    for g in range(N_GROUPS):
        esel = jnp.where(gidx == g, lt[8 + g * EXPERTS_PER_GROUP:8 + (g + 1) * EXPERTS_PER_GROUP], esel)
    ep = jnp.exp(esel - jnp.max(esel, axis=0, keepdims=True))
    eprob = ep / jnp.sum(ep, axis=0, keepdims=True)
    v1 = jnp.max(eprob, axis=0, keepdims=True)
    i1 = jnp.min(jnp.where(eprob == v1, rowi, 8), axis=0, keepdims=True)
    rest = jnp.where(rowi == i1, -1.0, eprob)
    v2 = jnp.max(rest, axis=0, keepdims=True)
    i2 = jnp.min(jnp.where(rest == v2, rowi, 8), axis=0, keepdims=True)
    den = v1 + v2
    e1 = gidx * EXPERTS_PER_GROUP + i1
    e2 = gidx * EXPERTS_PER_GROUP + i2
    eidx_ref[...] = jnp.where(rowi == 0, e1, jnp.where(rowi == 1, e2, 0))
    wts_ref[...] = jnp.where(rowi == 0, g_w * (v1 / den), jnp.where(rowi == 1, g_w * (v2 / den), 0.0))


def _outproj_router(h, ya, yb, yc, lw):
    n = h.shape[0]
    t = ROUTER_TILE
    row = lambda w: pl.BlockSpec((t, w), lambda i: (i, 0))
    weights = [lw[k] for k in ("bg_c", "w_o_a", "w_o_b", "w_o_c", "ffn_g", "wr_split", "br")]
    colspec = pl.BlockSpec((8, t), lambda i: (0, i))
    return pl.pallas_call(
        _outproj_router_kernel,
        grid=(n // t,),
        in_specs=[row(D_MODEL), row(SGU_WIDTH), row(CONV_WIDTH), row(MLA_HEADS * MLA_V)]
        + [_full(w.shape) for w in weights],
        out_specs=(row(D_MODEL), pl.BlockSpec((t * ROW_TILES, LANES), lambda i: (i, 0)), colspec, colspec),
        out_shape=(
            jax.ShapeDtypeStruct((n, D_MODEL), F32),
            jax.ShapeDtypeStruct((n * ROW_TILES, LANES), F32),
            jax.ShapeDtypeStruct((8, n), jnp.int32),
            jax.ShapeDtypeStruct((8, n), F32),
        ),
        compiler_params=pltpu.CompilerParams(dimension_semantics=("arbitrary",), vmem_limit_bytes=VMEM_LIMIT),
        name="outproj_router",
    )(h, ya, yb, yc, *weights)


def _moe_rank_kernel(e_ref, upper_ref, ones_ref, ltri_ref, dest_ref, cnt_ref, run_ref, base_ref):
    phase = pl.program_id(0)
    step = pl.program_id(1)
    c = RANK_CHUNK
    expert = lax.broadcasted_iota(jnp.int32, (N_EXPERTS, c), 0)

    @pl.when(jnp.logical_and(phase == 0, step == 0))
    def _():
        run_ref[...] = jnp.zeros_like(run_ref)
        base_ref[...] = jnp.zeros_like(base_ref)

    @pl.when(jnp.logical_and(phase == 1, step == 0))
    def _():
        blocks = jnp.floor((run_ref[...] + (EXPERT_ROWS - 1)) * (1.0 / EXPERT_ROWS))
        base_ref[...] = _dot(ltri_ref[...], blocks.astype(BF16)) * EXPERT_ROWS
        run_ref[...] = jnp.zeros_like(run_ref)

    def chunk_onehot(sub):
        return jnp.where(expert == e_ref[:, sub * c:(sub + 1) * c], 1.0, 0.0)

    @pl.when(phase == 0)
    def _():
        dest_ref[...] = jnp.zeros_like(dest_ref)
        for sub in range(RANK_STEP // c):
            run_ref[...] = run_ref[...] + _dot(chunk_onehot(sub).astype(BF16), ones_ref[...])

    @pl.when(phase == 1)
    def _():
        for sub in range(RANK_STEP // c):
            onehot = chunk_onehot(sub)
            oh16 = onehot.astype(BF16)
            before = _dot(oh16, upper_ref[...])
            pos = before + run_ref[:, 0:1] + base_ref[:, 0:1]
            dest_ref[:, sub * c:(sub + 1) * c] = jnp.sum(onehot * pos, axis=0, keepdims=True).astype(jnp.int32)
            run_ref[...] = run_ref[...] + _dot(oh16, ones_ref[...])

    cnt_ref[...] = run_ref[...]


def _moe_rank(e_flat, consts):
    total = e_flat.shape[1]
    c = RANK_CHUNK
    st = RANK_STEP
    return pl.pallas_call(
        _moe_rank_kernel,
        grid=(2, total // st),
        in_specs=[pl.BlockSpec((1, st), lambda p, s: (0, s)),
                  _full((c, c)), _full((c, LANES)), _full((N_EXPERTS, N_EXPERTS))],
        out_specs=(pl.BlockSpec((1, st), lambda p, s: (0, s * p)),
                   pl.BlockSpec((N_EXPERTS, LANES), lambda p, s: (0, 0))),
        out_shape=(jax.ShapeDtypeStruct((1, total), jnp.int32),
                   jax.ShapeDtypeStruct((N_EXPERTS, LANES), F32)),
        scratch_shapes=[pltpu.VMEM((N_EXPERTS, LANES), F32), pltpu.VMEM((N_EXPERTS, LANES), F32)],
        compiler_params=pltpu.CompilerParams(dimension_semantics=("arbitrary", "arbitrary")),
        name="moe_rank",
    )(e_flat, consts["upper"], consts["ones"], consts["ltri"])


def _store_row_tiled(ref, x, offset=0):
    rows = x.shape[0]
    for s in range(ROW_TILES):
        ref[pl.ds(offset + s, rows, stride=ROW_TILES), :] = x[:, s * LANES:(s + 1) * LANES]


def _load_row_tiled(ref, rows, offset=0):
    return jnp.concatenate(
        [ref[pl.ds(offset + s, rows, stride=ROW_TILES), :] for s in range(ROW_TILES)], axis=1)


def _row_copy(src, src_row, dst, dst_row, sem):
    return pltpu.make_async_copy(
        src.at[pl.ds(pl.multiple_of(src_row * ROW_TILES, ROW_TILES), ROW_TILES)],
        dst.at[pl.ds(pl.multiple_of(dst_row * ROW_TILES, ROW_TILES), ROW_TILES)], sem)


def _rows_wait(src, dst, dst_row, rows, sem):
    pltpu.make_async_copy(
        src.at[pl.ds(0, rows * ROW_TILES)],
        dst.at[pl.ds(pl.multiple_of(dst_row * ROW_TILES, ROW_TILES), rows * ROW_TILES)], sem).wait()


def _moe_dispatch_kernel(dest_ref, cnt_ref, region_ref, m_ref, xs_ref, zbuf_ref, sem, zsem, *, n_tok):
    t = DISPATCH_TILE
    r = EXPERT_ROWS
    i = pl.program_id(0)

    @pl.when(i == 0)
    def _():
        zbuf_ref[...] = jnp.zeros_like(zbuf_ref)

        def zero_copy(e):
            last = pl.multiple_of((region_ref[e + 1] - r) * ROW_TILES, r * ROW_TILES)
            return pltpu.make_async_copy(zbuf_ref, xs_ref.at[pl.ds(last, r * ROW_TILES)], zsem)

        def start(e, carry):
            @pl.when(cnt_ref[e] > 0)
            def _():
                zero_copy(e).start()
            return carry

        def wait(e, carry):
            @pl.when(cnt_ref[e] > 0)
            def _():
                zero_copy(e).wait()
            return carry

        def tail_copy(blk):
            return pltpu.make_async_copy(
                zbuf_ref, xs_ref.at[pl.ds(pl.multiple_of(blk * (r * ROW_TILES), r * ROW_TILES), r * ROW_TILES)], zsem)

        def tail_start(blk, carry):
            tail_copy(blk).start()
            return carry

        def tail_wait(blk, carry):
            tail_copy(blk).wait()
            return carry

        first_unused = region_ref[N_EXPERTS] // r
        n_blocks = xs_ref.shape[0] // (r * ROW_TILES)
        lax.fori_loop(0, N_EXPERTS, start, 0)
        lax.fori_loop(first_unused, n_blocks, tail_start, 0)
        lax.fori_loop(0, N_EXPERTS, wait, 0)
        lax.fori_loop(first_unused, n_blocks, tail_wait, 0)

    per_iter = DMA_UNROLL // TOP_K

    def issue(g, carry):
        for u in range(per_iter):
            row = g * per_iter + u
            for k in range(TOP_K):
                _row_copy(m_ref, row, xs_ref, dest_ref[k * n_tok + i * t + row], sem).start(priority=k % 2)
        return carry

    lax.fori_loop(0, t // per_iter, issue, 0)
    for k in range(TOP_K):
        pltpu.make_async_copy(m_ref, xs_ref.at[pl.ds(0, t * ROW_TILES)], sem).wait()


def _moe_dispatch(dest, cnt, region, m_rt, rows):
    n = m_rt.shape[0] // ROW_TILES
    t = DISPATCH_TILE
    return pl.pallas_call(
        functools.partial(_moe_dispatch_kernel, n_tok=n),
        grid_spec=pltpu.PrefetchScalarGridSpec(
            num_scalar_prefetch=3,
            grid=(n // t,),
            in_specs=[pl.BlockSpec((t * ROW_TILES, LANES), lambda i, d, c, rg: (i, 0))],
            out_specs=pl.BlockSpec(memory_space=pl.ANY),
            scratch_shapes=[pltpu.VMEM((EXPERT_ROWS * ROW_TILES, LANES), F32),
                            pltpu.SemaphoreType.DMA(()), pltpu.SemaphoreType.DMA(())],
        ),
        out_shape=jax.ShapeDtypeStruct((rows * ROW_TILES, LANES), F32),
        compiler_params=pltpu.CompilerParams(dimension_semantics=("arbitrary",)),
        name="moe_dispatch",
    )(dest, cnt, region, m_rt)


def _moe_experts_kernel(blk_e_ref, nact_ref, x_ref, wg_ref, wu_ref, wd_ref, y_ref, wgu_ref, wdn_ref):
    r = EXPERT_ROWS
    b = pl.program_id(0)
    live = b < nact_ref[0]

    @pl.when(live)
    def _():
        @pl.when(jnp.logical_or(b == 0, blk_e_ref[b] != blk_e_ref[jnp.maximum(b - 1, 0)]))
        def _():
            wgu_ref[:, :D_EXPERT] = wg_ref[0, 0].astype(BF16)
            wgu_ref[:, D_EXPERT:] = wu_ref[0, 0].astype(BF16)
            wdn_ref[...] = wd_ref[0, 0].astype(BF16)

        x = _load_row_tiled(x_ref, r).astype(BF16)
        gu = _dot(x, wgu_ref[...])
        hb = jax.nn.silu(gu[:, :D_EXPERT]) * gu[:, D_EXPERT:]
        _store_row_tiled(y_ref, _dot(hb.astype(BF16), wdn_ref[...]))

    @pl.when(jnp.logical_not(live))
    def _():
        y_ref[...] = jnp.zeros_like(y_ref)


def _moe_experts(blk_e, nact, xs_rt, layer, w_gate, w_up, w_down):
    rows = xs_rt.shape[0] // ROW_TILES
    r = EXPERT_ROWS
    wspec = lambda k, n: pl.BlockSpec((1, 1, k, n), lambda b, be, na: (layer, be[b], 0, 0))
    return pl.pallas_call(
        _moe_experts_kernel,
        grid_spec=pltpu.PrefetchScalarGridSpec(
            num_scalar_prefetch=2,
            grid=(rows // r,),
            in_specs=[pl.BlockSpec((r * ROW_TILES, LANES), lambda b, be, na: (jnp.minimum(b, na[0] - 1), 0)),
                      wspec(D_MODEL, D_EXPERT), wspec(D_MODEL, D_EXPERT), wspec(D_EXPERT, D_MODEL)],
            out_specs=pl.BlockSpec((r * ROW_TILES, LANES), lambda b, be, na: (b, 0)),
            scratch_shapes=[pltpu.VMEM((D_MODEL, 2 * D_EXPERT), BF16),
                            pltpu.VMEM((D_EXPERT, D_MODEL), BF16)],
        ),
        out_shape=jax.ShapeDtypeStruct((rows * ROW_TILES, LANES), F32),
        compiler_params=pltpu.CompilerParams(dimension_semantics=("arbitrary",), vmem_limit_bytes=VMEM_LIMIT),
        name="moe_experts",
    )(blk_e, nact, xs_rt, w_gate, w_up, w_down)


def _combine_ple_kernel(dest_ref, h1_ref, wt_ref, p_ref, y_ref, pleg_ref, gatew_ref, projw_ref, postg_ref,
                        out_ref, ybuf_ref, sem, *, n_tok):
    t = COMBINE_TILE
    i = pl.program_id(0)

    def gather(step, slot):
        def issue(g, carry):
            for u in range(DMA_UNROLL // TOP_K):
                r = g * (DMA_UNROLL // TOP_K) + u
                for k in range(TOP_K):
                    _row_copy(y_ref, dest_ref[k * n_tok + step * t + r], ybuf_ref, (slot * TOP_K + k) * t + r,
                              sem.at[slot]).start(priority=k % 2)
            return carry
        lax.fori_loop(0, t // (DMA_UNROLL // TOP_K), issue, 0)

    @pl.when(i == 0)
    def _():
        gather(0, 0)

    @pl.when(i + 1 < pl.num_programs(0))
    def _():
        gather(i + 1, (i + 1) % 2)

    e = _rms(_dot(p_ref[...].astype(BF16), projw_ref[...]), postg_ref[...])
    slot = i % 2
    _rows_wait(y_ref, ybuf_ref, slot * TOP_K * t, TOP_K * t, sem.at[slot])
    wt = wt_ref[...].T
    y0 = _load_row_tiled(ybuf_ref, t, (slot * TOP_K) * (t * ROW_TILES))
    y1 = _load_row_tiled(ybuf_ref, t, (slot * TOP_K + 1) * (t * ROW_TILES))
    h2 = h1_ref[...] + (wt[:, 0:1] * y0 + wt[:, 1:2] * y1)
    gate = jax.nn.sigmoid(_dot(_rms(h2, pleg_ref[...]).astype(BF16), gatew_ref[...]))
    out_ref[...] = h2 + gate * e


def _combine_ple(dest, h1, wt, p, y_rt, lw):
    n = h1.shape[0]
    t = COMBINE_TILE
    row = lambda w: pl.BlockSpec((t, w), lambda i, d: (i, 0))
    layer_rows = lw["layer"] * (n // t)
    weights = [lw[k] for k in ("ple_g", "gate_w", "proj_w", "post_g")]
    return pl.pallas_call(
        functools.partial(_combine_ple_kernel, n_tok=n),
        grid_spec=pltpu.PrefetchScalarGridSpec(
            num_scalar_prefetch=1,
            grid=(n // t,),
            in_specs=[row(D_MODEL), pl.BlockSpec((SUBLANES, t), lambda i, d: (0, i)), pl.BlockSpec((t, p.shape[1]), lambda i, d: (layer_rows + i, 0)),
                      pl.BlockSpec(memory_space=pl.ANY)]
            + [pl.BlockSpec(w.shape, lambda i, d, nd=w.ndim: (0,) * nd) for w in weights],
            out_specs=row(D_MODEL),
            scratch_shapes=[pltpu.VMEM((2 * TOP_K * t * ROW_TILES, LANES), F32), pltpu.SemaphoreType.DMA((2,))],
        ),
        out_shape=jax.ShapeDtypeStruct((n, D_MODEL), F32),
        compiler_params=pltpu.CompilerParams(dimension_semantics=("arbitrary",), vmem_limit_bytes=VMEM_LIMIT),
        name="combine_ple",
    )(dest, h1, wt, p, y_rt, *weights)


def _segment_matrix(seg_ids):
    seg_ids = jnp.asarray(seg_ids)
    same = (seg_ids[:, None] == seg_ids[None, :]).astype(F32)
    return (same / jnp.sum(same, axis=1, keepdims=True)).astype(BF16)


def _constants():
    lane = jnp.arange(256)
    qk_seg = (lane // HEAD_PAD) * 3 + jnp.where(lane % HEAD_PAD < MLA_NOPE, 0, jnp.where(lane % HEAD_PAD < MLA_QK, 1, 2))
    i = jnp.arange(RANK_CHUNK)
    e = jnp.arange(N_EXPERTS)
    return {
        "g64": _segment_matrix(lane // HEAD_DIM),
        "gqk": _segment_matrix(qk_seg),
        "upper": (i[:, None] < i[None, :]).astype(BF16),
        "ones": jnp.ones((RANK_CHUNK, LANES), BF16),
        "ltri": (e[None, :] < e[:, None]).astype(BF16),
    }


def _layer_weights(i, consts, p):
    row = lambda v: v.reshape(1, -1).astype(F32)
    w_in = p["w_in"][i]
    o_sgu, o_conv, o_q, o_kv = 2 * SGU_WIDTH, 2 * SGU_WIDTH + 2 * CONV_WIDTH, 0, 0
    o_q = o_conv + Q_RANK
    o_kv = o_q + KV_RANK
    w_ckv = jnp.concatenate([w_in[:, o_q:], jnp.zeros((D_MODEL, 256 - KV_RANK - MLA_ROPE), F32)], axis=1)

    causal = jnp.tril(jnp.ones((CHUNK, CHUNK), F32))
    wcat = jnp.transpose(p["sgu_w"][i] * causal, (1, 0, 2)).reshape(CHUNK, SGU_HEADS * CHUNK)
    sbias = jnp.broadcast_to(jnp.transpose(p["sgu_b"][i])[:, :, None],
                             (CHUNK, SGU_HEADS, HEAD_DIM)).reshape(CHUNK, SGU_WIDTH)

    w_uq = p["w_uq"][i].reshape(Q_RANK, MLA_HEADS, MLA_QK)
    w_uq = jnp.pad(w_uq, ((0, 0), (0, 0), (0, HEAD_PAD - MLA_QK))).reshape(Q_RANK, MLA_HEADS * HEAD_PAD)
    qn_g = p["q_norm_g"][i]
    q_gain = jnp.tile(jnp.concatenate([qn_g, jnp.zeros((HEAD_PAD - MLA_QK,), F32)]), MLA_HEADS)
    w_ukv = p["w_ukv"][i].reshape(KV_RANK, MLA_HEADS, MLA_NOPE + MLA_V)
    w_ukt = jnp.transpose(w_ukv[:, :, :MLA_NOPE], (1, 2, 0)).reshape(MLA_HEADS * MLA_NOPE, KV_RANK)
    w_v = w_ukv[:, :, MLA_NOPE:].reshape(KV_RANK, MLA_HEADS // 2, 2, MLA_V)
    zeros_v = jnp.zeros((KV_RANK, MLA_HEADS // 2, MLA_V), F32)
    w_uv = jnp.stack([w_v[:, :, 0], zeros_v, zeros_v, w_v[:, :, 1]], axis=2).reshape(KV_RANK, MLA_HEADS * HEAD_PAD)
    v_ones = jnp.tile(jnp.repeat(jnp.array([0.0, 1.0, 1.0, 0.0], F32), MLA_V), MLA_HEADS // 2)
    kn_g = p["k_norm_g"][i]

    bg = p["branch_norm_g"][i]
    w_o = p["w_o"][i]
    wr = jnp.concatenate([p["router_group_w"][i], jnp.zeros((D_MODEL, 8 - N_GROUPS), F32),
                          p["router_expert_w"][i], jnp.zeros((D_MODEL, LANES - 8 - N_EXPERTS), F32)], axis=1)
    wr_hi = wr.astype(BF16)
    br = jnp.concatenate([p["router_group_b"][i], jnp.full((8 - N_GROUPS,), NEG_BIG, F32),
                          p["router_expert_b"][i], jnp.zeros((LANES - 8 - N_EXPERTS,), F32)])
    return {
        "mix_g": row(p["mix_norm_g"][i]),
        "w_sgu": w_in[:, :o_sgu].astype(BF16),
        "w_conv": w_in[:, o_sgu:o_conv].astype(BF16),
        "w_cq": w_in[:, o_conv:o_q].astype(BF16),
        "w_ckv": w_ckv.astype(BF16),
        "sgu_ln_g": row(p["sgu_ln_g"][i]), "sgu_ln_b": row(p["sgu_ln_b"][i]),
        "sgu_wcat": wcat.astype(BF16), "sgu_bias": sbias, "g64": consts["g64"], "bg_a": row(bg[:SGU_WIDTH]),
        "conv_w": jnp.pad(p["conv_w"][i], ((0, 1), (0, 0))), "conv_b": row(p["conv_b"][i]),
        "conv_ln_g": row(p["conv_ln_g"][i]), "conv_ln_b": row(p["conv_ln_b"][i]),
        "pw_w": p["conv_pw_w"][i].astype(BF16), "pw_b": row(p["conv_pw_b"][i]),
        "bg_b": row(bg[SGU_WIDTH:SGU_WIDTH + CONV_WIDTH]),
        "qa_g": row(p["q_a_norm_g"][i]), "w_uq": w_uq.astype(BF16), "gqk": consts["gqk"],
        "q_gain": row(q_gain),
        "kva_g": row(p["kv_a_norm_g"][i]), "w_ukt": w_ukt.astype(BF16), "w_uv": w_uv.astype(BF16),
        "v_ones": row(v_ones),
        "kn_g": jnp.broadcast_to(kn_g[:MLA_NOPE, None], (MLA_NOPE, MIX_TILE)),
        "kpe_g": jnp.broadcast_to(kn_g[MLA_NOPE:, None], (MLA_ROPE, MIX_TILE)),
        "bg_c": row(bg[SGU_WIDTH + CONV_WIDTH:]),
        "w_o_a": w_o[:SGU_WIDTH].astype(BF16),
        "w_o_b": w_o[SGU_WIDTH:SGU_WIDTH + CONV_WIDTH].astype(BF16),
        "w_o_c": w_o[SGU_WIDTH + CONV_WIDTH:].astype(BF16),
        "ffn_g": row(p["ffn_norm_g"][i]),
        "wr_split": jnp.concatenate([wr_hi, (wr - wr_hi.astype(F32)).astype(BF16)], axis=1), "br": row(br),
        "layer": i, "w_gate": p["moe_w_gate"], "w_up": p["moe_w_up"], "w_down": p["moe_w_down"],
        "ple_g": row(p["ple_norm_g"][i]), "gate_w": p["ple_gate_w"][i].astype(BF16),
        "proj_w": p["ple_proj_w"][i].astype(BF16), "post_g": row(p["ple_post_norm_g"][i]),
    }


def _moe(h1, m, eidx, wts, pl_i, lw, consts):
    n = h1.shape[0]
    total = TOP_K * n
    rows = total + N_EXPERTS * EXPERT_ROWS
    dest2d, counts = _moe_rank(eidx[:TOP_K].reshape(1, total), consts)
    dest = dest2d.reshape(total)
    cnt = counts[:, 0].astype(jnp.int32)
    padded = (cnt + EXPERT_ROWS - 1) // EXPERT_ROWS * EXPERT_ROWS
    pend = jnp.cumsum(padded)
    nblk = rows // EXPERT_ROWS
    blk_start = jnp.arange(nblk, dtype=jnp.int32) * EXPERT_ROWS
    blk_e = jnp.minimum(jnp.sum((pend[None, :] <= blk_start[:, None]).astype(jnp.int32), axis=1), N_EXPERTS - 1)
    nact = (pend[-1:] // EXPERT_ROWS).astype(jnp.int32)
    region = jnp.concatenate([jnp.zeros((1,), jnp.int32), pend.astype(jnp.int32)])
    xs = _moe_dispatch(dest, cnt, region, m, rows)
    y = _moe_experts(blk_e, nact, xs, lw["layer"], lw["w_gate"], lw["w_up"], lw["w_down"])
    return _combine_ple(dest, h1, wts, pl_i, y, lw)


def kernel(x, p, positions, mix_norm_g, w_in, sgu_ln_g, sgu_ln_b, sgu_w, sgu_b, conv_w, conv_b, conv_ln_g, conv_ln_b, conv_pw_w, conv_pw_b, q_a_norm_g, w_uq, kv_a_norm_g, w_ukv, q_norm_g, k_norm_g, branch_norm_g, w_o, ffn_norm_g, router_group_w, router_group_b, router_expert_w, router_expert_b, moe_w_gate, moe_w_up, moe_w_down, ple_norm_g, ple_gate_w, ple_proj_w, ple_post_norm_g):
    params = dict(
        mix_norm_g=mix_norm_g, w_in=w_in, sgu_ln_g=sgu_ln_g, sgu_ln_b=sgu_ln_b, sgu_w=sgu_w, sgu_b=sgu_b,
        conv_w=conv_w, conv_b=conv_b, conv_ln_g=conv_ln_g, conv_ln_b=conv_ln_b, conv_pw_w=conv_pw_w,
        conv_pw_b=conv_pw_b, q_a_norm_g=q_a_norm_g, w_uq=w_uq, kv_a_norm_g=kv_a_norm_g, w_ukv=w_ukv,
        q_norm_g=q_norm_g, k_norm_g=k_norm_g, branch_norm_g=branch_norm_g, w_o=w_o, ffn_norm_g=ffn_norm_g,
        router_group_w=router_group_w, router_group_b=router_group_b, router_expert_w=router_expert_w,
        router_expert_b=router_expert_b, moe_w_gate=moe_w_gate, moe_w_up=moe_w_up, moe_w_down=moe_w_down,
        ple_norm_g=ple_norm_g, ple_gate_w=ple_gate_w, ple_proj_w=ple_proj_w, ple_post_norm_g=ple_post_norm_g)
    batch, seq, d = x.shape
    n = batch * seq
    depth = w_in.shape[0]
    consts = _constants()
    cos, sin = _rope_tables(positions)
    h = x.reshape(n, d)
    p_rows = p.reshape(depth * n, p.shape[-1])
    for i in range(depth):
        lw = _layer_weights(i, consts, params)
        ya, yb, q, kt, v = _mixer_pre(h, cos, sin, lw, batch, seq)
        yc = _attention(q, kt, v, batch, seq)
        h1, m, eidx, wts = _outproj_router(h, ya, yb, yc, lw)
        h = _moe(h1, m, eidx, wts, p_rows, lw, consts)
    return h.reshape(batch, seq, d)
```

```python
import functools
import math

import jax
import jax.numpy as jnp
from jax import lax
from jax.experimental import pallas as pl
from jax.experimental.pallas import tpu as pltpu

F32 = jnp.float32
BF16 = jnp.bfloat16

D_MODEL = 1024
HEAD_DIM = 64
SGU_HEADS = 4
SGU_WIDTH = 256
CHUNK = 128
CONV_WIDTH = 256
CONV_KERNEL = 31
MLA_HEADS = 8
MLA_NOPE = 64
MLA_ROPE = 32
MLA_QK = MLA_NOPE + MLA_ROPE
MLA_V = 64
Q_RANK = 256
KV_RANK = 128
ROPE_THETA = 10000.0
N_GROUPS = 4
EXPERTS_PER_GROUP = 8
N_EXPERTS = 32
D_EXPERT = 256
TOP_K = 2
EPS = 1e-6

LANES = 128
SUBLANES = 8
HEAD_PAD = 128
MIX_TILE = 1024
ROUTER_TILE = 1024
CONV_HALO = 32
CONV_SHIFTED_ROWS = MIX_TILE + CONV_HALO - SUBLANES
MIXER_ORDER = ("q", "sgu", "kv", "conv")
ATT_TQ = 256
ATT_PAIRS = 2
RANK_CHUNK = 512
RANK_STEP = 2048
EXPERT_ROWS = 512
DISPATCH_TILE = 1024
COMBINE_TILE = 512
ROW_TILES = D_MODEL // LANES
DMA_UNROLL = 8
VMEM_LIMIT = 48 * 1024 * 1024
NEG_BIG = -1e30


def _dot(a, b):
    return jnp.dot(a, b, preferred_element_type=F32)


def _rms(x, g):
    ms = jnp.mean(x * x, axis=-1, keepdims=True)
    return x * lax.rsqrt(ms + EPS) * g


def _seg_mean(x, g_ref, split=True):
    g = g_ref[...]
    outs = []
    for c in range(x.shape[1] // 256):
        xb = x[:, c * 256:(c + 1) * 256]
        hi = xb.astype(BF16)
        acc = _dot(hi, g)
        if split:
            acc = acc + _dot((xb - hi.astype(F32)).astype(BF16), g)
        outs.append(acc)
    return outs[0] if len(outs) == 1 else jnp.concatenate(outs, axis=-1)


def _group_layernorm(x, g_ref, gain, bias):
    d = x - _seg_mean(x, g_ref)
    var = _seg_mean(d * d, g_ref)
    return d * lax.rsqrt(var + EPS) * gain + bias


def _rope_table_kernel(pos_ref, invf_ref, cos_ref, sin_ref):
    ang = pos_ref[...].astype(F32) * invf_ref[...]
    cos_ref[...] = jnp.cos(ang)
    sin_ref[...] = jnp.sin(ang)


def _rope_tables(positions):
    b, s = positions.shape
    n = b * s
    half = MLA_ROPE // 2
    inv_freq = ROPE_THETA ** (-jnp.arange(0, MLA_ROPE, 2, dtype=F32) / MLA_ROPE)
    pos_rep = jnp.broadcast_to(positions.reshape(n, 1), (n, half)).reshape(n * half // LANES, LANES)
    invf_rep = jnp.tile(inv_freq, LANES // half).reshape(1, LANES)
    shape = jax.ShapeDtypeStruct(pos_rep.shape, F32)
    cos, sin = pl.pallas_call(_rope_table_kernel, out_shape=(shape, shape), name="rope_tables")(pos_rep, invf_rep)
    return cos.reshape(n, half), sin.reshape(n, half)


def _mixer_pre_kernel(
        h_ref, cos_ref, sin_ref,
        mixg_ref, wsgu_ref, wconv_ref, wcq_ref, wckv_ref,
        slng_ref, slnb_ref, wcat_ref, sbias_ref, g64_ref, bga_ref,
        cw_ref, cb_ref, clng_ref, clnb_ref, pww_ref, pwb_ref, bgb_ref,
        qag_ref, wuq_ref, gqk_ref, qgain_ref,
        kvag_ref, wukt_ref, wuv_ref, vones_ref, kng_ref, kpeg_ref,
        ya_ref, yb_ref, q_ref, kt_ref, v_ref,
        ybuf_ref, ysh_ref, *, tiles_per_seq):
    t = MIX_TILE
    a = _rms(h_ref[...], mixg_ref[...]).astype(BF16)
    zs = _dot(a, wsgu_ref[...])
    zc = _dot(a, wconv_ref[...])
    zq = _dot(a, wcq_ref[...])
    zkv = _dot(a, wckv_ref[...])

    half = MLA_ROPE // 2
    cos, sin = cos_ref[...], sin_ref[...]
    zn = jnp.zeros((t, MLA_NOPE), F32)
    zh = jnp.zeros((t, half), F32)
    zp = jnp.zeros((t, HEAD_PAD - MLA_QK), F32)
    rc = jnp.concatenate([jnp.ones((t, MLA_NOPE), F32), cos, cos, zp], axis=1)
    rs1 = jnp.concatenate([zn, -sin, zh, zp], axis=1)
    rs2 = jnp.concatenate([zn, zh, sin, zp], axis=1)

    def sgu_branch():
        zg = jax.nn.gelu(zs)
        u = zg[:, :SGU_WIDTH]
        vn = _group_layernorm(zg[:, SGU_WIDTH:], g64_ref, slng_ref[...], slnb_ref[...])
        lane_head = lax.broadcasted_iota(jnp.int32, (CHUNK, SGU_WIDTH), 1) // HEAD_DIM
        wcat = wcat_ref[...]
        sbias = sbias_ref[...]
        parts = []
        for c in range(t // CHUNK):
            vc = vn[c * CHUNK:(c + 1) * CHUNK]
            stacked = jnp.concatenate(
                [jnp.where(lane_head == hh, vc, 0.0) for hh in range(SGU_HEADS)], axis=0).astype(BF16)
            s = _dot(wcat, stacked) + sbias
            parts.append(u[c * CHUNK:(c + 1) * CHUNK] * s)
        ya_ref[...] = _rms(jnp.concatenate(parts, axis=0), bga_ref[...]).astype(BF16)

    def conv_branch():
        yg = zc[:, :CONV_WIDTH] * jax.nn.sigmoid(zc[:, CONV_WIDTH:])
        first = (pl.program_id(0) % tiles_per_seq) == 0

        @pl.when(first)
        def _():
            ybuf_ref[0:CONV_HALO, :] = jnp.zeros((CONV_HALO, CONV_WIDTH), F32)

        @pl.when(jnp.logical_not(first))
        def _():
            ybuf_ref[0:CONV_HALO, :] = ybuf_ref[t:t + CONV_HALO, :]

        ybuf_ref[CONV_HALO:CONV_HALO + t, :] = yg
        for sh in range(1, SUBLANES):
            ysh_ref[sh - 1] = ybuf_ref[pl.ds(sh, CONV_SHIFTED_ROWS), :]
        rows = 64
        first_tap_row = CONV_HALO - (CONV_KERNEL - 1)
        conv_parts = []
        for r in range(t // rows):
            acc = jnp.broadcast_to(cb_ref[...], (rows, CONV_WIDTH))
            for tap in range(CONV_KERNEL):
                sh = (first_tap_row + tap) % SUBLANES
                start = r * rows + first_tap_row + tap - sh
                src = ybuf_ref[pl.ds(start, rows), :] if sh == 0 else ysh_ref[sh - 1, pl.ds(start, rows), :]
                acc = acc + cw_ref[tap:tap + 1, :] * src
            conv_parts.append(acc)
        cv = jnp.concatenate(conv_parts, axis=0)
        cn = _group_layernorm(cv, g64_ref, clng_ref[...], clnb_ref[...])
        yb = _dot(jax.nn.silu(cn).astype(BF16), pww_ref[...]) + pwb_ref[...]
        yb_ref[...] = _rms(yb, bgb_ref[...]).astype(BF16)

    def query_branch():
        cqn = _rms(zq, qag_ref[...]).astype(BF16)
        qf = _dot(cqn, wuq_ref[...])
        qn = qf * lax.rsqrt(_seg_mean(qf * qf, gqk_ref, split=False) + EPS) * qgain_ref[...]
        for hh in range(MLA_HEADS):
            blk = qn[:, hh * HEAD_PAD:(hh + 1) * HEAD_PAD]
            rot = blk * rc + pltpu.roll(blk, HEAD_PAD - half, 1) * rs1 + pltpu.roll(blk, half, 1) * rs2
            q_ref[:, hh * HEAD_PAD:(hh + 1) * HEAD_PAD] = rot.astype(BF16)

    def key_value_branch():
        ckvn = _rms(zkv[:, :KV_RANK], kvag_ref[...]).astype(BF16)
        v_ref[...] = (_dot(ckvn, wuv_ref[...]) + vones_ref[...]).astype(BF16)
        knt = lax.dot_general(wukt_ref[...], ckvn, (((1,), (1,)), ((), ())), preferred_element_type=F32)
        x = zkv[:, KV_RANK:].T[0:MLA_ROPE]
        xn = x * lax.rsqrt(jnp.mean(x * x, axis=0, keepdims=True) + EPS) * kpeg_ref[...]
        x1, x2 = xn[:MLA_ROPE // 2], xn[MLA_ROPE // 2:]
        cos_t = rc.T[MLA_NOPE:MLA_NOPE + half]
        sin_t = rs2.T[MLA_NOPE + half:MLA_QK]
        kpe = jnp.concatenate([x1 * cos_t - x2 * sin_t, x2 * cos_t + x1 * sin_t], axis=0)
        pad = jnp.zeros((HEAD_PAD - MLA_QK, t), F32)
        kng = kng_ref[...]
        for hh in range(MLA_HEADS):
            blk = knt[hh * MLA_NOPE:(hh + 1) * MLA_NOPE]
            kn = blk * lax.rsqrt(jnp.mean(blk * blk, axis=0, keepdims=True) + EPS) * kng
            kt_ref[0, hh * HEAD_PAD:(hh + 1) * HEAD_PAD, :] = jnp.concatenate([kn, kpe, pad], axis=0).astype(BF16)

    branches = {"sgu": sgu_branch, "conv": conv_branch, "q": query_branch, "kv": key_value_branch}
    for name in MIXER_ORDER:
        branches[name]()


def _full(shape):
    nd = len(shape)
    return pl.BlockSpec(shape, lambda *_: (0,) * nd)


def _mixer_pre(h, cos, sin, lw, batch, seq):
    n = h.shape[0]
    t = MIX_TILE
    tps = seq // t
    row = lambda w: pl.BlockSpec((t, w), lambda i: (i, 0))
    weights = [lw[k] for k in (
        "mix_g", "w_sgu", "w_conv", "w_cq", "w_ckv",
        "sgu_ln_g", "sgu_ln_b", "sgu_wcat", "sgu_bias", "g64", "bg_a",
        "conv_w", "conv_b", "conv_ln_g", "conv_ln_b", "pw_w", "pw_b", "bg_b",
        "qa_g", "w_uq", "gqk", "q_gain",
        "kva_g", "w_ukt", "w_uv", "v_ones", "kn_g", "kpe_g")]
    in_specs = [row(D_MODEL), row(MLA_ROPE // 2), row(MLA_ROPE // 2)] + [_full(w.shape) for w in weights]
    out_shape = (
        jax.ShapeDtypeStruct((n, SGU_WIDTH), BF16),
        jax.ShapeDtypeStruct((n, CONV_WIDTH), BF16),
        jax.ShapeDtypeStruct((n, MLA_HEADS * HEAD_PAD), BF16),
        jax.ShapeDtypeStruct((batch, MLA_HEADS * HEAD_PAD, seq), BF16),
        jax.ShapeDtypeStruct((n, MLA_HEADS * HEAD_PAD), BF16),
    )
    out_specs = (
        row(SGU_WIDTH), row(CONV_WIDTH), row(MLA_HEADS * HEAD_PAD),
        pl.BlockSpec((1, MLA_HEADS * HEAD_PAD, t), lambda i: (i // tps, 0, i % tps)),
        row(MLA_HEADS * HEAD_PAD),
    )
    return pl.pallas_call(
        functools.partial(_mixer_pre_kernel, tiles_per_seq=tps),
        grid=(n // t,),
        in_specs=in_specs,
        out_specs=out_specs,
        out_shape=out_shape,
        scratch_shapes=[pltpu.VMEM((t + CONV_HALO, CONV_WIDTH), F32),
                        pltpu.VMEM((SUBLANES - 1, CONV_SHIFTED_ROWS, CONV_WIDTH), F32)],
        compiler_params=pltpu.CompilerParams(dimension_semantics=("arbitrary",), vmem_limit_bytes=VMEM_LIMIT),
        name="mixer_pre",
    )(h, cos, sin, *weights)


def _attention_kernel(q_ref, kt_ref, v_ref, o_ref, *, seq):
    tq = ATT_TQ
    exp2_scale = MLA_QK ** -0.5 * math.log2(math.e)
    row = lax.broadcasted_iota(jnp.int32, (tq, tq), 0)
    col = lax.broadcasted_iota(jnp.int32, (tq, tq), 1)
    lane = lax.broadcasted_iota(jnp.int32, (tq, HEAD_PAD), 1)
    def scores(pair, qi, hh):
        nk = (qi + 1) * tq
        head = (2 * pair + hh) * HEAD_PAD
        s = _dot(q_ref[qi * tq:(qi + 1) * tq, head:head + HEAD_PAD], kt_ref[0, head:head + HEAD_PAD, 0:nk])
        diag = jnp.where(col <= row, s[:, nk - tq:], NEG_BIG)
        return diag if qi == 0 else jnp.concatenate([s[:, :nk - tq], diag], axis=1)

    n_blocks = seq // tq
    items = [(pair, qi) for pair in range(ATT_PAIRS) for qi in range(n_blocks)]
    ahead = [scores(*items[0], hh) for hh in range(2)]
    for idx, (pair, qi) in enumerate(items):
        s_pair, ahead = ahead, []
        if idx + 1 < len(items):
            ahead = [scores(*items[idx + 1], hh) for hh in range(2)]
        p_pair = [jnp.exp2(((s - jnp.max(s, axis=-1, keepdims=True)) * exp2_scale).astype(BF16)) for s in s_pair]
        vcols = slice(2 * pair * HEAD_PAD, 2 * (pair + 1) * HEAD_PAD)
        acc = _dot(jnp.concatenate(p_pair, axis=0), v_ref[0:(qi + 1) * tq, vcols])
        a0, a1 = acc[:tq, :HEAD_PAD], acc[tq:, HEAD_PAD:]
        o_ref[qi * tq:(qi + 1) * tq, 2 * pair * MLA_V:2 * (pair + 1) * MLA_V] = jnp.where(
            lane < MLA_V, a0 / pltpu.roll(a0, MLA_V, 1), a1 / pltpu.roll(a1, MLA_V, 1))


def _attention(q, kt, v, batch, seq):
    n = q.shape[0]
    width = 2 * ATT_PAIRS
    return pl.pallas_call(
        functools.partial(_attention_kernel, seq=seq),
        grid=(batch, MLA_HEADS // width),
        in_specs=[
            pl.BlockSpec((seq, width * HEAD_PAD), lambda b, p: (b, p)),
            pl.BlockSpec((1, width * HEAD_PAD, seq), lambda b, p: (b, p, 0)),
            pl.BlockSpec((seq, width * HEAD_PAD), lambda b, p: (b, p)),
        ],
        out_specs=pl.BlockSpec((seq, width * MLA_V), lambda b, p: (b, p)),
        out_shape=jax.ShapeDtypeStruct((n, MLA_HEADS * MLA_V), F32),
        compiler_params=pltpu.CompilerParams(
            dimension_semantics=("arbitrary", "arbitrary"), vmem_limit_bytes=VMEM_LIMIT),
        name="attention",
    )(q, kt, v)


def _outproj_router_kernel(h_ref, ya_ref, yb_ref, yc_ref, bgc_ref, woa_ref, wob_ref, woc_ref,
                           ffng_ref, wr_ref, br_ref,
                           h1_ref, m_ref, eidx_ref, wts_ref):
    t = ROUTER_TILE
    ycn = _rms(yc_ref[...], bgc_ref[...])
    proj = (_dot(ya_ref[...], woa_ref[...]) + _dot(yb_ref[...], wob_ref[...])
            + _dot(ycn.astype(BF16), woc_ref[...]))
    h1 = h_ref[...] + proj
    h1_ref[...] = h1
    m = _rms(h1, ffng_ref[...])
    _store_row_tiled(m_ref, m)

    mh = m.astype(BF16)
    ml = (m - mh.astype(F32)).astype(BF16)
    wr = wr_ref[...]
    both = _dot(mh, wr)
    logits = both[:, :LANES] + both[:, LANES:] + _dot(ml, wr[:, :LANES]) + br_ref[...]
    lt = logits.T
    rowi = lax.broadcasted_iota(jnp.int32, (EXPERTS_PER_GROUP, t), 0)
    g8 = lt[0:8]
    gmax = jnp.max(g8, axis=0, keepdims=True)
    gsum = jnp.sum(jnp.exp(g8 - gmax), axis=0, keepdims=True)
    gidx = jnp.min(jnp.where(g8 == gmax, rowi, 8), axis=0, keepdims=True)
    g_w = 1.0 / gsum
    esel = jnp.zeros((EXPERTS_PER_GROUP, t), F32)
    for g in range(N_GROUPS):
        esel = jnp.where(gidx == g, lt[8 + g * EXPERTS_PER_GROUP:8 + (g + 1) * EXPERTS_PER_GROUP], esel)
    ep = jnp.exp(esel - jnp.max(esel, axis=0, keepdims=True))
    eprob = ep / jnp.sum(ep, axis=0, keepdims=True)
    v1 = jnp.max(eprob, axis=0, keepdims=True)
    i1 = jnp.min(jnp.where(eprob == v1, rowi, 8), axis=0, keepdims=True)
    rest = jnp.where(rowi == i1, -1.0, eprob)
    v2 = jnp.max(rest, axis=0, keepdims=True)
    i2 = jnp.min(jnp.where(rest == v2, rowi, 8), axis=0, keepdims=True)
    den = v1 + v2
    e1 = gidx * EXPERTS_PER_GROUP + i1
    e2 = gidx * EXPERTS_PER_GROUP + i2
    eidx_ref[...] = jnp.where(rowi == 0, e1, jnp.where(rowi == 1, e2, 0))
    wts_ref[...] = jnp.where(rowi == 0, g_w * (v1 / den), jnp.where(rowi == 1, g_w * (v2 / den), 0.0))


def _outproj_router(h, ya, yb, yc, lw):
    n = h.shape[0]
    t = ROUTER_TILE
    row = lambda w: pl.BlockSpec((t, w), lambda i: (i, 0))
    weights = [lw[k] for k in ("bg_c", "w_o_a", "w_o_b", "w_o_c", "ffn_g", "wr_split", "br")]
    colspec = pl.BlockSpec((8, t), lambda i: (0, i))
    return pl.pallas_call(
        _outproj_router_kernel,
        grid=(n // t,),
        in_specs=[row(D_MODEL), row(SGU_WIDTH), row(CONV_WIDTH), row(MLA_HEADS * MLA_V)]
        + [_full(w.shape) for w in weights],
        out_specs=(row(D_MODEL), pl.BlockSpec((t * ROW_TILES, LANES), lambda i: (i, 0)), colspec, colspec),
        out_shape=(
            jax.ShapeDtypeStruct((n, D_MODEL), F32),
            jax.ShapeDtypeStruct((n * ROW_TILES, LANES), F32),
            jax.ShapeDtypeStruct((8, n), jnp.int32),
            jax.ShapeDtypeStruct((8, n), F32),
        ),
        compiler_params=pltpu.CompilerParams(dimension_semantics=("arbitrary",), vmem_limit_bytes=VMEM_LIMIT),
        name="outproj_router",
    )(h, ya, yb, yc, *weights)


def _moe_rank_kernel(e_ref, upper_ref, ones_ref, ltri_ref, dest_ref, cnt_ref, run_ref, base_ref):
    phase = pl.program_id(0)
    step = pl.program_id(1)
    c = RANK_CHUNK
    expert = lax.broadcasted_iota(jnp.int32, (N_EXPERTS, c), 0)

    @pl.when(jnp.logical_and(phase == 0, step == 0))
    def _():
        run_ref[...] = jnp.zeros_like(run_ref)
        base_ref[...] = jnp.zeros_like(base_ref)

    @pl.when(jnp.logical_and(phase == 1, step == 0))
    def _():
        blocks = jnp.floor((run_ref[...] + (EXPERT_ROWS - 1)) * (1.0 / EXPERT_ROWS))
        base_ref[...] = _dot(ltri_ref[...], blocks.astype(BF16)) * EXPERT_ROWS
        run_ref[...] = jnp.zeros_like(run_ref)

    def chunk_onehot(sub):
        return jnp.where(expert == e_ref[:, sub * c:(sub + 1) * c], 1.0, 0.0)

    @pl.when(phase == 0)
    def _():
        dest_ref[...] = jnp.zeros_like(dest_ref)
        for sub in range(RANK_STEP // c):
            run_ref[...] = run_ref[...] + _dot(chunk_onehot(sub).astype(BF16), ones_ref[...])

    @pl.when(phase == 1)
    def _():
        for sub in range(RANK_STEP // c):
            onehot = chunk_onehot(sub)
            oh16 = onehot.astype(BF16)
            before = _dot(oh16, upper_ref[...])
            pos = before + run_ref[:, 0:1] + base_ref[:, 0:1]
            dest_ref[:, sub * c:(sub + 1) * c] = jnp.sum(onehot * pos, axis=0, keepdims=True).astype(jnp.int32)
            run_ref[...] = run_ref[...] + _dot(oh16, ones_ref[...])

    cnt_ref[...] = run_ref[...]


def _moe_rank(e_flat, consts):
    total = e_flat.shape[1]
    c = RANK_CHUNK
    st = RANK_STEP
    return pl.pallas_call(
        _moe_rank_kernel,
        grid=(2, total // st),
        in_specs=[pl.BlockSpec((1, st), lambda p, s: (0, s)),
                  _full((c, c)), _full((c, LANES)), _full((N_EXPERTS, N_EXPERTS))],
        out_specs=(pl.BlockSpec((1, st), lambda p, s: (0, s * p)),
                   pl.BlockSpec((N_EXPERTS, LANES), lambda p, s: (0, 0))),
        out_shape=(jax.ShapeDtypeStruct((1, total), jnp.int32),
                   jax.ShapeDtypeStruct((N_EXPERTS, LANES), F32)),
        scratch_shapes=[pltpu.VMEM((N_EXPERTS, LANES), F32), pltpu.VMEM((N_EXPERTS, LANES), F32)],
        compiler_params=pltpu.CompilerParams(dimension_semantics=("arbitrary", "arbitrary")),
        name="moe_rank",
    )(e_flat, consts["upper"], consts["ones"], consts["ltri"])


def _store_row_tiled(ref, x, offset=0):
    rows = x.shape[0]
    for s in range(ROW_TILES):
        ref[pl.ds(offset + s, rows, stride=ROW_TILES), :] = x[:, s * LANES:(s + 1) * LANES]


def _load_row_tiled(ref, rows, offset=0):
    return jnp.concatenate(
        [ref[pl.ds(offset + s, rows, stride=ROW_TILES), :] for s in range(ROW_TILES)], axis=1)


def _row_copy(src, src_row, dst, dst_row, sem):
    return pltpu.make_async_copy(
        src.at[pl.ds(pl.multiple_of(src_row * ROW_TILES, ROW_TILES), ROW_TILES)],
        dst.at[pl.ds(pl.multiple_of(dst_row * ROW_TILES, ROW_TILES), ROW_TILES)], sem)


def _rows_wait(src, dst, dst_row, rows, sem):
    pltpu.make_async_copy(
        src.at[pl.ds(0, rows * ROW_TILES)],
        dst.at[pl.ds(pl.multiple_of(dst_row * ROW_TILES, ROW_TILES), rows * ROW_TILES)], sem).wait()


def _moe_dispatch_kernel(dest_ref, cnt_ref, region_ref, m_ref, xs_ref, zbuf_ref, sem, zsem, *, n_tok):
    t = DISPATCH_TILE
    r = EXPERT_ROWS
    i = pl.program_id(0)

    @pl.when(i == 0)
    def _():
        zbuf_ref[...] = jnp.zeros_like(zbuf_ref)

        def zero_copy(e):
            last = pl.multiple_of((region_ref[e + 1] - r) * ROW_TILES, r * ROW_TILES)
            return pltpu.make_async_copy(zbuf_ref, xs_ref.at[pl.ds(last, r * ROW_TILES)], zsem)

        def start(e, carry):
            @pl.when(cnt_ref[e] > 0)
            def _():
                zero_copy(e).start()
            return carry

        def wait(e, carry):
            @pl.when(cnt_ref[e] > 0)
            def _():
                zero_copy(e).wait()
            return carry

        def tail_copy(blk):
            return pltpu.make_async_copy(
                zbuf_ref, xs_ref.at[pl.ds(pl.multiple_of(blk * (r * ROW_TILES), r * ROW_TILES), r * ROW_TILES)], zsem)

        def tail_start(blk, carry):
            tail_copy(blk).start()
            return carry

        def tail_wait(blk, carry):
            tail_copy(blk).wait()
            return carry

        first_unused = region_ref[N_EXPERTS] // r
        n_blocks = xs_ref.shape[0] // (r * ROW_TILES)
        lax.fori_loop(0, N_EXPERTS, start, 0)
        lax.fori_loop(first_unused, n_blocks, tail_start, 0)
        lax.fori_loop(0, N_EXPERTS, wait, 0)
        lax.fori_loop(first_unused, n_blocks, tail_wait, 0)

    per_iter = DMA_UNROLL // TOP_K

    def issue(g, carry):
        for u in range(per_iter):
            row = g * per_iter + u
            for k in range(TOP_K):
                _row_copy(m_ref, row, xs_ref, dest_ref[k * n_tok + i * t + row], sem).start(priority=k % 2)
        return carry

    lax.fori_loop(0, t // per_iter, issue, 0)
    for k in range(TOP_K):
        pltpu.make_async_copy(m_ref, xs_ref.at[pl.ds(0, t * ROW_TILES)], sem).wait()


def _moe_dispatch(dest, cnt, region, m_rt, rows):
    n = m_rt.shape[0] // ROW_TILES
    t = DISPATCH_TILE
    return pl.pallas_call(
        functools.partial(_moe_dispatch_kernel, n_tok=n),
        grid_spec=pltpu.PrefetchScalarGridSpec(
            num_scalar_prefetch=3,
            grid=(n // t,),
            in_specs=[pl.BlockSpec((t * ROW_TILES, LANES), lambda i, d, c, rg: (i, 0))],
            out_specs=pl.BlockSpec(memory_space=pl.ANY),
            scratch_shapes=[pltpu.VMEM((EXPERT_ROWS * ROW_TILES, LANES), F32),
                            pltpu.SemaphoreType.DMA(()), pltpu.SemaphoreType.DMA(())],
        ),
        out_shape=jax.ShapeDtypeStruct((rows * ROW_TILES, LANES), F32),
        compiler_params=pltpu.CompilerParams(dimension_semantics=("arbitrary",)),
        name="moe_dispatch",
    )(dest, cnt, region, m_rt)


def _moe_experts_kernel(blk_e_ref, nact_ref, x_ref, wg_ref, wu_ref, wd_ref, y_ref, wgu_ref, wdn_ref):
    r = EXPERT_ROWS
    b = pl.program_id(0)
    live = b < nact_ref[0]

    @pl.when(live)
    def _():
        @pl.when(jnp.logical_or(b == 0, blk_e_ref[b] != blk_e_ref[jnp.maximum(b - 1, 0)]))
        def _():
            wgu_ref[:, :D_EXPERT] = wg_ref[0, 0].astype(BF16)
            wgu_ref[:, D_EXPERT:] = wu_ref[0, 0].astype(BF16)
            wdn_ref[...] = wd_ref[0, 0].astype(BF16)

        x = _load_row_tiled(x_ref, r).astype(BF16)
        gu = _dot(x, wgu_ref[...])
        hb = jax.nn.silu(gu[:, :D_EXPERT]) * gu[:, D_EXPERT:]
        _store_row_tiled(y_ref, _dot(hb.astype(BF16), wdn_ref[...]))

    @pl.when(jnp.logical_not(live))
    def _():
        y_ref[...] = jnp.zeros_like(y_ref)


def _moe_experts(blk_e, nact, xs_rt, layer, w_gate, w_up, w_down):
    rows = xs_rt.shape[0] // ROW_TILES
    r = EXPERT_ROWS
    wspec = lambda k, n: pl.BlockSpec((1, 1, k, n), lambda b, be, na: (layer, be[b], 0, 0))
    return pl.pallas_call(
        _moe_experts_kernel,
        grid_spec=pltpu.PrefetchScalarGridSpec(
            num_scalar_prefetch=2,
            grid=(rows // r,),
            in_specs=[pl.BlockSpec((r * ROW_TILES, LANES), lambda b, be, na: (jnp.minimum(b, na[0] - 1), 0)),
                      wspec(D_MODEL, D_EXPERT), wspec(D_MODEL, D_EXPERT), wspec(D_EXPERT, D_MODEL)],
            out_specs=pl.BlockSpec((r * ROW_TILES, LANES), lambda b, be, na: (b, 0)),
            scratch_shapes=[pltpu.VMEM((D_MODEL, 2 * D_EXPERT), BF16),
                            pltpu.VMEM((D_EXPERT, D_MODEL), BF16)],
        ),
        out_shape=jax.ShapeDtypeStruct((rows * ROW_TILES, LANES), F32),
        compiler_params=pltpu.CompilerParams(dimension_semantics=("arbitrary",), vmem_limit_bytes=VMEM_LIMIT),
        name="moe_experts",
    )(blk_e, nact, xs_rt, w_gate, w_up, w_down)


def _combine_ple_kernel(dest_ref, h1_ref, wt_ref, p_ref, y_ref, pleg_ref, gatew_ref, projw_ref, postg_ref,
                        out_ref, ybuf_ref, sem, *, n_tok):
    t = COMBINE_TILE
    i = pl.program_id(0)

    def gather(step, slot):
        def issue(g, carry):
            for u in range(DMA_UNROLL // TOP_K):
                r = g * (DMA_UNROLL // TOP_K) + u
                for k in range(TOP_K):
                    _row_copy(y_ref, dest_ref[k * n_tok + step * t + r], ybuf_ref, (slot * TOP_K + k) * t + r,
                              sem.at[slot]).start(priority=k % 2)
            return carry
        lax.fori_loop(0, t // (DMA_UNROLL // TOP_K), issue, 0)

    @pl.when(i == 0)
    def _():
        gather(0, 0)

    @pl.when(i + 1 < pl.num_programs(0))
    def _():
        gather(i + 1, (i + 1) % 2)

    e = _rms(_dot(p_ref[...].astype(BF16), projw_ref[...]), postg_ref[...])
    slot = i % 2
    _rows_wait(y_ref, ybuf_ref, slot * TOP_K * t, TOP_K * t, sem.at[slot])
    wt = wt_ref[...].T
    y0 = _load_row_tiled(ybuf_ref, t, (slot * TOP_K) * (t * ROW_TILES))
    y1 = _load_row_tiled(ybuf_ref, t, (slot * TOP_K + 1) * (t * ROW_TILES))
    h2 = h1_ref[...] + (wt[:, 0:1] * y0 + wt[:, 1:2] * y1)
    gate = jax.nn.sigmoid(_dot(_rms(h2, pleg_ref[...]).astype(BF16), gatew_ref[...]))
    out_ref[...] = h2 + gate * e


def _combine_ple(dest, h1, wt, p, y_rt, lw):
    n = h1.shape[0]
    t = COMBINE_TILE
    row = lambda w: pl.BlockSpec((t, w), lambda i, d: (i, 0))
    layer_rows = lw["layer"] * (n // t)
    weights = [lw[k] for k in ("ple_g", "gate_w", "proj_w", "post_g")]
    return pl.pallas_call(
        functools.partial(_combine_ple_kernel, n_tok=n),
        grid_spec=pltpu.PrefetchScalarGridSpec(
            num_scalar_prefetch=1,
            grid=(n // t,),
            in_specs=[row(D_MODEL), pl.BlockSpec((SUBLANES, t), lambda i, d: (0, i)), pl.BlockSpec((t, p.shape[1]), lambda i, d: (layer_rows + i, 0)),
                      pl.BlockSpec(memory_space=pl.ANY)]
            + [pl.BlockSpec(w.shape, lambda i, d, nd=w.ndim: (0,) * nd) for w in weights],
            out_specs=row(D_MODEL),
            scratch_shapes=[pltpu.VMEM((2 * TOP_K * t * ROW_TILES, LANES), F32), pltpu.SemaphoreType.DMA((2,))],
        ),
        out_shape=jax.ShapeDtypeStruct((n, D_MODEL), F32),
        compiler_params=pltpu.CompilerParams(dimension_semantics=("arbitrary",), vmem_limit_bytes=VMEM_LIMIT),
        name="combine_ple",
    )(dest, h1, wt, p, y_rt, *weights)


def _segment_matrix(seg_ids):
    seg_ids = jnp.asarray(seg_ids)
    same = (seg_ids[:, None] == seg_ids[None, :]).astype(F32)
    return (same / jnp.sum(same, axis=1, keepdims=True)).astype(BF16)


def _constants():
    lane = jnp.arange(256)
    qk_seg = (lane // HEAD_PAD) * 3 + jnp.where(lane % HEAD_PAD < MLA_NOPE, 0, jnp.where(lane % HEAD_PAD < MLA_QK, 1, 2))
    i = jnp.arange(RANK_CHUNK)
    e = jnp.arange(N_EXPERTS)
    return {
        "g64": _segment_matrix(lane // HEAD_DIM),
        "gqk": _segment_matrix(qk_seg),
        "upper": (i[:, None] < i[None, :]).astype(BF16),
        "ones": jnp.ones((RANK_CHUNK, LANES), BF16),
        "ltri": (e[None, :] < e[:, None]).astype(BF16),
    }


def _layer_weights(i, consts, p):
    row = lambda v: v.reshape(1, -1).astype(F32)
    w_in = p["w_in"][i]
    o_sgu, o_conv, o_q, o_kv = 2 * SGU_WIDTH, 2 * SGU_WIDTH + 2 * CONV_WIDTH, 0, 0
    o_q = o_conv + Q_RANK
    o_kv = o_q + KV_RANK
    w_ckv = jnp.concatenate([w_in[:, o_q:], jnp.zeros((D_MODEL, 256 - KV_RANK - MLA_ROPE), F32)], axis=1)

    causal = jnp.tril(jnp.ones((CHUNK, CHUNK), F32))
    wcat = jnp.transpose(p["sgu_w"][i] * causal, (1, 0, 2)).reshape(CHUNK, SGU_HEADS * CHUNK)
    sbias = jnp.broadcast_to(jnp.transpose(p["sgu_b"][i])[:, :, None],
                             (CHUNK, SGU_HEADS, HEAD_DIM)).reshape(CHUNK, SGU_WIDTH)

    w_uq = p["w_uq"][i].reshape(Q_RANK, MLA_HEADS, MLA_QK)
    w_uq = jnp.pad(w_uq, ((0, 0), (0, 0), (0, HEAD_PAD - MLA_QK))).reshape(Q_RANK, MLA_HEADS * HEAD_PAD)
    qn_g = p["q_norm_g"][i]
    q_gain = jnp.tile(jnp.concatenate([qn_g, jnp.zeros((HEAD_PAD - MLA_QK,), F32)]), MLA_HEADS)
    w_ukv = p["w_ukv"][i].reshape(KV_RANK, MLA_HEADS, MLA_NOPE + MLA_V)
    w_ukt = jnp.transpose(w_ukv[:, :, :MLA_NOPE], (1, 2, 0)).reshape(MLA_HEADS * MLA_NOPE, KV_RANK)
    w_v = w_ukv[:, :, MLA_NOPE:].reshape(KV_RANK, MLA_HEADS // 2, 2, MLA_V)
    zeros_v = jnp.zeros((KV_RANK, MLA_HEADS // 2, MLA_V), F32)
    w_uv = jnp.stack([w_v[:, :, 0], zeros_v, zeros_v, w_v[:, :, 1]], axis=2).reshape(KV_RANK, MLA_HEADS * HEAD_PAD)
    v_ones = jnp.tile(jnp.repeat(jnp.array([0.0, 1.0, 1.0, 0.0], F32), MLA_V), MLA_HEADS // 2)
    kn_g = p["k_norm_g"][i]

    bg = p["branch_norm_g"][i]
    w_o = p["w_o"][i]
    wr = jnp.concatenate([p["router_group_w"][i], jnp.zeros((D_MODEL, 8 - N_GROUPS), F32),
                          p["router_expert_w"][i], jnp.zeros((D_MODEL, LANES - 8 - N_EXPERTS), F32)], axis=1)
    wr_hi = wr.astype(BF16)
    br = jnp.concatenate([p["router_group_b"][i], jnp.full((8 - N_GROUPS,), NEG_BIG, F32),
                          p["router_expert_b"][i], jnp.zeros((LANES - 8 - N_EXPERTS,), F32)])
    return {
        "mix_g": row(p["mix_norm_g"][i]),
        "w_sgu": w_in[:, :o_sgu].astype(BF16),
        "w_conv": w_in[:, o_sgu:o_conv].astype(BF16),
        "w_cq": w_in[:, o_conv:o_q].astype(BF16),
        "w_ckv": w_ckv.astype(BF16),
        "sgu_ln_g": row(p["sgu_ln_g"][i]), "sgu_ln_b": row(p["sgu_ln_b"][i]),
        "sgu_wcat": wcat.astype(BF16), "sgu_bias": sbias, "g64": consts["g64"], "bg_a": row(bg[:SGU_WIDTH]),
        "conv_w": jnp.pad(p["conv_w"][i], ((0, 1), (0, 0))), "conv_b": row(p["conv_b"][i]),
        "conv_ln_g": row(p["conv_ln_g"][i]), "conv_ln_b": row(p["conv_ln_b"][i]),
        "pw_w": p["conv_pw_w"][i].astype(BF16), "pw_b": row(p["conv_pw_b"][i]),
        "bg_b": row(bg[SGU_WIDTH:SGU_WIDTH + CONV_WIDTH]),
        "qa_g": row(p["q_a_norm_g"][i]), "w_uq": w_uq.astype(BF16), "gqk": consts["gqk"],
        "q_gain": row(q_gain),
        "kva_g": row(p["kv_a_norm_g"][i]), "w_ukt": w_ukt.astype(BF16), "w_uv": w_uv.astype(BF16),
        "v_ones": row(v_ones),
        "kn_g": jnp.broadcast_to(kn_g[:MLA_NOPE, None], (MLA_NOPE, MIX_TILE)),
        "kpe_g": jnp.broadcast_to(kn_g[MLA_NOPE:, None], (MLA_ROPE, MIX_TILE)),
        "bg_c": row(bg[SGU_WIDTH + CONV_WIDTH:]),
        "w_o_a": w_o[:SGU_WIDTH].astype(BF16),
        "w_o_b": w_o[SGU_WIDTH:SGU_WIDTH + CONV_WIDTH].astype(BF16),
        "w_o_c": w_o[SGU_WIDTH + CONV_WIDTH:].astype(BF16),
        "ffn_g": row(p["ffn_norm_g"][i]),
        "wr_split": jnp.concatenate([wr_hi, (wr - wr_hi.astype(F32)).astype(BF16)], axis=1), "br": row(br),
        "layer": i, "w_gate": p["moe_w_gate"], "w_up": p["moe_w_up"], "w_down": p["moe_w_down"],
        "ple_g": row(p["ple_norm_g"][i]), "gate_w": p["ple_gate_w"][i].astype(BF16),
        "proj_w": p["ple_proj_w"][i].astype(BF16), "post_g": row(p["ple_post_norm_g"][i]),
    }


def _moe(h1, m, eidx, wts, pl_i, lw, consts):
    n = h1.shape[0]
    total = TOP_K * n
    rows = total + N_EXPERTS * EXPERT_ROWS
    dest2d, counts = _moe_rank(eidx[:TOP_K].reshape(1, total), consts)
    dest = dest2d.reshape(total)
    cnt = counts[:, 0].astype(jnp.int32)
    padded = (cnt + EXPERT_ROWS - 1) // EXPERT_ROWS * EXPERT_ROWS
    pend = jnp.cumsum(padded)
    nblk = rows // EXPERT_ROWS
    blk_start = jnp.arange(nblk, dtype=jnp.int32) * EXPERT_ROWS
    blk_e = jnp.minimum(jnp.sum((pend[None, :] <= blk_start[:, None]).astype(jnp.int32), axis=1), N_EXPERTS - 1)
    nact = (pend[-1:] // EXPERT_ROWS).astype(jnp.int32)
    region = jnp.concatenate([jnp.zeros((1,), jnp.int32), pend.astype(jnp.int32)])
    xs = _moe_dispatch(dest, cnt, region, m, rows)
    y = _moe_experts(blk_e, nact, xs, lw["layer"], lw["w_gate"], lw["w_up"], lw["w_down"])
    return _combine_ple(dest, h1, wts, pl_i, y, lw)


def kernel(x, p, positions, mix_norm_g, w_in, sgu_ln_g, sgu_ln_b, sgu_w, sgu_b, conv_w, conv_b, conv_ln_g, conv_ln_b, conv_pw_w, conv_pw_b, q_a_norm_g, w_uq, kv_a_norm_g, w_ukv, q_norm_g, k_norm_g, branch_norm_g, w_o, ffn_norm_g, router_group_w, router_group_b, router_expert_w, router_expert_b, moe_w_gate, moe_w_up, moe_w_down, ple_norm_g, ple_gate_w, ple_proj_w, ple_post_norm_g):
    params = dict(
        mix_norm_g=mix_norm_g, w_in=w_in, sgu_ln_g=sgu_ln_g, sgu_ln_b=sgu_ln_b, sgu_w=sgu_w, sgu_b=sgu_b,
        conv_w=conv_w, conv_b=conv_b, conv_ln_g=conv_ln_g, conv_ln_b=conv_ln_b, conv_pw_w=conv_pw_w,
        conv_pw_b=conv_pw_b, q_a_norm_g=q_a_norm_g, w_uq=w_uq, kv_a_norm_g=kv_a_norm_g, w_ukv=w_ukv,
        q_norm_g=q_norm_g, k_norm_g=k_norm_g, branch_norm_g=branch_norm_g, w_o=w_o, ffn_norm_g=ffn_norm_g,
        router_group_w=router_group_w, router_group_b=router_group_b, router_expert_w=router_expert_w,
        router_expert_b=router_expert_b, moe_w_gate=moe_w_gate, moe_w_up=moe_w_up, moe_w_down=moe_w_down,
        ple_norm_g=ple_norm_g, ple_gate_w=ple_gate_w, ple_proj_w=ple_proj_w, ple_post_norm_g=ple_post_norm_g)
    batch, seq, d = x.shape
    n = batch * seq
    depth = w_in.shape[0]
    consts = _constants()
    cos, sin = _rope_tables(positions)
    h = x.reshape(n, d)
    p_rows = p.reshape(depth * n, p.shape[-1])
    for i in range(depth):
        lw = _layer_weights(i, consts, params)
        ya, yb, q, kt, v = _mixer_pre(h, cos, sin, lw, batch, seq)
        yc = _attention(q, kt, v, batch, seq)
        h1, m, eidx, wts = _outproj_router(h, ya, yb, yc, lw)
        h = _moe(h1, m, eidx, wts, p_rows, lw, consts)
    return h.reshape(batch, seq, d)
```

```python
import functools
import math

import jax
import jax.numpy as jnp
from jax import lax
from jax.experimental import pallas as pl
from jax.experimental.pallas import tpu as pltpu

F32 = jnp.float32
BF16 = jnp.bfloat16

D_MODEL = 1024
HEAD_DIM = 64
SGU_HEADS = 4
SGU_WIDTH = 256
CHUNK = 128
CONV_WIDTH = 256
CONV_KERNEL = 31
MLA_HEADS = 8
MLA_NOPE = 64
MLA_ROPE = 32
MLA_QK = MLA_NOPE + MLA_ROPE
MLA_V = 64
Q_RANK = 256
KV_RANK = 128
ROPE_THETA = 10000.0
N_GROUPS = 4
EXPERTS_PER_GROUP = 8
N_EXPERTS = 32
D_EXPERT = 256
TOP_K = 2
EPS = 1e-6

LANES = 128
SUBLANES = 8
HEAD_PAD = 128
MIX_TILE = 1024
ROUTER_TILE = 1024
CONV_HALO = 32
CONV_SHIFTED_ROWS = MIX_TILE + CONV_HALO - SUBLANES
MIXER_ORDER = ("q", "sgu", "kv", "conv")
ATT_TQ = 256
ATT_PAIRS = 2
RANK_CHUNK = 512
RANK_STEP = 4096
EXPERT_ROWS = 512
DISPATCH_TILE = 2048
COMBINE_TILE = 512
ROW_TILES = D_MODEL // LANES
DMA_UNROLL = 8
VMEM_LIMIT = 48 * 1024 * 1024
NEG_BIG = -1e30


def _dot(a, b):
    return jnp.dot(a, b, preferred_element_type=F32)


def _rms(x, g):
    ms = jnp.mean(x * x, axis=-1, keepdims=True)
    return x * lax.rsqrt(ms + EPS) * g


def _seg_mean(x, g_ref, split=True):
    g = g_ref[...]
    outs = []
    for c in range(x.shape[1] // 256):
        xb = x[:, c * 256:(c + 1) * 256]
        hi = xb.astype(BF16)
        acc = _dot(hi, g)
        if split:
            acc = acc + _dot((xb - hi.astype(F32)).astype(BF16), g)
        outs.append(acc)
    return outs[0] if len(outs) == 1 else jnp.concatenate(outs, axis=-1)


def _group_layernorm(x, g_ref, gain, bias):
    d = x - _seg_mean(x, g_ref)
    var = _seg_mean(d * d, g_ref)
    return d * lax.rsqrt(var + EPS) * gain + bias


def _rope_table_kernel(pos_ref, invf_ref, cos_ref, sin_ref):
    ang = pos_ref[...].astype(F32) * invf_ref[...]
    cos_ref[...] = jnp.cos(ang)
    sin_ref[...] = jnp.sin(ang)


def _rope_tables(positions):
    b, s = positions.shape
    n = b * s
    half = MLA_ROPE // 2
    inv_freq = ROPE_THETA ** (-jnp.arange(0, MLA_ROPE, 2, dtype=F32) / MLA_ROPE)
    pos_rep = jnp.broadcast_to(positions.reshape(n, 1), (n, half)).reshape(n * half // LANES, LANES)
    invf_rep = jnp.tile(inv_freq, LANES // half).reshape(1, LANES)
    shape = jax.ShapeDtypeStruct(pos_rep.shape, F32)
    cos, sin = pl.pallas_call(_rope_table_kernel, out_shape=(shape, shape), name="rope_tables")(pos_rep, invf_rep)
    return cos.reshape(n, half), sin.reshape(n, half)


def _mixer_pre_kernel(
        h_ref, cos_ref, sin_ref,
        mixg_ref, wsgu_ref, wconv_ref, wcq_ref, wckv_ref,
        slng_ref, slnb_ref, wcat_ref, sbias_ref, g64_ref, bga_ref,
        cw_ref, cb_ref, clng_ref, clnb_ref, pww_ref, pwb_ref, bgb_ref,
        qag_ref, wuq_ref, gqk_ref, qgain_ref,
        kvag_ref, wukt_ref, wuv_ref, vones_ref, kng_ref, kpeg_ref,
        ya_ref, yb_ref, q_ref, kt_ref, v_ref,
        ybuf_ref, ysh_ref, *, tiles_per_seq):
    t = MIX_TILE
    a = _rms(h_ref[...], mixg_ref[...]).astype(BF16)
    zs = _dot(a, wsgu_ref[...])
    zc = _dot(a, wconv_ref[...])
    zq = _dot(a, wcq_ref[...])
    zkv = _dot(a, wckv_ref[...])

    half = MLA_ROPE // 2
    cos, sin = cos_ref[...], sin_ref[...]
    zn = jnp.zeros((t, MLA_NOPE), F32)
    zh = jnp.zeros((t, half), F32)
    zp = jnp.zeros((t, HEAD_PAD - MLA_QK), F32)
    rc = jnp.concatenate([jnp.ones((t, MLA_NOPE), F32), cos, cos, zp], axis=1)
    rs1 = jnp.concatenate([zn, -sin, zh, zp], axis=1)
    rs2 = jnp.concatenate([zn, zh, sin, zp], axis=1)

    def sgu_branch():
        zg = jax.nn.gelu(zs)
        u = zg[:, :SGU_WIDTH]
        vn = _group_layernorm(zg[:, SGU_WIDTH:], g64_ref, slng_ref[...], slnb_ref[...])
        lane_head = lax.broadcasted_iota(jnp.int32, (CHUNK, SGU_WIDTH), 1) // HEAD_DIM
        wcat = wcat_ref[...]
        sbias = sbias_ref[...]
        parts = []
        for c in range(t // CHUNK):
            vc = vn[c * CHUNK:(c + 1) * CHUNK]
            stacked = jnp.concatenate(
                [jnp.where(lane_head == hh, vc, 0.0) for hh in range(SGU_HEADS)], axis=0).astype(BF16)
            s = _dot(wcat, stacked) + sbias
            parts.append(u[c * CHUNK:(c + 1) * CHUNK] * s)
        ya_ref[...] = _rms(jnp.concatenate(parts, axis=0), bga_ref[...]).astype(BF16)

    def conv_branch():
        yg = zc[:, :CONV_WIDTH] * jax.nn.sigmoid(zc[:, CONV_WIDTH:])
        first = (pl.program_id(0) % tiles_per_seq) == 0

        @pl.when(first)
        def _():
            ybuf_ref[0:CONV_HALO, :] = jnp.zeros((CONV_HALO, CONV_WIDTH), F32)

        @pl.when(jnp.logical_not(first))
        def _():
            ybuf_ref[0:CONV_HALO, :] = ybuf_ref[t:t + CONV_HALO, :]

        ybuf_ref[CONV_HALO:CONV_HALO + t, :] = yg
        for sh in range(1, SUBLANES):
            ysh_ref[sh - 1] = ybuf_ref[pl.ds(sh, CONV_SHIFTED_ROWS), :]
        rows = 64
        first_tap_row = CONV_HALO - (CONV_KERNEL - 1)
        conv_parts = []
        for r in range(t // rows):
            acc = jnp.broadcast_to(cb_ref[...], (rows, CONV_WIDTH))
            for tap in range(CONV_KERNEL):
                sh = (first_tap_row + tap) % SUBLANES
                start = r * rows + first_tap_row + tap - sh
                src = ybuf_ref[pl.ds(start, rows), :] if sh == 0 else ysh_ref[sh - 1, pl.ds(start, rows), :]
                acc = acc + cw_ref[tap:tap + 1, :] * src
            conv_parts.append(acc)
        cv = jnp.concatenate(conv_parts, axis=0)
        cn = _group_layernorm(cv, g64_ref, clng_ref[...], clnb_ref[...])
        yb = _dot(jax.nn.silu(cn).astype(BF16), pww_ref[...]) + pwb_ref[...]
        yb_ref[...] = _rms(yb, bgb_ref[...]).astype(BF16)

    def query_branch():
        cqn = _rms(zq, qag_ref[...]).astype(BF16)
        qf = _dot(cqn, wuq_ref[...])
        qn = qf * lax.rsqrt(_seg_mean(qf * qf, gqk_ref, split=False) + EPS) * qgain_ref[...]
        for hh in range(MLA_HEADS):
            blk = qn[:, hh * HEAD_PAD:(hh + 1) * HEAD_PAD]
            rot = blk * rc + pltpu.roll(blk, HEAD_PAD - half, 1) * rs1 + pltpu.roll(blk, half, 1) * rs2
            q_ref[:, hh * HEAD_PAD:(hh + 1) * HEAD_PAD] = rot.astype(BF16)

    def key_value_branch():
        ckvn = _rms(zkv[:, :KV_RANK], kvag_ref[...]).astype(BF16)
        v_ref[...] = (_dot(ckvn, wuv_ref[...]) + vones_ref[...]).astype(BF16)
        knt = lax.dot_general(wukt_ref[...], ckvn, (((1,), (1,)), ((), ())), preferred_element_type=F32)
        x = zkv[:, KV_RANK:].T[0:MLA_ROPE]
        xn = x * lax.rsqrt(jnp.mean(x * x, axis=0, keepdims=True) + EPS) * kpeg_ref[...]
        x1, x2 = xn[:MLA_ROPE // 2], xn[MLA_ROPE // 2:]
        cos_t = rc.T[MLA_NOPE:MLA_NOPE + half]
        sin_t = rs2.T[MLA_NOPE + half:MLA_QK]
        kpe = jnp.concatenate([x1 * cos_t - x2 * sin_t, x2 * cos_t + x1 * sin_t], axis=0)
        pad = jnp.zeros((HEAD_PAD - MLA_QK, t), F32)
        kng = kng_ref[...]
        for hh in range(MLA_HEADS):
            blk = knt[hh * MLA_NOPE:(hh + 1) * MLA_NOPE]
            kn = blk * lax.rsqrt(jnp.mean(blk * blk, axis=0, keepdims=True) + EPS) * kng
            kt_ref[0, hh * HEAD_PAD:(hh + 1) * HEAD_PAD, :] = jnp.concatenate([kn, kpe, pad], axis=0).astype(BF16)

    branches = {"sgu": sgu_branch, "conv": conv_branch, "q": query_branch, "kv": key_value_branch}
    for name in MIXER_ORDER:
        branches[name]()


def _full(shape):
    nd = len(shape)
    return pl.BlockSpec(shape, lambda *_: (0,) * nd)


def _mixer_pre(h, cos, sin, lw, batch, seq):
    n = h.shape[0]
    t = MIX_TILE
    tps = seq // t
    row = lambda w: pl.BlockSpec((t, w), lambda i: (i, 0))
    weights = [lw[k] for k in (
        "mix_g", "w_sgu", "w_conv", "w_cq", "w_ckv",
        "sgu_ln_g", "sgu_ln_b", "sgu_wcat", "sgu_bias", "g64", "bg_a",
        "conv_w", "conv_b", "conv_ln_g", "conv_ln_b", "pw_w", "pw_b", "bg_b",
        "qa_g", "w_uq", "gqk", "q_gain",
        "kva_g", "w_ukt", "w_uv", "v_ones", "kn_g", "kpe_g")]
    in_specs = [row(D_MODEL), row(MLA_ROPE // 2), row(MLA_ROPE // 2)] + [_full(w.shape) for w in weights]
    out_shape = (
        jax.ShapeDtypeStruct((n, SGU_WIDTH), BF16),
        jax.ShapeDtypeStruct((n, CONV_WIDTH), BF16),
        jax.ShapeDtypeStruct((n, MLA_HEADS * HEAD_PAD), BF16),
        jax.ShapeDtypeStruct((batch, MLA_HEADS * HEAD_PAD, seq), BF16),
        jax.ShapeDtypeStruct((n, MLA_HEADS * HEAD_PAD), BF16),
    )
    out_specs = (
        row(SGU_WIDTH), row(CONV_WIDTH), row(MLA_HEADS * HEAD_PAD),
        pl.BlockSpec((1, MLA_HEADS * HEAD_PAD, t), lambda i: (i // tps, 0, i % tps)),
        row(MLA_HEADS * HEAD_PAD),
    )
    return pl.pallas_call(
        functools.partial(_mixer_pre_kernel, tiles_per_seq=tps),
        grid=(n // t,),
        in_specs=in_specs,
        out_specs=out_specs,
        out_shape=out_shape,
        scratch_shapes=[pltpu.VMEM((t + CONV_HALO, CONV_WIDTH), F32),
                        pltpu.VMEM((SUBLANES - 1, CONV_SHIFTED_ROWS, CONV_WIDTH), F32)],
        compiler_params=pltpu.CompilerParams(dimension_semantics=("arbitrary",), vmem_limit_bytes=VMEM_LIMIT),
        name="mixer_pre",
    )(h, cos, sin, *weights)


def _attention_kernel(q_ref, kt_ref, v_ref, o_ref, *, seq):
    tq = ATT_TQ
    exp2_scale = MLA_QK ** -0.5 * math.log2(math.e)
    row = lax.broadcasted_iota(jnp.int32, (tq, tq), 0)
    col = lax.broadcasted_iota(jnp.int32, (tq, tq), 1)
    lane = lax.broadcasted_iota(jnp.int32, (tq, HEAD_PAD), 1)
    def scores(pair, qi, hh):
        nk = (qi + 1) * tq
        head = (2 * pair + hh) * HEAD_PAD
        s = _dot(q_ref[qi * tq:(qi + 1) * tq, head:head + HEAD_PAD], kt_ref[0, head:head + HEAD_PAD, 0:nk])
        diag = jnp.where(col <= row, s[:, nk - tq:], NEG_BIG)
        return diag if qi == 0 else jnp.concatenate([s[:, :nk - tq], diag], axis=1)

    n_blocks = seq // tq
    items = [(pair, qi) for pair in range(ATT_PAIRS) for qi in range(n_blocks)]
    ahead = [scores(*items[0], hh) for hh in range(2)]
    for idx, (pair, qi) in enumerate(items):
        s_pair, ahead = ahead, []
        if idx + 1 < len(items):
            ahead = [scores(*items[idx + 1], hh) for hh in range(2)]
        p_pair = [jnp.exp2(((s - jnp.max(s, axis=-1, keepdims=True)) * exp2_scale).astype(BF16)) for s in s_pair]
        vcols = slice(2 * pair * HEAD_PAD, 2 * (pair + 1) * HEAD_PAD)
        acc = _dot(jnp.concatenate(p_pair, axis=0), v_ref[0:(qi + 1) * tq, vcols])
        a0, a1 = acc[:tq, :HEAD_PAD], acc[tq:, HEAD_PAD:]
        o_ref[qi * tq:(qi + 1) * tq, 2 * pair * MLA_V:2 * (pair + 1) * MLA_V] = jnp.where(
            lane < MLA_V, a0 / pltpu.roll(a0, MLA_V, 1), a1 / pltpu.roll(a1, MLA_V, 1))


def _attention(q, kt, v, batch, seq):
    n = q.shape[0]
    width = 2 * ATT_PAIRS
    return pl.pallas_call(
        functools.partial(_attention_kernel, seq=seq),
        grid=(batch, MLA_HEADS // width),
        in_specs=[
            pl.BlockSpec((seq, width * HEAD_PAD), lambda b, p: (b, p)),
            pl.BlockSpec((1, width * HEAD_PAD, seq), lambda b, p: (b, p, 0)),
            pl.BlockSpec((seq, width * HEAD_PAD), lambda b, p: (b, p)),
        ],
        out_specs=pl.BlockSpec((seq, width * MLA_V), lambda b, p: (b, p)),
        out_shape=jax.ShapeDtypeStruct((n, MLA_HEADS * MLA_V), F32),
        compiler_params=pltpu.CompilerParams(
            dimension_semantics=("arbitrary", "arbitrary"), vmem_limit_bytes=VMEM_LIMIT),
        name="attention",
    )(q, kt, v)


def _outproj_router_kernel(h_ref, ya_ref, yb_ref, yc_ref, bgc_ref, woa_ref, wob_ref, woc_ref,
                           ffng_ref, wr_ref, br_ref,
                           h1_ref, m_ref, eidx_ref, wts_ref):
    t = ROUTER_TILE
    ycn = _rms(yc_ref[...], bgc_ref[...])
    proj = (_dot(ya_ref[...], woa_ref[...]) + _dot(yb_ref[...], wob_ref[...])
            + _dot(ycn.astype(BF16), woc_ref[...]))
    h1 = h_ref[...] + proj
    h1_ref[...] = h1
    m = _rms(h1, ffng_ref[...])
    _store_row_tiled(m_ref, m)

    mh = m.astype(BF16)
    ml = (m - mh.astype(F32)).astype(BF16)
    wr = wr_ref[...]
    both = _dot(mh, wr)
    logits = both[:, :LANES] + both[:, LANES:] + _dot(ml, wr[:, :LANES]) + br_ref[...]
    lt = logits.T
    rowi = lax.broadcasted_iota(jnp.int32, (EXPERTS_PER_GROUP, t), 0)
    g8 = lt[0:8]
    gmax = jnp.max(g8, axis=0, keepdims=True)
    gsum = jnp.sum(jnp.exp(g8 - gmax), axis=0, keepdims=True)
    gidx = jnp.min(jnp.where(g8 == gmax, rowi, 8), axis=0, keepdims=True)
    g_w = 1.0 / gsum
    esel = jnp.zeros((EXPERTS_PER_GROUP, t), F32)
    for g in range(N_GROUPS):
        esel = jnp.where(gidx == g, lt[8 + g * EXPERTS_PER_GROUP:8 + (g + 1) * EXPERTS_PER_GROUP], esel)
    ep = jnp.exp(esel - jnp.max(esel, axis=0, keepdims=True))
    eprob = ep / jnp.sum(ep, axis=0, keepdims=True)
    v1 = jnp.max(eprob, axis=0, keepdims=True)
    i1 = jnp.min(jnp.where(eprob == v1, rowi, 8), axis=0, keepdims=True)
    rest = jnp.where(rowi == i1, -1.0, eprob)
    v2 = jnp.max(rest, axis=0, keepdims=True)
    i2 = jnp.min(jnp.where(rest == v2, rowi, 8), axis=0, keepdims=True)
    den = v1 + v2
    e1 = gidx * EXPERTS_PER_GROUP + i1
    e2 = gidx * EXPERTS_PER_GROUP + i2
    eidx_ref[...] = jnp.where(rowi == 0, e1, jnp.where(rowi == 1, e2, 0))
    wts_ref[...] = jnp.where(rowi == 0, g_w * (v1 / den), jnp.where(rowi == 1, g_w * (v2 / den), 0.0))


def _outproj_router(h, ya, yb, yc, lw):
    n = h.shape[0]
    t = ROUTER_TILE
    row = lambda w: pl.BlockSpec((t, w), lambda i: (i, 0))
    weights = [lw[k] for k in ("bg_c", "w_o_a", "w_o_b", "w_o_c", "ffn_g", "wr_split", "br")]
    colspec = pl.BlockSpec((8, t), lambda i: (0, i))
    return pl.pallas_call(
        _outproj_router_kernel,
        grid=(n // t,),
        in_specs=[row(D_MODEL), row(SGU_WIDTH), row(CONV_WIDTH), row(MLA_HEADS * MLA_V)]
        + [_full(w.shape) for w in weights],
        out_specs=(row(D_MODEL), pl.BlockSpec((t * ROW_TILES, LANES), lambda i: (i, 0)), colspec, colspec),
        out_shape=(
            jax.ShapeDtypeStruct((n, D_MODEL), F32),
            jax.ShapeDtypeStruct((n * ROW_TILES, LANES), F32),
            jax.ShapeDtypeStruct((8, n), jnp.int32),
            jax.ShapeDtypeStruct((8, n), F32),
        ),
        compiler_params=pltpu.CompilerParams(dimension_semantics=("arbitrary",), vmem_limit_bytes=VMEM_LIMIT),
        name="outproj_router",
    )(h, ya, yb, yc, *weights)


def _moe_rank_kernel(e_ref, upper_ref, ones_ref, ltri_ref, dest_ref, cnt_ref, run_ref, base_ref):
    phase = pl.program_id(0)
    step = pl.program_id(1)
    c = RANK_CHUNK
    expert = lax.broadcasted_iota(jnp.int32, (N_EXPERTS, c), 0)

    @pl.when(jnp.logical_and(phase == 0, step == 0))
    def _():
        run_ref[...] = jnp.zeros_like(run_ref)
        base_ref[...] = jnp.zeros_like(base_ref)

    @pl.when(jnp.logical_and(phase == 1, step == 0))
    def _():
        blocks = jnp.floor((run_ref[...] + (EXPERT_ROWS - 1)) * (1.0 / EXPERT_ROWS))
        base_ref[...] = _dot(ltri_ref[...], blocks.astype(BF16)) * EXPERT_ROWS
        run_ref[...] = jnp.zeros_like(run_ref)

    def chunk_onehot(sub):
        return jnp.where(expert == e_ref[:, sub * c:(sub + 1) * c], 1.0, 0.0)

    @pl.when(phase == 0)
    def _():
        dest_ref[...] = jnp.zeros_like(dest_ref)
        for sub in range(RANK_STEP // c):
            run_ref[...] = run_ref[...] + _dot(chunk_onehot(sub).astype(BF16), ones_ref[...])

    @pl.when(phase == 1)
    def _():
        for sub in range(RANK_STEP // c):
            onehot = chunk_onehot(sub)
            oh16 = onehot.astype(BF16)
            before = _dot(oh16, upper_ref[...])
            pos = before + run_ref[:, 0:1] + base_ref[:, 0:1]
            dest_ref[:, sub * c:(sub + 1) * c] = jnp.sum(onehot * pos, axis=0, keepdims=True).astype(jnp.int32)
            run_ref[...] = run_ref[...] + _dot(oh16, ones_ref[...])

    cnt_ref[...] = run_ref[...]


def _moe_rank(e_flat, consts):
    total = e_flat.shape[1]
    c = RANK_CHUNK
    st = RANK_STEP
    return pl.pallas_call(
        _moe_rank_kernel,
        grid=(2, total // st),
        in_specs=[pl.BlockSpec((1, st), lambda p, s: (0, s)),
                  _full((c, c)), _full((c, LANES)), _full((N_EXPERTS, N_EXPERTS))],
        out_specs=(pl.BlockSpec((1, st), lambda p, s: (0, s * p)),
                   pl.BlockSpec((N_EXPERTS, LANES), lambda p, s: (0, 0))),
        out_shape=(jax.ShapeDtypeStruct((1, total), jnp.int32),
                   jax.ShapeDtypeStruct((N_EXPERTS, LANES), F32)),
        scratch_shapes=[pltpu.VMEM((N_EXPERTS, LANES), F32), pltpu.VMEM((N_EXPERTS, LANES), F32)],
        compiler_params=pltpu.CompilerParams(dimension_semantics=("arbitrary", "arbitrary")),
        name="moe_rank",
    )(e_flat, consts["upper"], consts["ones"], consts["ltri"])


def _store_row_tiled(ref, x, offset=0):
    rows = x.shape[0]
    for s in range(ROW_TILES):
        ref[pl.ds(offset + s, rows, stride=ROW_TILES), :] = x[:, s * LANES:(s + 1) * LANES]


def _load_row_tiled(ref, rows, offset=0):
    return jnp.concatenate(
        [ref[pl.ds(offset + s, rows, stride=ROW_TILES), :] for s in range(ROW_TILES)], axis=1)


def _row_copy(src, src_row, dst, dst_row, sem):
    return pltpu.make_async_copy(
        src.at[pl.ds(pl.multiple_of(src_row * ROW_TILES, ROW_TILES), ROW_TILES)],
        dst.at[pl.ds(pl.multiple_of(dst_row * ROW_TILES, ROW_TILES), ROW_TILES)], sem)


def _rows_wait(src, dst, dst_row, rows, sem):
    pltpu.make_async_copy(
        src.at[pl.ds(0, rows * ROW_TILES)],
        dst.at[pl.ds(pl.multiple_of(dst_row * ROW_TILES, ROW_TILES), rows * ROW_TILES)], sem).wait()


def _moe_dispatch_kernel(dest_ref, cnt_ref, region_ref, m_ref, xs_ref, zbuf_ref, sem, zsem, *, n_tok):
    t = DISPATCH_TILE
    r = EXPERT_ROWS
    i = pl.program_id(0)

    @pl.when(i == 0)
    def _():
        zbuf_ref[...] = jnp.zeros_like(zbuf_ref)

        def zero_copy(e):
            last = pl.multiple_of((region_ref[e + 1] - r) * ROW_TILES, r * ROW_TILES)
            return pltpu.make_async_copy(zbuf_ref, xs_ref.at[pl.ds(last, r * ROW_TILES)], zsem)

        def start(e, carry):
            @pl.when(cnt_ref[e] > 0)
            def _():
                zero_copy(e).start()
            return carry

        def wait(e, carry):
            @pl.when(cnt_ref[e] > 0)
            def _():
                zero_copy(e).wait()
            return carry

        def tail_copy(blk):
            return pltpu.make_async_copy(
                zbuf_ref, xs_ref.at[pl.ds(pl.multiple_of(blk * (r * ROW_TILES), r * ROW_TILES), r * ROW_TILES)], zsem)

        def tail_start(blk, carry):
            tail_copy(blk).start()
            return carry

        def tail_wait(blk, carry):
            tail_copy(blk).wait()
            return carry

        first_unused = region_ref[N_EXPERTS] // r
        n_blocks = xs_ref.shape[0] // (r * ROW_TILES)
        lax.fori_loop(0, N_EXPERTS, start, 0)
        lax.fori_loop(first_unused, n_blocks, tail_start, 0)
        lax.fori_loop(0, N_EXPERTS, wait, 0)
        lax.fori_loop(first_unused, n_blocks, tail_wait, 0)

    per_iter = DMA_UNROLL // TOP_K

    def issue(g, carry):
        for u in range(per_iter):
            row = g * per_iter + u
            for k in range(TOP_K):
                _row_copy(m_ref, row, xs_ref, dest_ref[k * n_tok + i * t + row], sem).start(priority=k % 2)
        return carry

    lax.fori_loop(0, t // per_iter, issue, 0)
    for k in range(TOP_K):
        pltpu.make_async_copy(m_ref, xs_ref.at[pl.ds(0, t * ROW_TILES)], sem).wait()


def _moe_dispatch(dest, cnt, region, m_rt, rows):
    n = m_rt.shape[0] // ROW_TILES
    t = DISPATCH_TILE
    return pl.pallas_call(
        functools.partial(_moe_dispatch_kernel, n_tok=n),
        grid_spec=pltpu.PrefetchScalarGridSpec(
            num_scalar_prefetch=3,
            grid=(n // t,),
            in_specs=[pl.BlockSpec((t * ROW_TILES, LANES), lambda i, d, c, rg: (i, 0))],
            out_specs=pl.BlockSpec(memory_space=pl.ANY),
            scratch_shapes=[pltpu.VMEM((EXPERT_ROWS * ROW_TILES, LANES), F32),
                            pltpu.SemaphoreType.DMA(()), pltpu.SemaphoreType.DMA(())],
        ),
        out_shape=jax.ShapeDtypeStruct((rows * ROW_TILES, LANES), F32),
        compiler_params=pltpu.CompilerParams(dimension_semantics=("arbitrary",)),
        name="moe_dispatch",
    )(dest, cnt, region, m_rt)


def _moe_experts_kernel(blk_e_ref, nact_ref, x_ref, wg_ref, wu_ref, wd_ref, y_ref, wgu_ref, wdn_ref):
    r = EXPERT_ROWS
    b = pl.program_id(0)
    live = b < nact_ref[0]

    @pl.when(live)
    def _():
        @pl.when(jnp.logical_or(b == 0, blk_e_ref[b] != blk_e_ref[jnp.maximum(b - 1, 0)]))
        def _():
            wgu_ref[:, :D_EXPERT] = wg_ref[0, 0].astype(BF16)
            wgu_ref[:, D_EXPERT:] = wu_ref[0, 0].astype(BF16)
            wdn_ref[...] = wd_ref[0, 0].astype(BF16)

        x = _load_row_tiled(x_ref, r).astype(BF16)
        gu = _dot(x, wgu_ref[...])
        hb = jax.nn.silu(gu[:, :D_EXPERT]) * gu[:, D_EXPERT:]
        _store_row_tiled(y_ref, _dot(hb.astype(BF16), wdn_ref[...]))

    @pl.when(jnp.logical_not(live))
    def _():
        y_ref[...] = jnp.zeros_like(y_ref)


def _moe_experts(blk_e, nact, xs_rt, layer, w_gate, w_up, w_down):
    rows = xs_rt.shape[0] // ROW_TILES
    r = EXPERT_ROWS
    wspec = lambda k, n: pl.BlockSpec((1, 1, k, n), lambda b, be, na: (layer, be[b], 0, 0))
    return pl.pallas_call(
        _moe_experts_kernel,
        grid_spec=pltpu.PrefetchScalarGridSpec(
            num_scalar_prefetch=2,
            grid=(rows // r,),
            in_specs=[pl.BlockSpec((r * ROW_TILES, LANES), lambda b, be, na: (jnp.minimum(b, na[0] - 1), 0)),
                      wspec(D_MODEL, D_EXPERT), wspec(D_MODEL, D_EXPERT), wspec(D_EXPERT, D_MODEL)],
            out_specs=pl.BlockSpec((r * ROW_TILES, LANES), lambda b, be, na: (b, 0)),
            scratch_shapes=[pltpu.VMEM((D_MODEL, 2 * D_EXPERT), BF16),
                            pltpu.VMEM((D_EXPERT, D_MODEL), BF16)],
        ),
        out_shape=jax.ShapeDtypeStruct((rows * ROW_TILES, LANES), F32),
        compiler_params=pltpu.CompilerParams(dimension_semantics=("arbitrary",), vmem_limit_bytes=VMEM_LIMIT),
        name="moe_experts",
    )(blk_e, nact, xs_rt, w_gate, w_up, w_down)


def _combine_ple_kernel(dest_ref, h1_ref, wt_ref, p_ref, y_ref, pleg_ref, gatew_ref, projw_ref, postg_ref,
                        out_ref, ybuf_ref, sem, *, n_tok):
    t = COMBINE_TILE
    i = pl.program_id(0)

    def gather(step, slot):
        def issue(g, carry):
            for u in range(DMA_UNROLL // TOP_K):
                r = g * (DMA_UNROLL // TOP_K) + u
                for k in range(TOP_K):
                    _row_copy(y_ref, dest_ref[k * n_tok + step * t + r], ybuf_ref, (slot * TOP_K + k) * t + r,
                              sem.at[slot]).start(priority=k % 2)
            return carry
        lax.fori_loop(0, t // (DMA_UNROLL // TOP_K), issue, 0)

    @pl.when(i == 0)
    def _():
        gather(0, 0)

    @pl.when(i + 1 < pl.num_programs(0))
    def _():
        gather(i + 1, (i + 1) % 2)

    e = _rms(_dot(p_ref[...].astype(BF16), projw_ref[...]), postg_ref[...])
    slot = i % 2
    _rows_wait(y_ref, ybuf_ref, slot * TOP_K * t, TOP_K * t, sem.at[slot])
    wt = wt_ref[...].T
    y0 = _load_row_tiled(ybuf_ref, t, (slot * TOP_K) * (t * ROW_TILES))
    y1 = _load_row_tiled(ybuf_ref, t, (slot * TOP_K + 1) * (t * ROW_TILES))
    h2 = h1_ref[...] + (wt[:, 0:1] * y0 + wt[:, 1:2] * y1)
    gate = jax.nn.sigmoid(_dot(_rms(h2, pleg_ref[...]).astype(BF16), gatew_ref[...]))
    out_ref[...] = h2 + gate * e


def _combine_ple(dest, h1, wt, p, y_rt, lw):
    n = h1.shape[0]
    t = COMBINE_TILE
    row = lambda w: pl.BlockSpec((t, w), lambda i, d: (i, 0))
    layer_rows = lw["layer"] * (n // t)
    weights = [lw[k] for k in ("ple_g", "gate_w", "proj_w", "post_g")]
    return pl.pallas_call(
        functools.partial(_combine_ple_kernel, n_tok=n),
        grid_spec=pltpu.PrefetchScalarGridSpec(
            num_scalar_prefetch=1,
            grid=(n // t,),
            in_specs=[row(D_MODEL), pl.BlockSpec((SUBLANES, t), lambda i, d: (0, i)), pl.BlockSpec((t, p.shape[1]), lambda i, d: (layer_rows + i, 0)),
                      pl.BlockSpec(memory_space=pl.ANY)]
            + [pl.BlockSpec(w.shape, lambda i, d, nd=w.ndim: (0,) * nd) for w in weights],
            out_specs=row(D_MODEL),
            scratch_shapes=[pltpu.VMEM((2 * TOP_K * t * ROW_TILES, LANES), F32), pltpu.SemaphoreType.DMA((2,))],
        ),
        out_shape=jax.ShapeDtypeStruct((n, D_MODEL), F32),
        compiler_params=pltpu.CompilerParams(dimension_semantics=("arbitrary",), vmem_limit_bytes=VMEM_LIMIT),
        name="combine_ple",
    )(dest, h1, wt, p, y_rt, *weights)


def _segment_matrix(seg_ids):
    seg_ids = jnp.asarray(seg_ids)
    same = (seg_ids[:, None] == seg_ids[None, :]).astype(F32)
    return (same / jnp.sum(same, axis=1, keepdims=True)).astype(BF16)


def _constants():
    lane = jnp.arange(256)
    qk_seg = (lane // HEAD_PAD) * 3 + jnp.where(lane % HEAD_PAD < MLA_NOPE, 0, jnp.where(lane % HEAD_PAD < MLA_QK, 1, 2))
    i = jnp.arange(RANK_CHUNK)
    e = jnp.arange(N_EXPERTS)
    return {
        "g64": _segment_matrix(lane // HEAD_DIM),
        "gqk": _segment_matrix(qk_seg),
        "upper": (i[:, None] < i[None, :]).astype(BF16),
        "ones": jnp.ones((RANK_CHUNK, LANES), BF16),
        "ltri": (e[None, :] < e[:, None]).astype(BF16),
    }


def _layer_weights(i, consts, p):
    row = lambda v: v.reshape(1, -1).astype(F32)
    w_in = p["w_in"][i]
    o_sgu, o_conv, o_q, o_kv = 2 * SGU_WIDTH, 2 * SGU_WIDTH + 2 * CONV_WIDTH, 0, 0
    o_q = o_conv + Q_RANK
    o_kv = o_q + KV_RANK
    w_ckv = jnp.concatenate([w_in[:, o_q:], jnp.zeros((D_MODEL, 256 - KV_RANK - MLA_ROPE), F32)], axis=1)

    causal = jnp.tril(jnp.ones((CHUNK, CHUNK), F32))
    wcat = jnp.transpose(p["sgu_w"][i] * causal, (1, 0, 2)).reshape(CHUNK, SGU_HEADS * CHUNK)
    sbias = jnp.broadcast_to(jnp.transpose(p["sgu_b"][i])[:, :, None],
                             (CHUNK, SGU_HEADS, HEAD_DIM)).reshape(CHUNK, SGU_WIDTH)

    w_uq = p["w_uq"][i].reshape(Q_RANK, MLA_HEADS, MLA_QK)
    w_uq = jnp.pad(w_uq, ((0, 0), (0, 0), (0, HEAD_PAD - MLA_QK))).reshape(Q_RANK, MLA_HEADS * HEAD_PAD)
    qn_g = p["q_norm_g"][i]
    q_gain = jnp.tile(jnp.concatenate([qn_g, jnp.zeros((HEAD_PAD - MLA_QK,), F32)]), MLA_HEADS)
    w_ukv = p["w_ukv"][i].reshape(KV_RANK, MLA_HEADS, MLA_NOPE + MLA_V)
    w_ukt = jnp.transpose(w_ukv[:, :, :MLA_NOPE], (1, 2, 0)).reshape(MLA_HEADS * MLA_NOPE, KV_RANK)
    w_v = w_ukv[:, :, MLA_NOPE:].reshape(KV_RANK, MLA_HEADS // 2, 2, MLA_V)
    zeros_v = jnp.zeros((KV_RANK, MLA_HEADS // 2, MLA_V), F32)
    w_uv = jnp.stack([w_v[:, :, 0], zeros_v, zeros_v, w_v[:, :, 1]], axis=2).reshape(KV_RANK, MLA_HEADS * HEAD_PAD)
    v_ones = jnp.tile(jnp.repeat(jnp.array([0.0, 1.0, 1.0, 0.0], F32), MLA_V), MLA_HEADS // 2)
    kn_g = p["k_norm_g"][i]

    bg = p["branch_norm_g"][i]
    w_o = p["w_o"][i]
    wr = jnp.concatenate([p["router_group_w"][i], jnp.zeros((D_MODEL, 8 - N_GROUPS), F32),
                          p["router_expert_w"][i], jnp.zeros((D_MODEL, LANES - 8 - N_EXPERTS), F32)], axis=1)
    wr_hi = wr.astype(BF16)
    br = jnp.concatenate([p["router_group_b"][i], jnp.full((8 - N_GROUPS,), NEG_BIG, F32),
                          p["router_expert_b"][i], jnp.zeros((LANES - 8 - N_EXPERTS,), F32)])
    return {
        "mix_g": row(p["mix_norm_g"][i]),
        "w_sgu": w_in[:, :o_sgu].astype(BF16),
        "w_conv": w_in[:, o_sgu:o_conv].astype(BF16),
        "w_cq": w_in[:, o_conv:o_q].astype(BF16),
        "w_ckv": w_ckv.astype(BF16),
        "sgu_ln_g": row(p["sgu_ln_g"][i]), "sgu_ln_b": row(p["sgu_ln_b"][i]),
        "sgu_wcat": wcat.astype(BF16), "sgu_bias": sbias, "g64": consts["g64"], "bg_a": row(bg[:SGU_WIDTH]),
        "conv_w": jnp.pad(p["conv_w"][i], ((0, 1), (0, 0))), "conv_b": row(p["conv_b"][i]),
        "conv_ln_g": row(p["conv_ln_g"][i]), "conv_ln_b": row(p["conv_ln_b"][i]),
        "pw_w": p["conv_pw_w"][i].astype(BF16), "pw_b": row(p["conv_pw_b"][i]),
        "bg_b": row(bg[SGU_WIDTH:SGU_WIDTH + CONV_WIDTH]),
        "qa_g": row(p["q_a_norm_g"][i]), "w_uq": w_uq.astype(BF16), "gqk": consts["gqk"],
        "q_gain": row(q_gain),
        "kva_g": row(p["kv_a_norm_g"][i]), "w_ukt": w_ukt.astype(BF16), "w_uv": w_uv.astype(BF16),
        "v_ones": row(v_ones),
        "kn_g": jnp.broadcast_to(kn_g[:MLA_NOPE, None], (MLA_NOPE, MIX_TILE)),
        "kpe_g": jnp.broadcast_to(kn_g[MLA_NOPE:, None], (MLA_ROPE, MIX_TILE)),
        "bg_c": row(bg[SGU_WIDTH + CONV_WIDTH:]),
        "w_o_a": w_o[:SGU_WIDTH].astype(BF16),
        "w_o_b": w_o[SGU_WIDTH:SGU_WIDTH + CONV_WIDTH].astype(BF16),
        "w_o_c": w_o[SGU_WIDTH + CONV_WIDTH:].astype(BF16),
        "ffn_g": row(p["ffn_norm_g"][i]),
        "wr_split": jnp.concatenate([wr_hi, (wr - wr_hi.astype(F32)).astype(BF16)], axis=1), "br": row(br),
        "layer": i, "w_gate": p["moe_w_gate"], "w_up": p["moe_w_up"], "w_down": p["moe_w_down"],
        "ple_g": row(p["ple_norm_g"][i]), "gate_w": p["ple_gate_w"][i].astype(BF16),
        "proj_w": p["ple_proj_w"][i].astype(BF16), "post_g": row(p["ple_post_norm_g"][i]),
    }


def _moe(h1, m, eidx, wts, pl_i, lw, consts):
    n = h1.shape[0]
    total = TOP_K * n
    rows = total + N_EXPERTS * EXPERT_ROWS
    dest2d, counts = _moe_rank(eidx[:TOP_K].reshape(1, total), consts)
    dest = dest2d.reshape(total)
    cnt = counts[:, 0].astype(jnp.int32)
    padded = (cnt + EXPERT_ROWS - 1) // EXPERT_ROWS * EXPERT_ROWS
    pend = jnp.cumsum(padded)
    nblk = rows // EXPERT_ROWS
    blk_start = jnp.arange(nblk, dtype=jnp.int32) * EXPERT_ROWS
    blk_e = jnp.minimum(jnp.sum((pend[None, :] <= blk_start[:, None]).astype(jnp.int32), axis=1), N_EXPERTS - 1)
    nact = (pend[-1:] // EXPERT_ROWS).astype(jnp.int32)
    region = jnp.concatenate([jnp.zeros((1,), jnp.int32), pend.astype(jnp.int32)])
    xs = _moe_dispatch(dest, cnt, region, m, rows)
    y = _moe_experts(blk_e, nact, xs, lw["layer"], lw["w_gate"], lw["w_up"], lw["w_down"])
    return _combine_ple(dest, h1, wts, pl_i, y, lw)


def kernel(x, p, positions, mix_norm_g, w_in, sgu_ln_g, sgu_ln_b, sgu_w, sgu_b, conv_w, conv_b, conv_ln_g, conv_ln_b, conv_pw_w, conv_pw_b, q_a_norm_g, w_uq, kv_a_norm_g, w_ukv, q_norm_g, k_norm_g, branch_norm_g, w_o, ffn_norm_g, router_group_w, router_group_b, router_expert_w, router_expert_b, moe_w_gate, moe_w_up, moe_w_down, ple_norm_g, ple_gate_w, ple_proj_w, ple_post_norm_g):
    params = dict(
        mix_norm_g=mix_norm_g, w_in=w_in, sgu_ln_g=sgu_ln_g, sgu_ln_b=sgu_ln_b, sgu_w=sgu_w, sgu_b=sgu_b,
        conv_w=conv_w, conv_b=conv_b, conv_ln_g=conv_ln_g, conv_ln_b=conv_ln_b, conv_pw_w=conv_pw_w,
        conv_pw_b=conv_pw_b, q_a_norm_g=q_a_norm_g, w_uq=w_uq, kv_a_norm_g=kv_a_norm_g, w_ukv=w_ukv,
        q_norm_g=q_norm_g, k_norm_g=k_norm_g, branch_norm_g=branch_norm_g, w_o=w_o, ffn_norm_g=ffn_norm_g,
        router_group_w=router_group_w, router_group_b=router_group_b, router_expert_w=router_expert_w,
        router_expert_b=router_expert_b, moe_w_gate=moe_w_gate, moe_w_up=moe_w_up, moe_w_down=moe_w_down,
        ple_norm_g=ple_norm_g, ple_gate_w=ple_gate_w, ple_proj_w=ple_proj_w, ple_post_norm_g=ple_post_norm_g)
    batch, seq, d = x.shape
    n = batch * seq
    depth = w_in.shape[0]
    consts = _constants()
    cos, sin = _rope_tables(positions)
    h = x.reshape(n, d)
    p_rows = p.reshape(depth * n, p.shape[-1])
    for i in range(depth):
        lw = _layer_weights(i, consts, params)
        ya, yb, q, kt, v = _mixer_pre(h, cos, sin, lw, batch, seq)
        yc = _attention(q, kt, v, batch, seq)
        h1, m, eidx, wts = _outproj_router(h, ya, yb, yc, lw)
        h = _moe(h1, m, eidx, wts, p_rows, lw, consts)
    return h.reshape(batch, seq, d)
```

```python
import functools
import math

import jax
import jax.numpy as jnp
from jax import lax
from jax.experimental import pallas as pl
from jax.experimental.pallas import tpu as pltpu

F32 = jnp.float32
BF16 = jnp.bfloat16

D_MODEL = 1024
HEAD_DIM = 64
SGU_HEADS = 4
SGU_WIDTH = 256
CHUNK = 128
CONV_WIDTH = 256
CONV_KERNEL = 31
MLA_HEADS = 8
MLA_NOPE = 64
MLA_ROPE = 32
MLA_QK = MLA_NOPE + MLA_ROPE
MLA_V = 64
Q_RANK = 256
KV_RANK = 128
ROPE_THETA = 10000.0
N_GROUPS = 4
EXPERTS_PER_GROUP = 8
N_EXPERTS = 32
D_EXPERT = 256
TOP_K = 2
EPS = 1e-6

LANES = 128
SUBLANES = 8
HEAD_PAD = 128
MIX_TILE = 1024
ROUTER_TILE = 1024
CONV_HALO = 32
CONV_SHIFTED_ROWS = MIX_TILE + CONV_HALO - SUBLANES
MIXER_ORDER = ("q", "sgu", "kv", "conv")
ATT_TQ = 256
ATT_PAIRS = 2
RANK_CHUNK = 512
RANK_STEP = 4096
EXPERT_ROWS = 512
DISPATCH_TILE = 2048
COMBINE_TILE = 1024
ROW_TILES = D_MODEL // LANES
DMA_UNROLL = 8
VMEM_LIMIT = 48 * 1024 * 1024
NEG_BIG = -1e30


def _dot(a, b):
    return jnp.dot(a, b, preferred_element_type=F32)


def _rms(x, g):
    ms = jnp.mean(x * x, axis=-1, keepdims=True)
    return x * lax.rsqrt(ms + EPS) * g


def _seg_mean(x, g_ref, split=True):
    g = g_ref[...]
    outs = []
    for c in range(x.shape[1] // 256):
        xb = x[:, c * 256:(c + 1) * 256]
        hi = xb.astype(BF16)
        acc = _dot(hi, g)
        if split:
            acc = acc + _dot((xb - hi.astype(F32)).astype(BF16), g)
        outs.append(acc)
    return outs[0] if len(outs) == 1 else jnp.concatenate(outs, axis=-1)


def _group_layernorm(x, g_ref, gain, bias):
    d = x - _seg_mean(x, g_ref)
    var = _seg_mean(d * d, g_ref)
    return d * lax.rsqrt(var + EPS) * gain + bias


def _rope_table_kernel(pos_ref, invf_ref, cos_ref, sin_ref):
    ang = pos_ref[...].astype(F32) * invf_ref[...]
    cos_ref[...] = jnp.cos(ang)
    sin_ref[...] = jnp.sin(ang)


def _rope_tables(positions):
    b, s = positions.shape
    n = b * s
    half = MLA_ROPE // 2
    inv_freq = ROPE_THETA ** (-jnp.arange(0, MLA_ROPE, 2, dtype=F32) / MLA_ROPE)
    pos_rep = jnp.broadcast_to(positions.reshape(n, 1), (n, half)).reshape(n * half // LANES, LANES)
    invf_rep = jnp.tile(inv_freq, LANES // half).reshape(1, LANES)
    shape = jax.ShapeDtypeStruct(pos_rep.shape, F32)
    cos, sin = pl.pallas_call(_rope_table_kernel, out_shape=(shape, shape), name="rope_tables")(pos_rep, invf_rep)
    return cos.reshape(n, half), sin.reshape(n, half)


def _mixer_pre_kernel(
        h_ref, cos_ref, sin_ref,
        mixg_ref, wsgu_ref, wconv_ref, wcq_ref, wckv_ref,
        slng_ref, slnb_ref, wcat_ref, sbias_ref, g64_ref, bga_ref,
        cw_ref, cb_ref, clng_ref, clnb_ref, pww_ref, pwb_ref, bgb_ref,
        qag_ref, wuq_ref, gqk_ref, qgain_ref,
        kvag_ref, wukt_ref, wuv_ref, vones_ref, kng_ref, kpeg_ref,
        ya_ref, yb_ref, q_ref, kt_ref, v_ref,
        ybuf_ref, ysh_ref, *, tiles_per_seq):
    t = MIX_TILE
    a = _rms(h_ref[...], mixg_ref[...]).astype(BF16)
    zs = _dot(a, wsgu_ref[...])
    zc = _dot(a, wconv_ref[...])
    zq = _dot(a, wcq_ref[...])
    zkv = _dot(a, wckv_ref[...])

    half = MLA_ROPE // 2
    cos, sin = cos_ref[...], sin_ref[...]
    zn = jnp.zeros((t, MLA_NOPE), F32)
    zh = jnp.zeros((t, half), F32)
    zp = jnp.zeros((t, HEAD_PAD - MLA_QK), F32)
    rc = jnp.concatenate([jnp.ones((t, MLA_NOPE), F32), cos, cos, zp], axis=1)
    rs1 = jnp.concatenate([zn, -sin, zh, zp], axis=1)
    rs2 = jnp.concatenate([zn, zh, sin, zp], axis=1)

    def sgu_branch():
        zg = jax.nn.gelu(zs)
        u = zg[:, :SGU_WIDTH]
        vn = _group_layernorm(zg[:, SGU_WIDTH:], g64_ref, slng_ref[...], slnb_ref[...])
        lane_head = lax.broadcasted_iota(jnp.int32, (CHUNK, SGU_WIDTH), 1) // HEAD_DIM
        wcat = wcat_ref[...]
        sbias = sbias_ref[...]
        parts = []
        for c in range(t // CHUNK):
            vc = vn[c * CHUNK:(c + 1) * CHUNK]
            stacked = jnp.concatenate(
                [jnp.where(lane_head == hh, vc, 0.0) for hh in range(SGU_HEADS)], axis=0).astype(BF16)
            s = _dot(wcat, stacked) + sbias
            parts.append(u[c * CHUNK:(c + 1) * CHUNK] * s)
        ya_ref[...] = _rms(jnp.concatenate(parts, axis=0), bga_ref[...]).astype(BF16)

    def conv_branch():
        yg = zc[:, :CONV_WIDTH] * jax.nn.sigmoid(zc[:, CONV_WIDTH:])
        first = (pl.program_id(0) % tiles_per_seq) == 0

        @pl.when(first)
        def _():
            ybuf_ref[0:CONV_HALO, :] = jnp.zeros((CONV_HALO, CONV_WIDTH), F32)

        @pl.when(jnp.logical_not(first))
        def _():
            ybuf_ref[0:CONV_HALO, :] = ybuf_ref[t:t + CONV_HALO, :]

        ybuf_ref[CONV_HALO:CONV_HALO + t, :] = yg
        for sh in range(1, SUBLANES):
            ysh_ref[sh - 1] = ybuf_ref[pl.ds(sh, CONV_SHIFTED_ROWS), :]
        rows = 64
        first_tap_row = CONV_HALO - (CONV_KERNEL - 1)
        conv_parts = []
        for r in range(t // rows):
            acc = jnp.broadcast_to(cb_ref[...], (rows, CONV_WIDTH))
            for tap in range(CONV_KERNEL):
                sh = (first_tap_row + tap) % SUBLANES
                start = r * rows + first_tap_row + tap - sh
                src = ybuf_ref[pl.ds(start, rows), :] if sh == 0 else ysh_ref[sh - 1, pl.ds(start, rows), :]
                acc = acc + cw_ref[tap:tap + 1, :] * src
            conv_parts.append(acc)
        cv = jnp.concatenate(conv_parts, axis=0)
        cn = _group_layernorm(cv, g64_ref, clng_ref[...], clnb_ref[...])
        yb = _dot(jax.nn.silu(cn).astype(BF16), pww_ref[...]) + pwb_ref[...]
        yb_ref[...] = _rms(yb, bgb_ref[...]).astype(BF16)

    def query_branch():
        cqn = _rms(zq, qag_ref[...]).astype(BF16)
        qf = _dot(cqn, wuq_ref[...])
        qn = qf * lax.rsqrt(_seg_mean(qf * qf, gqk_ref, split=False) + EPS) * qgain_ref[...]
        for hh in range(MLA_HEADS):
            blk = qn[:, hh * HEAD_PAD:(hh + 1) * HEAD_PAD]
            rot = blk * rc + pltpu.roll(blk, HEAD_PAD - half, 1) * rs1 + pltpu.roll(blk, half, 1) * rs2
            q_ref[:, hh * HEAD_PAD:(hh + 1) * HEAD_PAD] = rot.astype(BF16)

    def key_value_branch():
        ckvn = _rms(zkv[:, :KV_RANK], kvag_ref[...]).astype(BF16)
        v_ref[...] = (_dot(ckvn, wuv_ref[...]) + vones_ref[...]).astype(BF16)
        knt = lax.dot_general(wukt_ref[...], ckvn, (((1,), (1,)), ((), ())), preferred_element_type=F32)
        x = zkv[:, KV_RANK:].T[0:MLA_ROPE]
        xn = x * lax.rsqrt(jnp.mean(x * x, axis=0, keepdims=True) + EPS) * kpeg_ref[...]
        x1, x2 = xn[:MLA_ROPE // 2], xn[MLA_ROPE // 2:]
        cos_t = rc.T[MLA_NOPE:MLA_NOPE + half]
        sin_t = rs2.T[MLA_NOPE + half:MLA_QK]
        kpe = jnp.concatenate([x1 * cos_t - x2 * sin_t, x2 * cos_t + x1 * sin_t], axis=0)
        pad = jnp.zeros((HEAD_PAD - MLA_QK, t), F32)
        kng = kng_ref[...]
        for hh in range(MLA_HEADS):
            blk = knt[hh * MLA_NOPE:(hh + 1) * MLA_NOPE]
            kn = blk * lax.rsqrt(jnp.mean(blk * blk, axis=0, keepdims=True) + EPS) * kng
            kt_ref[0, hh * HEAD_PAD:(hh + 1) * HEAD_PAD, :] = jnp.concatenate([kn, kpe, pad], axis=0).astype(BF16)

    branches = {"sgu": sgu_branch, "conv": conv_branch, "q": query_branch, "kv": key_value_branch}
    for name in MIXER_ORDER:
        branches[name]()


def _full(shape):
    nd = len(shape)
    return pl.BlockSpec(shape, lambda *_: (0,) * nd)


def _mixer_pre(h, cos, sin, lw, batch, seq):
    n = h.shape[0]
    t = MIX_TILE
    tps = seq // t
    row = lambda w: pl.BlockSpec((t, w), lambda i: (i, 0))
    weights = [lw[k] for k in (
        "mix_g", "w_sgu", "w_conv", "w_cq", "w_ckv",
        "sgu_ln_g", "sgu_ln_b", "sgu_wcat", "sgu_bias", "g64", "bg_a",
        "conv_w", "conv_b", "conv_ln_g", "conv_ln_b", "pw_w", "pw_b", "bg_b",
        "qa_g", "w_uq", "gqk", "q_gain",
        "kva_g", "w_ukt", "w_uv", "v_ones", "kn_g", "kpe_g")]
    in_specs = [row(D_MODEL), row(MLA_ROPE // 2), row(MLA_ROPE // 2)] + [_full(w.shape) for w in weights]
    out_shape = (
        jax.ShapeDtypeStruct((n, SGU_WIDTH), BF16),
        jax.ShapeDtypeStruct((n, CONV_WIDTH), BF16),
        jax.ShapeDtypeStruct((n, MLA_HEADS * HEAD_PAD), BF16),
        jax.ShapeDtypeStruct((batch, MLA_HEADS * HEAD_PAD, seq), BF16),
        jax.ShapeDtypeStruct((n, MLA_HEADS * HEAD_PAD), BF16),
    )
    out_specs = (
        row(SGU_WIDTH), row(CONV_WIDTH), row(MLA_HEADS * HEAD_PAD),
        pl.BlockSpec((1, MLA_HEADS * HEAD_PAD, t), lambda i: (i // tps, 0, i % tps)),
        row(MLA_HEADS * HEAD_PAD),
    )
    return pl.pallas_call(
        functools.partial(_mixer_pre_kernel, tiles_per_seq=tps),
        grid=(n // t,),
        in_specs=in_specs,
        out_specs=out_specs,
        out_shape=out_shape,
        scratch_shapes=[pltpu.VMEM((t + CONV_HALO, CONV_WIDTH), F32),
                        pltpu.VMEM((SUBLANES - 1, CONV_SHIFTED_ROWS, CONV_WIDTH), F32)],
        compiler_params=pltpu.CompilerParams(dimension_semantics=("arbitrary",), vmem_limit_bytes=VMEM_LIMIT),
        name="mixer_pre",
    )(h, cos, sin, *weights)


def _attention_kernel(q_ref, kt_ref, v_ref, o_ref, *, seq):
    tq = ATT_TQ
    exp2_scale = MLA_QK ** -0.5 * math.log2(math.e)
    row = lax.broadcasted_iota(jnp.int32, (tq, tq), 0)
    col = lax.broadcasted_iota(jnp.int32, (tq, tq), 1)
    lane = lax.broadcasted_iota(jnp.int32, (tq, HEAD_PAD), 1)
    def scores(pair, qi, hh):
        nk = (qi + 1) * tq
        head = (2 * pair + hh) * HEAD_PAD
        s = _dot(q_ref[qi * tq:(qi + 1) * tq, head:head + HEAD_PAD], kt_ref[0, head:head + HEAD_PAD, 0:nk])
        diag = jnp.where(col <= row, s[:, nk - tq:], NEG_BIG)
        return diag if qi == 0 else jnp.concatenate([s[:, :nk - tq], diag], axis=1)

    n_blocks = seq // tq
    items = [(pair, qi) for pair in range(ATT_PAIRS) for qi in range(n_blocks)]
    ahead = [scores(*items[0], hh) for hh in range(2)]
    for idx, (pair, qi) in enumerate(items):
        s_pair, ahead = ahead, []
        if idx + 1 < len(items):
            ahead = [scores(*items[idx + 1], hh) for hh in range(2)]
        p_pair = [jnp.exp2(((s - jnp.max(s, axis=-1, keepdims=True)) * exp2_scale).astype(BF16)) for s in s_pair]
        vcols = slice(2 * pair * HEAD_PAD, 2 * (pair + 1) * HEAD_PAD)
        acc = _dot(jnp.concatenate(p_pair, axis=0), v_ref[0:(qi + 1) * tq, vcols])
        a0, a1 = acc[:tq, :HEAD_PAD], acc[tq:, HEAD_PAD:]
        o_ref[qi * tq:(qi + 1) * tq, 2 * pair * MLA_V:2 * (pair + 1) * MLA_V] = jnp.where(
            lane < MLA_V, a0 / pltpu.roll(a0, MLA_V, 1), a1 / pltpu.roll(a1, MLA_V, 1))


def _attention(q, kt, v, batch, seq):
    n = q.shape[0]
    width = 2 * ATT_PAIRS
    return pl.pallas_call(
        functools.partial(_attention_kernel, seq=seq),
        grid=(batch, MLA_HEADS // width),
        in_specs=[
            pl.BlockSpec((seq, width * HEAD_PAD), lambda b, p: (b, p)),
            pl.BlockSpec((1, width * HEAD_PAD, seq), lambda b, p: (b, p, 0)),
            pl.BlockSpec((seq, width * HEAD_PAD), lambda b, p: (b, p)),
        ],
        out_specs=pl.BlockSpec((seq, width * MLA_V), lambda b, p: (b, p)),
        out_shape=jax.ShapeDtypeStruct((n, MLA_HEADS * MLA_V), F32),
        compiler_params=pltpu.CompilerParams(
            dimension_semantics=("arbitrary", "arbitrary"), vmem_limit_bytes=VMEM_LIMIT),
        name="attention",
    )(q, kt, v)


def _outproj_router_kernel(h_ref, ya_ref, yb_ref, yc_ref, bgc_ref, woa_ref, wob_ref, woc_ref,
                           ffng_ref, wr_ref, br_ref,
                           h1_ref, m_ref, eidx_ref, wts_ref):
    t = ROUTER_TILE
    ycn = _rms(yc_ref[...], bgc_ref[...])
    proj = (_dot(ya_ref[...], woa_ref[...]) + _dot(yb_ref[...], wob_ref[...])
            + _dot(ycn.astype(BF16), woc_ref[...]))
    h1 = h_ref[...] + proj
    h1_ref[...] = h1
    m = _rms(h1, ffng_ref[...])
    _store_row_tiled(m_ref, m)

    mh = m.astype(BF16)
    ml = (m - mh.astype(F32)).astype(BF16)
    wr = wr_ref[...]
    both = _dot(mh, wr)
    logits = both[:, :LANES] + both[:, LANES:] + _dot(ml, wr[:, :LANES]) + br_ref[...]
    lt = logits.T
    rowi = lax.broadcasted_iota(jnp.int32, (EXPERTS_PER_GROUP, t), 0)
    g8 = lt[0:8]
    gmax = jnp.max(g8, axis=0, keepdims=True)
    gsum = jnp.sum(jnp.exp(g8 - gmax), axis=0, keepdims=True)
    gidx = jnp.min(jnp.where(g8 == gmax, rowi, 8), axis=0, keepdims=True)
    g_w = 1.0 / gsum
    esel = jnp.zeros((EXPERTS_PER_GROUP, t), F32)
    for g in range(N_GROUPS):
        esel = jnp.where(gidx == g, lt[8 + g * EXPERTS_PER_GROUP:8 + (g + 1) * EXPERTS_PER_GROUP], esel)
    ep = jnp.exp(esel - jnp.max(esel, axis=0, keepdims=True))
    eprob = ep / jnp.sum(ep, axis=0, keepdims=True)
    v1 = jnp.max(eprob, axis=0, keepdims=True)
    i1 = jnp.min(jnp.where(eprob == v1, rowi, 8), axis=0, keepdims=True)
    rest = jnp.where(rowi == i1, -1.0, eprob)
    v2 = jnp.max(rest, axis=0, keepdims=True)
    i2 = jnp.min(jnp.where(rest == v2, rowi, 8), axis=0, keepdims=True)
    den = v1 + v2
    e1 = gidx * EXPERTS_PER_GROUP + i1
    e2 = gidx * EXPERTS_PER_GROUP + i2
    eidx_ref[...] = jnp.where(rowi == 0, e1, jnp.where(rowi == 1, e2, 0))
    wts_ref[...] = jnp.where(rowi == 0, g_w * (v1 / den), jnp.where(rowi == 1, g_w * (v2 / den), 0.0))


def _outproj_router(h, ya, yb, yc, lw):
    n = h.shape[0]
    t = ROUTER_TILE
    row = lambda w: pl.BlockSpec((t, w), lambda i: (i, 0))
    weights = [lw[k] for k in ("bg_c", "w_o_a", "w_o_b", "w_o_c", "ffn_g", "wr_split", "br")]
    colspec = pl.BlockSpec((8, t), lambda i: (0, i))
    return pl.pallas_call(
        _outproj_router_kernel,
        grid=(n // t,),
        in_specs=[row(D_MODEL), row(SGU_WIDTH), row(CONV_WIDTH), row(MLA_HEADS * MLA_V)]
        + [_full(w.shape) for w in weights],
        out_specs=(row(D_MODEL), pl.BlockSpec((t * ROW_TILES, LANES), lambda i: (i, 0)), colspec, colspec),
        out_shape=(
            jax.ShapeDtypeStruct((n, D_MODEL), F32),
            jax.ShapeDtypeStruct((n * ROW_TILES, LANES), F32),
            jax.ShapeDtypeStruct((8, n), jnp.int32),
            jax.ShapeDtypeStruct((8, n), F32),
        ),
        compiler_params=pltpu.CompilerParams(dimension_semantics=("arbitrary",), vmem_limit_bytes=VMEM_LIMIT),
        name="outproj_router",
    )(h, ya, yb, yc, *weights)


def _moe_rank_kernel(e_ref, upper_ref, ones_ref, ltri_ref, dest_ref, cnt_ref, run_ref, base_ref):
    phase = pl.program_id(0)
    step = pl.program_id(1)
    c = RANK_CHUNK
    expert = lax.broadcasted_iota(jnp.int32, (N_EXPERTS, c), 0)

    @pl.when(jnp.logical_and(phase == 0, step == 0))
    def _():
        run_ref[...] = jnp.zeros_like(run_ref)
        base_ref[...] = jnp.zeros_like(base_ref)

    @pl.when(jnp.logical_and(phase == 1, step == 0))
    def _():
        blocks = jnp.floor((run_ref[...] + (EXPERT_ROWS - 1)) * (1.0 / EXPERT_ROWS))
        base_ref[...] = _dot(ltri_ref[...], blocks.astype(BF16)) * EXPERT_ROWS
        run_ref[...] = jnp.zeros_like(run_ref)

    def chunk_onehot(sub):
        return jnp.where(expert == e_ref[:, sub * c:(sub + 1) * c], 1.0, 0.0)

    @pl.when(phase == 0)
    def _():
        dest_ref[...] = jnp.zeros_like(dest_ref)
        for sub in range(RANK_STEP // c):
            run_ref[...] = run_ref[...] + _dot(chunk_onehot(sub).astype(BF16), ones_ref[...])

    @pl.when(phase == 1)
    def _():
        for sub in range(RANK_STEP // c):
            onehot = chunk_onehot(sub)
            oh16 = onehot.astype(BF16)
            before = _dot(oh16, upper_ref[...])
            pos = before + run_ref[:, 0:1] + base_ref[:, 0:1]
            dest_ref[:, sub * c:(sub + 1) * c] = jnp.sum(onehot * pos, axis=0, keepdims=True).astype(jnp.int32)
            run_ref[...] = run_ref[...] + _dot(oh16, ones_ref[...])

    cnt_ref[...] = run_ref[...]


def _moe_rank(e_flat, consts):
    total = e_flat.shape[1]
    c = RANK_CHUNK
    st = RANK_STEP
    return pl.pallas_call(
        _moe_rank_kernel,
        grid=(2, total // st),
        in_specs=[pl.BlockSpec((1, st), lambda p, s: (0, s)),
                  _full((c, c)), _full((c, LANES)), _full((N_EXPERTS, N_EXPERTS))],
        out_specs=(pl.BlockSpec((1, st), lambda p, s: (0, s * p)),
                   pl.BlockSpec((N_EXPERTS, LANES), lambda p, s: (0, 0))),
        out_shape=(jax.ShapeDtypeStruct((1, total), jnp.int32),
                   jax.ShapeDtypeStruct((N_EXPERTS, LANES), F32)),
        scratch_shapes=[pltpu.VMEM((N_EXPERTS, LANES), F32), pltpu.VMEM((N_EXPERTS, LANES), F32)],
        compiler_params=pltpu.CompilerParams(dimension_semantics=("arbitrary", "arbitrary")),
        name="moe_rank",
    )(e_flat, consts["upper"], consts["ones"], consts["ltri"])


def _store_row_tiled(ref, x, offset=0):
    rows = x.shape[0]
    for s in range(ROW_TILES):
        ref[pl.ds(offset + s, rows, stride=ROW_TILES), :] = x[:, s * LANES:(s + 1) * LANES]


def _load_row_tiled(ref, rows, offset=0):
    return jnp.concatenate(
        [ref[pl.ds(offset + s, rows, stride=ROW_TILES), :] for s in range(ROW_TILES)], axis=1)


def _row_copy(src, src_row, dst, dst_row, sem):
    return pltpu.make_async_copy(
        src.at[pl.ds(pl.multiple_of(src_row * ROW_TILES, ROW_TILES), ROW_TILES)],
        dst.at[pl.ds(pl.multiple_of(dst_row * ROW_TILES, ROW_TILES), ROW_TILES)], sem)


def _rows_wait(src, dst, dst_row, rows, sem):
    pltpu.make_async_copy(
        src.at[pl.ds(0, rows * ROW_TILES)],
        dst.at[pl.ds(pl.multiple_of(dst_row * ROW_TILES, ROW_TILES), rows * ROW_TILES)], sem).wait()


def _moe_dispatch_kernel(dest_ref, cnt_ref, region_ref, m_ref, xs_ref, zbuf_ref, sem, zsem, *, n_tok):
    t = DISPATCH_TILE
    r = EXPERT_ROWS
    i = pl.program_id(0)

    @pl.when(i == 0)
    def _():
        zbuf_ref[...] = jnp.zeros_like(zbuf_ref)

        def zero_copy(e):
            last = pl.multiple_of((region_ref[e + 1] - r) * ROW_TILES, r * ROW_TILES)
            return pltpu.make_async_copy(zbuf_ref, xs_ref.at[pl.ds(last, r * ROW_TILES)], zsem)

        def start(e, carry):
            @pl.when(cnt_ref[e] > 0)
            def _():
                zero_copy(e).start()
            return carry

        def wait(e, carry):
            @pl.when(cnt_ref[e] > 0)
            def _():
                zero_copy(e).wait()
            return carry

        def tail_copy(blk):
            return pltpu.make_async_copy(
                zbuf_ref, xs_ref.at[pl.ds(pl.multiple_of(blk * (r * ROW_TILES), r * ROW_TILES), r * ROW_TILES)], zsem)

        def tail_start(blk, carry):
            tail_copy(blk).start()
            return carry

        def tail_wait(blk, carry):
            tail_copy(blk).wait()
            return carry

        first_unused = region_ref[N_EXPERTS] // r
        n_blocks = xs_ref.shape[0] // (r * ROW_TILES)
        lax.fori_loop(0, N_EXPERTS, start, 0)
        lax.fori_loop(first_unused, n_blocks, tail_start, 0)
        lax.fori_loop(0, N_EXPERTS, wait, 0)
        lax.fori_loop(first_unused, n_blocks, tail_wait, 0)

    per_iter = DMA_UNROLL // TOP_K

    def issue(g, carry):
        for u in range(per_iter):
            row = g * per_iter + u
            for k in range(TOP_K):
                _row_copy(m_ref, row, xs_ref, dest_ref[k * n_tok + i * t + row], sem).start(priority=k % 2)
        return carry

    lax.fori_loop(0, t // per_iter, issue, 0)
    for k in range(TOP_K):
        pltpu.make_async_copy(m_ref, xs_ref.at[pl.ds(0, t * ROW_TILES)], sem).wait()


def _moe_dispatch(dest, cnt, region, m_rt, rows):
    n = m_rt.shape[0] // ROW_TILES
    t = DISPATCH_TILE
    return pl.pallas_call(
        functools.partial(_moe_dispatch_kernel, n_tok=n),
        grid_spec=pltpu.PrefetchScalarGridSpec(
            num_scalar_prefetch=3,
            grid=(n // t,),
            in_specs=[pl.BlockSpec((t * ROW_TILES, LANES), lambda i, d, c, rg: (i, 0))],
            out_specs=pl.BlockSpec(memory_space=pl.ANY),
            scratch_shapes=[pltpu.VMEM((EXPERT_ROWS * ROW_TILES, LANES), F32),
                            pltpu.SemaphoreType.DMA(()), pltpu.SemaphoreType.DMA(())],
        ),
        out_shape=jax.ShapeDtypeStruct((rows * ROW_TILES, LANES), F32),
        compiler_params=pltpu.CompilerParams(dimension_semantics=("arbitrary",)),
        name="moe_dispatch",
    )(dest, cnt, region, m_rt)


def _moe_experts_kernel(blk_e_ref, nact_ref, x_ref, wg_ref, wu_ref, wd_ref, y_ref, wgu_ref, wdn_ref):
    r = EXPERT_ROWS
    b = pl.program_id(0)
    live = b < nact_ref[0]

    @pl.when(live)
    def _():
        @pl.when(jnp.logical_or(b == 0, blk_e_ref[b] != blk_e_ref[jnp.maximum(b - 1, 0)]))
        def _():
            wgu_ref[:, :D_EXPERT] = wg_ref[0, 0].astype(BF16)
            wgu_ref[:, D_EXPERT:] = wu_ref[0, 0].astype(BF16)
            wdn_ref[...] = wd_ref[0, 0].astype(BF16)

        x = _load_row_tiled(x_ref, r).astype(BF16)
        gu = _dot(x, wgu_ref[...])
        hb = jax.nn.silu(gu[:, :D_EXPERT]) * gu[:, D_EXPERT:]
        _store_row_tiled(y_ref, _dot(hb.astype(BF16), wdn_ref[...]))

    @pl.when(jnp.logical_not(live))
    def _():
        y_ref[...] = jnp.zeros_like(y_ref)


def _moe_experts(blk_e, nact, xs_rt, layer, w_gate, w_up, w_down):
    rows = xs_rt.shape[0] // ROW_TILES
    r = EXPERT_ROWS
    wspec = lambda k, n: pl.BlockSpec((1, 1, k, n), lambda b, be, na: (layer, be[b], 0, 0))
    return pl.pallas_call(
        _moe_experts_kernel,
        grid_spec=pltpu.PrefetchScalarGridSpec(
            num_scalar_prefetch=2,
            grid=(rows // r,),
            in_specs=[pl.BlockSpec((r * ROW_TILES, LANES), lambda b, be, na: (jnp.minimum(b, na[0] - 1), 0)),
                      wspec(D_MODEL, D_EXPERT), wspec(D_MODEL, D_EXPERT), wspec(D_EXPERT, D_MODEL)],
            out_specs=pl.BlockSpec((r * ROW_TILES, LANES), lambda b, be, na: (b, 0)),
            scratch_shapes=[pltpu.VMEM((D_MODEL, 2 * D_EXPERT), BF16),
                            pltpu.VMEM((D_EXPERT, D_MODEL), BF16)],
        ),
        out_shape=jax.ShapeDtypeStruct((rows * ROW_TILES, LANES), F32),
        compiler_params=pltpu.CompilerParams(dimension_semantics=("arbitrary",), vmem_limit_bytes=VMEM_LIMIT),
        name="moe_experts",
    )(blk_e, nact, xs_rt, w_gate, w_up, w_down)


def _combine_ple_kernel(dest_ref, h1_ref, wt_ref, p_ref, y_ref, pleg_ref, gatew_ref, projw_ref, postg_ref,
                        out_ref, ybuf_ref, sem, *, n_tok):
    t = COMBINE_TILE
    i = pl.program_id(0)

    def gather(step, slot):
        def issue(g, carry):
            for u in range(DMA_UNROLL // TOP_K):
                r = g * (DMA_UNROLL // TOP_K) + u
                for k in range(TOP_K):
                    _row_copy(y_ref, dest_ref[k * n_tok + step * t + r], ybuf_ref, (slot * TOP_K + k) * t + r,
                              sem.at[slot]).start(priority=k % 2)
            return carry
        lax.fori_loop(0, t // (DMA_UNROLL // TOP_K), issue, 0)

    @pl.when(i == 0)
    def _():
        gather(0, 0)

    @pl.when(i + 1 < pl.num_programs(0))
    def _():
        gather(i + 1, (i + 1) % 2)

    e = _rms(_dot(p_ref[...].astype(BF16), projw_ref[...]), postg_ref[...])
    slot = i % 2
    _rows_wait(y_ref, ybuf_ref, slot * TOP_K * t, TOP_K * t, sem.at[slot])
    wt = wt_ref[...].T
    y0 = _load_row_tiled(ybuf_ref, t, (slot * TOP_K) * (t * ROW_TILES))
    y1 = _load_row_tiled(ybuf_ref, t, (slot * TOP_K + 1) * (t * ROW_TILES))
    h2 = h1_ref[...] + (wt[:, 0:1] * y0 + wt[:, 1:2] * y1)
    gate = jax.nn.sigmoid(_dot(_rms(h2, pleg_ref[...]).astype(BF16), gatew_ref[...]))
    out_ref[...] = h2 + gate * e


def _combine_ple(dest, h1, wt, p, y_rt, lw):
    n = h1.shape[0]
    t = COMBINE_TILE
    row = lambda w: pl.BlockSpec((t, w), lambda i, d: (i, 0))
    layer_rows = lw["layer"] * (n // t)
    weights = [lw[k] for k in ("ple_g", "gate_w", "proj_w", "post_g")]
    return pl.pallas_call(
        functools.partial(_combine_ple_kernel, n_tok=n),
        grid_spec=pltpu.PrefetchScalarGridSpec(
            num_scalar_prefetch=1,
            grid=(n // t,),
            in_specs=[row(D_MODEL), pl.BlockSpec((SUBLANES, t), lambda i, d: (0, i)), pl.BlockSpec((t, p.shape[1]), lambda i, d: (layer_rows + i, 0)),
                      pl.BlockSpec(memory_space=pl.ANY)]
            + [pl.BlockSpec(w.shape, lambda i, d, nd=w.ndim: (0,) * nd) for w in weights],
            out_specs=row(D_MODEL),
            scratch_shapes=[pltpu.VMEM((2 * TOP_K * t * ROW_TILES, LANES), F32), pltpu.SemaphoreType.DMA((2,))],
        ),
        out_shape=jax.ShapeDtypeStruct((n, D_MODEL), F32),
        compiler_params=pltpu.CompilerParams(dimension_semantics=("arbitrary",), vmem_limit_bytes=VMEM_LIMIT),
        name="combine_ple",
    )(dest, h1, wt, p, y_rt, *weights)


def _segment_matrix(seg_ids):
    seg_ids = jnp.asarray(seg_ids)
    same = (seg_ids[:, None] == seg_ids[None, :]).astype(F32)
    return (same / jnp.sum(same, axis=1, keepdims=True)).astype(BF16)


def _constants():
    lane = jnp.arange(256)
    qk_seg = (lane // HEAD_PAD) * 3 + jnp.where(lane % HEAD_PAD < MLA_NOPE, 0, jnp.where(lane % HEAD_PAD < MLA_QK, 1, 2))
    i = jnp.arange(RANK_CHUNK)
    e = jnp.arange(N_EXPERTS)
    return {
        "g64": _segment_matrix(lane // HEAD_DIM),
        "gqk": _segment_matrix(qk_seg),
        "upper": (i[:, None] < i[None, :]).astype(BF16),
        "ones": jnp.ones((RANK_CHUNK, LANES), BF16),
        "ltri": (e[None, :] < e[:, None]).astype(BF16),
    }


def _layer_weights(i, consts, p):
    row = lambda v: v.reshape(1, -1).astype(F32)
    w_in = p["w_in"][i]
    o_sgu, o_conv, o_q, o_kv = 2 * SGU_WIDTH, 2 * SGU_WIDTH + 2 * CONV_WIDTH, 0, 0
    o_q = o_conv + Q_RANK
    o_kv = o_q + KV_RANK
    w_ckv = jnp.concatenate([w_in[:, o_q:], jnp.zeros((D_MODEL, 256 - KV_RANK - MLA_ROPE), F32)], axis=1)

    causal = jnp.tril(jnp.ones((CHUNK, CHUNK), F32))
    wcat = jnp.transpose(p["sgu_w"][i] * causal, (1, 0, 2)).reshape(CHUNK, SGU_HEADS * CHUNK)
    sbias = jnp.broadcast_to(jnp.transpose(p["sgu_b"][i])[:, :, None],
                             (CHUNK, SGU_HEADS, HEAD_DIM)).reshape(CHUNK, SGU_WIDTH)

    w_uq = p["w_uq"][i].reshape(Q_RANK, MLA_HEADS, MLA_QK)
    w_uq = jnp.pad(w_uq, ((0, 0), (0, 0), (0, HEAD_PAD - MLA_QK))).reshape(Q_RANK, MLA_HEADS * HEAD_PAD)
    qn_g = p["q_norm_g"][i]
    q_gain = jnp.tile(jnp.concatenate([qn_g, jnp.zeros((HEAD_PAD - MLA_QK,), F32)]), MLA_HEADS)
    w_ukv = p["w_ukv"][i].reshape(KV_RANK, MLA_HEADS, MLA_NOPE + MLA_V)
    w_ukt = jnp.transpose(w_ukv[:, :, :MLA_NOPE], (1, 2, 0)).reshape(MLA_HEADS * MLA_NOPE, KV_RANK)
    w_v = w_ukv[:, :, MLA_NOPE:].reshape(KV_RANK, MLA_HEADS // 2, 2, MLA_V)
    zeros_v = jnp.zeros((KV_RANK, MLA_HEADS // 2, MLA_V), F32)
    w_uv = jnp.stack([w_v[:, :, 0], zeros_v, zeros_v, w_v[:, :, 1]], axis=2).reshape(KV_RANK, MLA_HEADS * HEAD_PAD)
    v_ones = jnp.tile(jnp.repeat(jnp.array([0.0, 1.0, 1.0, 0.0], F32), MLA_V), MLA_HEADS // 2)
    kn_g = p["k_norm_g"][i]

    bg = p["branch_norm_g"][i]
    w_o = p["w_o"][i]
    wr = jnp.concatenate([p["router_group_w"][i], jnp.zeros((D_MODEL, 8 - N_GROUPS), F32),
                          p["router_expert_w"][i], jnp.zeros((D_MODEL, LANES - 8 - N_EXPERTS), F32)], axis=1)
    wr_hi = wr.astype(BF16)
    br = jnp.concatenate([p["router_group_b"][i], jnp.full((8 - N_GROUPS,), NEG_BIG, F32),
                          p["router_expert_b"][i], jnp.zeros((LANES - 8 - N_EXPERTS,), F32)])
    return {
        "mix_g": row(p["mix_norm_g"][i]),
        "w_sgu": w_in[:, :o_sgu].astype(BF16),
        "w_conv": w_in[:, o_sgu:o_conv].astype(BF16),
        "w_cq": w_in[:, o_conv:o_q].astype(BF16),
        "w_ckv": w_ckv.astype(BF16),
        "sgu_ln_g": row(p["sgu_ln_g"][i]), "sgu_ln_b": row(p["sgu_ln_b"][i]),
        "sgu_wcat": wcat.astype(BF16), "sgu_bias": sbias, "g64": consts["g64"], "bg_a": row(bg[:SGU_WIDTH]),
        "conv_w": jnp.pad(p["conv_w"][i], ((0, 1), (0, 0))), "conv_b": row(p["conv_b"][i]),
        "conv_ln_g": row(p["conv_ln_g"][i]), "conv_ln_b": row(p["conv_ln_b"][i]),
        "pw_w": p["conv_pw_w"][i].astype(BF16), "pw_b": row(p["conv_pw_b"][i]),
        "bg_b": row(bg[SGU_WIDTH:SGU_WIDTH + CONV_WIDTH]),
        "qa_g": row(p["q_a_norm_g"][i]), "w_uq": w_uq.astype(BF16), "gqk": consts["gqk"],
        "q_gain": row(q_gain),
        "kva_g": row(p["kv_a_norm_g"][i]), "w_ukt": w_ukt.astype(BF16), "w_uv": w_uv.astype(BF16),
        "v_ones": row(v_ones),
        "kn_g": jnp.broadcast_to(kn_g[:MLA_NOPE, None], (MLA_NOPE, MIX_TILE)),
        "kpe_g": jnp.broadcast_to(kn_g[MLA_NOPE:, None], (MLA_ROPE, MIX_TILE)),
        "bg_c": row(bg[SGU_WIDTH + CONV_WIDTH:]),
        "w_o_a": w_o[:SGU_WIDTH].astype(BF16),
        "w_o_b": w_o[SGU_WIDTH:SGU_WIDTH + CONV_WIDTH].astype(BF16),
        "w_o_c": w_o[SGU_WIDTH + CONV_WIDTH:].astype(BF16),
        "ffn_g": row(p["ffn_norm_g"][i]),
        "wr_split": jnp.concatenate([wr_hi, (wr - wr_hi.astype(F32)).astype(BF16)], axis=1), "br": row(br),
        "layer": i, "w_gate": p["moe_w_gate"], "w_up": p["moe_w_up"], "w_down": p["moe_w_down"],
        "ple_g": row(p["ple_norm_g"][i]), "gate_w": p["ple_gate_w"][i].astype(BF16),
        "proj_w": p["ple_proj_w"][i].astype(BF16), "post_g": row(p["ple_post_norm_g"][i]),
    }


def _moe(h1, m, eidx, wts, pl_i, lw, consts):
    n = h1.shape[0]
    total = TOP_K * n
    rows = total + N_EXPERTS * EXPERT_ROWS
    dest2d, counts = _moe_rank(eidx[:TOP_K].reshape(1, total), consts)
    dest = dest2d.reshape(total)
    cnt = counts[:, 0].astype(jnp.int32)
    padded = (cnt + EXPERT_ROWS - 1) // EXPERT_ROWS * EXPERT_ROWS
    pend = jnp.cumsum(padded)
    nblk = rows // EXPERT_ROWS
    blk_start = jnp.arange(nblk, dtype=jnp.int32) * EXPERT_ROWS
    blk_e = jnp.minimum(jnp.sum((pend[None, :] <= blk_start[:, None]).astype(jnp.int32), axis=1), N_EXPERTS - 1)
    nact = (pend[-1:] // EXPERT_ROWS).astype(jnp.int32)
    region = jnp.concatenate([jnp.zeros((1,), jnp.int32), pend.astype(jnp.int32)])
    xs = _moe_dispatch(dest, cnt, region, m, rows)
    y = _moe_experts(blk_e, nact, xs, lw["layer"], lw["w_gate"], lw["w_up"], lw["w_down"])
    return _combine_ple(dest, h1, wts, pl_i, y, lw)


def kernel(x, p, positions, mix_norm_g, w_in, sgu_ln_g, sgu_ln_b, sgu_w, sgu_b, conv_w, conv_b, conv_ln_g, conv_ln_b, conv_pw_w, conv_pw_b, q_a_norm_g, w_uq, kv_a_norm_g, w_ukv, q_norm_g, k_norm_g, branch_norm_g, w_o, ffn_norm_g, router_group_w, router_group_b, router_expert_w, router_expert_b, moe_w_gate, moe_w_up, moe_w_down, ple_norm_g, ple_gate_w, ple_proj_w, ple_post_norm_g):
    params = dict(
        mix_norm_g=mix_norm_g, w_in=w_in, sgu_ln_g=sgu_ln_g, sgu_ln_b=sgu_ln_b, sgu_w=sgu_w, sgu_b=sgu_b,
        conv_w=conv_w, conv_b=conv_b, conv_ln_g=conv_ln_g, conv_ln_b=conv_ln_b, conv_pw_w=conv_pw_w,
        conv_pw_b=conv_pw_b, q_a_norm_g=q_a_norm_g, w_uq=w_uq, kv_a_norm_g=kv_a_norm_g, w_ukv=w_ukv,
        q_norm_g=q_norm_g, k_norm_g=k_norm_g, branch_norm_g=branch_norm_g, w_o=w_o, ffn_norm_g=ffn_norm_g,
        router_group_w=router_group_w, router_group_b=router_group_b, router_expert_w=router_expert_w,
        router_expert_b=router_expert_b, moe_w_gate=moe_w_gate, moe_w_up=moe_w_up, moe_w_down=moe_w_down,
        ple_norm_g=ple_norm_g, ple_gate_w=ple_gate_w, ple_proj_w=ple_proj_w, ple_post_norm_g=ple_post_norm_g)
    batch, seq, d = x.shape
    n = batch * seq
    depth = w_in.shape[0]
    consts = _constants()
    cos, sin = _rope_tables(positions)
    h = x.reshape(n, d)
    p_rows = p.reshape(depth * n, p.shape[-1])
    for i in range(depth):
        lw = _layer_weights(i, consts, params)
        ya, yb, q, kt, v = _mixer_pre(h, cos, sin, lw, batch, seq)
        yc = _attention(q, kt, v, batch, seq)
        h1, m, eidx, wts = _outproj_router(h, ya, yb, yc, lw)
        h = _moe(h1, m, eidx, wts, p_rows, lw, consts)
    return h.reshape(batch, seq, d)
```
